```python
import math
import jax, jax.numpy as jnp
from jax import lax
import numpy as np

D_MODEL = 2048
BATCH = 4
SEQ = 4096
DEPTH = 2

CHUNK = 64
Q_BLOCK = 128
HEAD_DIM = 128
EPS = 1e-6
A_HEADS = 6
A_DIM = A_HEADS * HEAD_DIM
B_HEADS = 6
B_Q_LORA = 512
B_KV_LORA = 256
B_NOPE = 128
B_ROPE = 64
B_V = 128
B_DIM = B_HEADS * B_V
ROPE_THETA = 10000.0
C_HEADS = 4
C_QK = 64
C_V = 2 * C_QK
C_DIM = C_HEADS * C_V
REL_BUCKETS = 32
REL_MAX_DIST = 128
D_MIX = A_DIM + B_DIM + C_DIM
IN_SIZES = (A_DIM, A_DIM, A_DIM, A_HEADS, A_DIM,
            B_Q_LORA, B_KV_LORA, B_ROPE, B_DIM,
            2 * C_HEADS * C_QK, 2 * C_HEADS * C_QK, C_DIM, C_DIM)
IN_OFFSETS = tuple(int(v) for v in np.cumsum(IN_SIZES)[:-1])
N_IN = int(sum(IN_SIZES))

kernel_name = "hybrid_fox_mla_diff_stream_encoder"


def _rms_norm(x, g):
    xf = x.astype(jnp.float32)
    y = xf * lax.rsqrt(jnp.mean(xf * xf, axis=-1, keepdims=True) + EPS)
    return (y * g.astype(jnp.float32)).astype(x.dtype)


def _rope(x, pos):
    half = x.shape[-1] // 2
    inv = ROPE_THETA ** (-jnp.arange(half, dtype=jnp.float32) / half)
    ang = pos.astype(jnp.float32)[:, None] * inv[None, :]
    cos = jnp.cos(ang)[None, :, None, :].astype(x.dtype)
    sin = jnp.sin(ang)[None, :, None, :].astype(x.dtype)
    x1, x2 = x[..., :half], x[..., half:]
    return jnp.concatenate([x1 * cos - x2 * sin, x1 * sin + x2 * cos], axis=-1)


def _sweep(block_fn, seq):
    return jnp.concatenate([block_fn(i * Q_BLOCK, (i + 1) * Q_BLOCK)
                            for i in range(seq // Q_BLOCK)], axis=1)


def _chunk_mask(q0, q1):
    tq = jnp.arange(q0, q1) // CHUNK
    tk = jnp.arange(q1) // CHUNK
    return tk[None, :] <= tq[:, None]


def _frame_mask(q0, q1):
    return jnp.arange(q1)[None, :] <= jnp.arange(q0, q1)[:, None]


def _t5_bucket(rel):
    nb = REL_BUCKETS // 2
    max_exact = nb // 2
    ret = (rel > 0).astype(jnp.int32) * nb
    n = jnp.abs(rel)
    nf = jnp.maximum(n, 1).astype(jnp.float32)
    large = max_exact + (jnp.log(nf / max_exact) / math.log(REL_MAX_DIST / max_exact)
                         * (nb - max_exact)).astype(jnp.int32)
    large = jnp.minimum(large, nb - 1)
    return ret + jnp.where(n < max_exact, n, large)


def _fox_attention(q, k, v, cum_logf):
    scale = 1.0 / math.sqrt(q.shape[-1])
    cum_t = jnp.transpose(cum_logf, (0, 2, 1))

    def blk(q0, q1):
        s = jnp.einsum('bqhd,bkhd->bhqk', q[:, q0:q1], k[:, :q1]).astype(jnp.float32) * scale
        decay = cum_t[:, :, q0:q1, None] - cum_t[:, :, None, :q1]
        s = jnp.where(_frame_mask(q0, q1), s + decay, -jnp.inf)
        p = jax.nn.softmax(s, axis=-1).astype(v.dtype)
        return jnp.einsum('bhqk,bkhd->bqhd', p, v[:, :q1])

    return _sweep(blk, q.shape[1])


def _mla_attention(q_nope, q_rope, k_nope, k_rope, v):
    scale = 1.0 / math.sqrt(B_NOPE + B_ROPE)

    def blk(q0, q1):
        s = (jnp.einsum('bqhd,bkhd->bhqk', q_nope[:, q0:q1], k_nope[:, :q1])
             + jnp.einsum('bqhr,bkr->bhqk', q_rope[:, q0:q1], k_rope[:, :q1]))
        s = jnp.where(_chunk_mask(q0, q1), s.astype(jnp.float32) * scale, -jnp.inf)
        p = jax.nn.softmax(s, axis=-1).astype(v.dtype)
        return jnp.einsum('bhqk,bkhd->bqhd', p, v[:, :q1])

    return _sweep(blk, q_nope.shape[1])


def _diff_attention(q1, q2, k1, k2, v, lam, rel_bias):
    scale = 1.0 / math.sqrt(q1.shape[-1])

    def blk(q0, q1_):
        rel = jnp.arange(q1_)[None, :] - jnp.arange(q0, q1_)[:, None]
        bias = jnp.transpose(rel_bias[_t5_bucket(rel)], (2, 0, 1)).astype(jnp.float32)
        mask = _chunk_mask(q0, q1_)
        s1 = jnp.einsum('bqhd,bkhd->bhqk', q1[:, q0:q1_], k1[:, :q1_]).astype(jnp.float32) * scale + bias
        s2 = jnp.einsum('bqhd,bkhd->bhqk', q2[:, q0:q1_], k2[:, :q1_]).astype(jnp.float32) * scale + bias
        p1 = jax.nn.softmax(jnp.where(mask, s1, -jnp.inf), axis=-1)
        p2 = jax.nn.softmax(jnp.where(mask, s2, -jnp.inf), axis=-1)
        w = (p1 - lam * p2).astype(v.dtype)
        return jnp.einsum('bhqk,bkhd->bqhd', w, v[:, :q1_])

    return _sweep(blk, q1.shape[1])


def setup_inputs(seed: int = 0) -> dict:
    key = jax.random.key(seed)
    ks = jax.random.split(key, 20)
    nrm = lambda k, shape, s: jax.random.normal(k, shape, jnp.float32) * s
    return {
        "x": nrm(ks[0], (BATCH, SEQ, D_MODEL), 1.0),
        "norm_g": 1.0 + nrm(ks[1], (DEPTH, D_MODEL), 0.02),
        "w_in": nrm(ks[2], (DEPTH, D_MODEL, N_IN), D_MODEL ** -0.5),
        "b_forget": jax.random.uniform(ks[3], (DEPTH, A_HEADS), jnp.float32, 1.0, 4.0),
        "mla_q_norm_g": 1.0 + nrm(ks[4], (DEPTH, B_Q_LORA), 0.02),
        "w_uq": nrm(ks[5], (DEPTH, B_Q_LORA, B_HEADS * (B_NOPE + B_ROPE)), B_Q_LORA ** -0.5),
        "mla_kv_norm_g": 1.0 + nrm(ks[6], (DEPTH, B_KV_LORA), 0.02),
        "w_ukv": nrm(ks[7], (DEPTH, B_KV_LORA, B_HEADS * (B_NOPE + B_V)), B_KV_LORA ** -0.5),
        "lambda_q1": nrm(ks[8], (DEPTH, C_QK), 0.1),
        "lambda_k1": nrm(ks[9], (DEPTH, C_QK), 0.1),
        "lambda_q2": nrm(ks[10], (DEPTH, C_QK), 0.1),
        "lambda_k2": nrm(ks[11], (DEPTH, C_QK), 0.1),
        "diff_subln_g": 1.0 + nrm(ks[12], (DEPTH, C_V), 0.02),
        "rel_bias": nrm(ks[13], (REL_BUCKETS, C_HEADS), 0.5),
        "w_out": nrm(ks[14], (DEPTH, D_MIX, D_MODEL), D_MIX ** -0.5),
        "final_norm_g": 1.0 + nrm(ks[15], (D_MODEL,), 0.02),
    }


def reference(x, norm_g, w_in, b_forget, mla_q_norm_g, w_uq, mla_kv_norm_g, w_ukv,
              lambda_q1, lambda_k1, lambda_q2, lambda_k2, diff_subln_g, rel_bias,
              w_out, final_norm_g):
    Bsz, S, _ = x.shape
    pos = jnp.arange(S)
    for l in range(DEPTH):
        h = _rms_norm(x, norm_g[l])
        proj = jnp.einsum('bsd,dn->bsn', h, w_in[l])
        (a_q, a_k, a_v, a_f, a_gate,
         b_cq, b_ckv, b_krope, b_gate,
         c_q, c_k, c_v, c_gate) = jnp.split(proj, IN_OFFSETS, axis=-1)

        logf = jax.nn.log_sigmoid((a_f + b_forget[l]).astype(jnp.float32))
        cum_logf = jnp.cumsum(logf, axis=1)
        a_out = _fox_attention(a_q.reshape(Bsz, S, A_HEADS, HEAD_DIM),
                               a_k.reshape(Bsz, S, A_HEADS, HEAD_DIM),
                               a_v.reshape(Bsz, S, A_HEADS, HEAD_DIM), cum_logf)
        a_out = a_out.reshape(Bsz, S, A_DIM) * jax.nn.silu(a_gate)

        q = jnp.einsum('bsr,rn->bsn', _rms_norm(b_cq, mla_q_norm_g[l]), w_uq[l])
        q = q.reshape(Bsz, S, B_HEADS, B_NOPE + B_ROPE)
        q_nope, q_rope = q[..., :B_NOPE], _rope(q[..., B_NOPE:], pos)
        kv = jnp.einsum('bsr,rn->bsn', _rms_norm(b_ckv, mla_kv_norm_g[l]), w_ukv[l])
        kv = kv.reshape(Bsz, S, B_HEADS, B_NOPE + B_V)
        k_nope, b_v = kv[..., :B_NOPE], kv[..., B_NOPE:]
        k_rope = _rope(b_krope[:, :, None, :], pos)[:, :, 0, :]
        b_out = _mla_attention(q_nope, q_rope, k_nope, k_rope, b_v)
        b_out = b_out.reshape(Bsz, S, B_DIM) * jax.nn.silu(b_gate)

        lam_init = 0.8 - 0.6 * math.exp(-0.3 * l)
        lam = (jnp.exp(jnp.sum(lambda_q1[l].astype(jnp.float32) * lambda_k1[l].astype(jnp.float32)))
               - jnp.exp(jnp.sum(lambda_q2[l].astype(jnp.float32) * lambda_k2[l].astype(jnp.float32)))
               + lam_init)
        cq = c_q.reshape(Bsz, S, C_HEADS, 2, C_QK)
        ck = c_k.reshape(Bsz, S, C_HEADS, 2, C_QK)
        c_out = _diff_attention(cq[..., 0, :], cq[..., 1, :], ck[..., 0, :], ck[..., 1, :],
                                c_v.reshape(Bsz, S, C_HEADS, C_V), lam, rel_bias)
        c_out = _rms_norm(c_out, diff_subln_g[l]) * (1.0 - lam_init)
        c_out = c_out.reshape(Bsz, S, C_DIM) * jax.nn.silu(c_gate)

        mixed = jnp.concatenate([a_out, b_out, c_out], axis=-1)
        x = x + jnp.einsum('bsm,md->bsd', mixed, w_out[l])
    return _rms_norm(x, final_norm_g)
```

```python
import functools
import math

import jax
import jax.numpy as jnp
from jax import lax
from jax.experimental import pallas as pl
from jax.experimental.pallas import tpu as pltpu

F32 = jnp.float32
BF16 = jnp.bfloat16

D_MODEL = 2048
N_LAYERS = 2
CHUNK = 64
HEAD_DIM = 128
EPS = 1e-6
A_HEADS = 6
A_DIM = A_HEADS * HEAD_DIM
B_HEADS = 6
B_Q_LORA = 512
B_KV_LORA = 256
B_NOPE = 128
B_ROPE = 64
B_V = 128
B_DIM = B_HEADS * B_V
ROPE_THETA = 10000.0
C_HEADS = 4
C_QK = 64
C_V = 2 * C_QK
C_DIM = C_HEADS * C_V
REL_BUCKETS = 32
REL_MAX_DIST = 128
D_MIX = A_DIM + B_DIM + C_DIM

LANE = 128
V7X_VMEM_BYTES = 64 * 1024 * 1024
VMEM_LIMIT = 56 * 1024 * 1024

LOG2E = math.log2(math.e)
NEG_BIG = -1e30

TN = 768
SLABS_PER_TILE = TN // LANE
N_BF_SLABS = 30
N_F_SLABS = 24
N_BF_TILES = N_BF_SLABS // SLABS_PER_TILE
N_F_TILES = N_F_SLABS // SLABS_PER_TILE
N_PROJ = (N_BF_SLABS + N_F_SLABS) * LANE
SLAB_AQ, SLAB_AK, SLAB_AV, SLAB_CQ, SLAB_CK, SLAB_CV = 0, 6, 12, 18, 22, 26
FSLAB_AGATE, FSLAB_BGATE, FSLAB_CGATE, FSLAB_CQ, FSLAB_CKV, FSLAB_KROPE = 0, 6, 12, 16, 20, 22
AF_LANE = 64

TM_PROJ = 512
TQ = 512
TK = 512


def _cparams(n_grid):
    return pltpu.CompilerParams(dimension_semantics=("arbitrary",) * n_grid,
                                vmem_limit_bytes=VMEM_LIMIT)


def _prep_w_in(w):
    o = 0
    seg = {}
    for name, size in (("a_q", A_DIM), ("a_k", A_DIM), ("a_v", A_DIM), ("a_f", A_HEADS), ("a_gate", A_DIM),
                       ("b_cq", B_Q_LORA), ("b_ckv", B_KV_LORA), ("b_krope", B_ROPE), ("b_gate", B_DIM),
                       ("c_q", C_DIM), ("c_k", C_DIM), ("c_v", C_DIM), ("c_gate", C_DIM)):
        seg[name] = w[:, o:o + size]
        o += size
    half = B_ROPE // 2
    kr = seg["b_krope"]
    kr_rot = jnp.concatenate([-kr[:, half:], kr[:, :half]], axis=1)
    z = lambda n: jnp.zeros((w.shape[0], n), w.dtype)
    cols = [seg["a_q"], seg["a_k"], seg["a_v"], seg["c_q"], seg["c_k"], seg["c_v"],
            seg["a_gate"], seg["b_gate"], seg["c_gate"], seg["b_cq"], seg["b_ckv"],
            kr, z(LANE - B_ROPE),
            kr_rot, seg["a_f"], z(LANE - B_ROPE - A_HEADS)]
    out = jnp.concatenate(cols, axis=1)
    assert out.shape[1] == N_PROJ
    return out.astype(BF16)


def _proj_col_scale():
    s = jnp.ones((N_PROJ,), F32)
    s = s.at[SLAB_AQ * LANE:(SLAB_AQ + A_HEADS) * LANE].set(LOG2E / math.sqrt(HEAD_DIM))
    s = s.at[SLAB_CQ * LANE:(SLAB_CQ + C_HEADS) * LANE].set(LOG2E / math.sqrt(C_QK))
    return s.reshape(1, N_PROJ)


def _prep_w_uq(w):
    w = w.reshape(B_Q_LORA, B_HEADS, B_NOPE + B_ROPE)
    nope = w[:, :, :B_NOPE].reshape(B_Q_LORA, B_HEADS * B_NOPE)
    rope = w[:, :, B_NOPE:]
    half = B_ROPE // 2
    rot = jnp.concatenate([-rope[:, :, half:], rope[:, :, :half]], axis=2)
    pad = jnp.zeros((B_Q_LORA, B_HEADS, LANE - B_ROPE), w.dtype)
    rope_p = jnp.concatenate([rope, pad], axis=2).reshape(B_Q_LORA, B_HEADS * LANE)
    rot_p = jnp.concatenate([rot, pad], axis=2).reshape(B_Q_LORA, B_HEADS * LANE)
    return jnp.concatenate([nope, rope_p, rot_p], axis=1).astype(BF16)


def _prep_w_ukv(w):
    w = w.reshape(B_KV_LORA, B_HEADS, B_NOPE + B_V)
    k = w[:, :, :B_NOPE].reshape(B_KV_LORA, B_HEADS * B_NOPE)
    v = w[:, :, B_NOPE:].reshape(B_KV_LORA, B_HEADS * B_V)
    return jnp.concatenate([k, v], axis=1).astype(BF16)


def _rope_tables(seq):
    half = B_ROPE // 2
    inv = ROPE_THETA ** (-jnp.arange(half, dtype=F32) / half)
    ang = jnp.arange(seq).astype(F32)[:, None] * inv[None, :]
    pad = jnp.zeros((seq, LANE - B_ROPE), F32)
    cos = jnp.concatenate([jnp.cos(ang), jnp.cos(ang), pad], axis=1)
    sin = jnp.concatenate([jnp.sin(ang), jnp.sin(ang), pad], axis=1)
    return cos, sin


def _t5_bucket(rel):
    nb = REL_BUCKETS // 2
    max_exact = nb // 2
    ret = (rel > 0).astype(jnp.int32) * nb
    n = jnp.abs(rel)
    nf = jnp.maximum(n, 1).astype(F32)
    large = max_exact + (jnp.log(nf / max_exact) / math.log(REL_MAX_DIST / max_exact)
                         * (nb - max_exact)).astype(jnp.int32)
    large = jnp.minimum(large, nb - 1)
    return ret + jnp.where(n < max_exact, n, large)


def _t5_bias_tiles(rel_bias):
    assert TQ == TK and TQ > REL_MAX_DIST
    kb = jnp.arange(TK)[:, None]
    qa = jnp.arange(TQ)[None, :]
    tiles = [rel_bias[_t5_bucket(kb - qa - delta)] for delta in (0, TQ)]
    t = jnp.stack(tiles) - rel_bias[REL_BUCKETS // 2 - 1][None, None, None, :]
    return (jnp.transpose(t, (3, 0, 1, 2)) * LOG2E).astype(F32)


def _inproj_kernel(x_ref, g_ref, w_ref, cs_ref, ob_ref, of_ref, h_scr):
    j = pl.program_id(1)

    @pl.when(j == 0)
    def _():
        x = x_ref[...]
        ms = jnp.mean(x * x, axis=-1, keepdims=True)
        h_scr[...] = (x * lax.rsqrt(ms + EPS) * g_ref[...]).astype(BF16)

    acc = jnp.dot(h_scr[...], w_ref[...], preferred_element_type=F32) * cs_ref[...]

    @pl.when(j < N_BF_TILES)
    def _():
        for s in range(SLABS_PER_TILE):
            ob_ref[s] = acc[:, s * LANE:(s + 1) * LANE].astype(BF16)

    @pl.when(j >= N_BF_TILES)
    def _():
        for s in range(SLABS_PER_TILE):
            of_ref[s] = acc[:, s * LANE:(s + 1) * LANE]


def _inproj(x2, g, w, cs):
    m = x2.shape[0]
    tm = TM_PROJ
    return pl.pallas_call(
        _inproj_kernel,
        grid=(m // tm, N_BF_TILES + N_F_TILES),
        in_specs=[
            pl.BlockSpec((tm, D_MODEL), lambda i, j: (i, 0)),
            pl.BlockSpec((1, D_MODEL), lambda i, j: (0, 0)),
            pl.BlockSpec((D_MODEL, TN), lambda i, j: (0, j)),
            pl.BlockSpec((1, TN), lambda i, j: (0, j)),
        ],
        out_specs=[
            pl.BlockSpec((SLABS_PER_TILE, tm, LANE), lambda i, j: (jnp.minimum(j, N_BF_TILES - 1), i, 0)),
            pl.BlockSpec((SLABS_PER_TILE, tm, LANE), lambda i, j: (jnp.maximum(j - N_BF_TILES, 0), i, 0)),
        ],
        out_shape=[jax.ShapeDtypeStruct((N_BF_SLABS, m, LANE), BF16),
                   jax.ShapeDtypeStruct((N_F_SLABS, m, LANE), F32)],
        scratch_shapes=[pltpu.VMEM((tm, D_MODEL), BF16)],
        compiler_params=_cparams(2),
        name="inproj",
    )(x2, g, w, cs)


def _rms(x, g):
    return x * lax.rsqrt(jnp.mean(x * x, axis=-1, keepdims=True) + EPS) * g


def _mla_prep_kernel(cq_ref, ckv_ref, kr_ref, gq_ref, gkv_ref, wq_ref, wkv_ref, cos_ref, sin_ref,
                     qo_ref, ko_ref, vo_ref):
    scale = LOG2E / math.sqrt(B_NOPE + B_ROPE)
    cos = cos_ref[...]
    sin = sin_ref[...]
    cq = jnp.concatenate([cq_ref[s] for s in range(B_Q_LORA // LANE)], axis=1)
    q = jnp.dot(_rms(cq, gq_ref[...]).astype(BF16), wq_ref[...], preferred_element_type=F32)
    ckv = jnp.concatenate([ckv_ref[s] for s in range(B_KV_LORA // LANE)], axis=1)
    kv = jnp.dot(_rms(ckv, gkv_ref[...]).astype(BF16), wkv_ref[...], preferred_element_type=F32)
    k_rope = (kr_ref[0] * cos + kr_ref[1] * sin).astype(BF16)
    n_h = B_HEADS * LANE
    for h in range(B_HEADS):
        sl = slice(h * LANE, (h + 1) * LANE)
        q_rope = q[:, n_h + h * LANE:n_h + (h + 1) * LANE] * cos + q[:, 2 * n_h + h * LANE:2 * n_h + (h + 1) * LANE] * sin
        qo_ref[h, :, :LANE] = (q[:, sl] * scale).astype(BF16)
        qo_ref[h, :, LANE:] = (q_rope * scale).astype(BF16)
        ko_ref[h, :, :LANE] = kv[:, sl].astype(BF16)
        ko_ref[h, :, LANE:] = k_rope
        vo_ref[h] = kv[:, n_h + h * LANE:n_h + (h + 1) * LANE].astype(BF16)


def _mla_prep(of, gq, gkv, wq, wkv, cos, sin, seq):
    m = of.shape[1]
    tm = TM_PROJ
    n_pos = seq // tm
    nq = B_Q_LORA // LANE
    nkv = B_KV_LORA // LANE
    return pl.pallas_call(
        _mla_prep_kernel,
        grid=(m // tm,),
        in_specs=[
            pl.BlockSpec((nq, tm, LANE), lambda i: (FSLAB_CQ // nq, i, 0)),
            pl.BlockSpec((nkv, tm, LANE), lambda i: (FSLAB_CKV // nkv, i, 0)),
            pl.BlockSpec((2, tm, LANE), lambda i: (FSLAB_KROPE // 2, i, 0)),
            pl.BlockSpec((1, B_Q_LORA), lambda i: (0, 0)),
            pl.BlockSpec((1, B_KV_LORA), lambda i: (0, 0)),
            pl.BlockSpec(wq.shape, lambda i: (0, 0)),
            pl.BlockSpec(wkv.shape, lambda i: (0, 0)),
            pl.BlockSpec((tm, LANE), lambda i: (i % n_pos, 0)),
            pl.BlockSpec((tm, LANE), lambda i: (i % n_pos, 0)),
        ],
        out_specs=[
            pl.BlockSpec((B_HEADS, tm, 2 * LANE), lambda i: (0, i, 0)),
            pl.BlockSpec((B_HEADS, tm, 2 * LANE), lambda i: (0, i, 0)),
            pl.BlockSpec((B_HEADS, tm, LANE), lambda i: (0, i, 0)),
        ],
        out_shape=[jax.ShapeDtypeStruct((B_HEADS, m, 2 * LANE), BF16),
                   jax.ShapeDtypeStruct((B_HEADS, m, 2 * LANE), BF16),
                   jax.ShapeDtypeStruct((B_HEADS, m, LANE), BF16)],
        compiler_params=_cparams(1),
        name="mla_prep",
    )(of, of, of, gq, gkv, wq, wkv, cos, sin)


CUM_CHUNK = 128


def _fox_bias_kernel(af_ref, bf_ref, sel_ref, kb_ref, cum_scr):
    seq = af_ref.shape[1]
    r = lax.broadcasted_iota(jnp.int32, (CUM_CHUNK, CUM_CHUNK), 0)
    c = lax.broadcasted_iota(jnp.int32, (CUM_CHUNK, CUM_CHUNK), 1)
    tri = (r >= c).astype(F32)
    bvec = bf_ref[...]

    def body(t, carry):
        r0 = pl.multiple_of(t * CUM_CHUNK, CUM_CHUNK)
        x = af_ref[0, pl.ds(r0, CUM_CHUNK), :] + bvec
        logf = jnp.minimum(x, 0.0) - jnp.log1p(jnp.exp(-jnp.abs(x)))
        cs = jnp.dot(tri, logf, preferred_element_type=F32, precision=lax.Precision.HIGHEST) + carry
        cum_scr[pl.ds(r0, CUM_CHUNK), :] = cs
        return cs[CUM_CHUNK - 1:CUM_CHUNK, :]

    lax.fori_loop(0, seq // CUM_CHUNK, body, jnp.zeros((1, LANE), F32))
    bias = cum_scr[...] * (-LOG2E)
    hi = bias.astype(BF16)
    r1 = bias - hi.astype(F32)
    mid = r1.astype(BF16)
    lo = (r1 - mid.astype(F32)).astype(BF16)
    parts = jnp.concatenate([hi, mid, lo], axis=1)
    kb = jnp.dot(parts, sel_ref[...], preferred_element_type=F32)
    for h in range(A_HEADS):
        kb_ref[h, 0] = kb[:, h * LANE:(h + 1) * LANE].astype(BF16)


def _fox_bias(of, b_forget, batch, seq):
    bvec = jnp.zeros((1, LANE), F32).at[0, AF_LANE:AF_LANE + A_HEADS].set(b_forget)
    rows = jnp.arange(3 * LANE)
    part, lane = rows // LANE, rows % LANE
    head = lane - AF_LANE
    col = jnp.where((head >= 0) & (head < A_HEADS), head * LANE + part, -1)
    sel = (col[:, None] == jnp.arange(A_HEADS * LANE)[None, :]).astype(BF16)
    of4 = of.reshape(N_F_SLABS, batch, seq, LANE)
    return pl.pallas_call(
        _fox_bias_kernel,
        grid=(batch,),
        in_specs=[
            pl.BlockSpec((None, 1, seq, LANE), lambda b: (N_F_SLABS - 1, b, 0, 0)),
            pl.BlockSpec((1, LANE), lambda b: (0, 0)),
            pl.BlockSpec(sel.shape, lambda b: (0, 0)),
        ],
        out_specs=pl.BlockSpec((A_HEADS, 1, seq, LANE), lambda b: (0, b, 0, 0)),
        out_shape=jax.ShapeDtypeStruct((A_HEADS, batch, seq, LANE), BF16),
        scratch_shapes=[pltpu.VMEM((seq, LANE), F32)],
        compiler_params=_cparams(1),
        name="fox_bias",
    )(of4, bvec, sel)


def _nt_dot(a, b):
    return lax.dot_general(a, b, (((1,), (1,)), ((), ())), preferred_element_type=F32)


def _softmax_update(s, vt, m_scr, l_scr, acc_scr):
    m_prev = m_scr[...]
    m_new = jnp.maximum(m_prev, jnp.max(s, axis=0, keepdims=True))
    alpha = jnp.exp2(m_prev - m_new)
    p = jnp.exp2(s - m_new)
    l_scr[...] = alpha * l_scr[...] + jnp.sum(p, axis=0, keepdims=True)
    acc_scr[...] = alpha * acc_scr[...] + jnp.dot(vt, p.astype(BF16), preferred_element_type=F32)
    m_scr[...] = m_new


def _init_state(m_scr, l_scr, acc_scr):
    m_scr[...] = jnp.full(m_scr.shape, NEG_BIG, F32)
    l_scr[...] = jnp.zeros(l_scr.shape, F32)
    acc_scr[...] = jnp.zeros(acc_scr.shape, F32)


def _tile_iotas():
    return (lax.broadcasted_iota(jnp.int32, (TK, TQ), 0), lax.broadcasted_iota(jnp.int32, (TK, TQ), 1))


def _silu(g):
    return g / (1.0 + jnp.exp(-g))


def _load_vt(v_ref, vt_scr):
    for j in range(vt_scr.shape[0]):
        vt_scr[j] = v_ref[j * TK:(j + 1) * TK, :].T


def _fox_attn_kernel(q_ref, k_ref, kb_ref, v_ref, g_ref, o_ref, vt_scr, m_scr, l_scr, acc_scr):
    seq = q_ref.shape[0]
    _load_vt(v_ref, vt_scr)
    ones3 = jnp.where(lax.broadcasted_iota(jnp.int32, (TQ, LANE), 1) < 3, 1.0, 0.0).astype(BF16)
    key_i, qry_i = _tile_iotas()

    def scores(q_cat, k0):
        k_cat = jnp.concatenate([k_ref[pl.ds(k0, TK), :], kb_ref[pl.ds(k0, TK), :]], axis=1)
        return _nt_dot(k_cat, q_cat)

    def q_body(qi, carry):
        q0 = pl.multiple_of(qi * TQ, TQ)
        q_cat = jnp.concatenate([q_ref[pl.ds(q0, TQ), :], ones3], axis=1)
        _init_state(m_scr, l_scr, acc_scr)

        def k_body(kj, c):
            s = scores(q_cat, pl.multiple_of(kj * TK, TK))
            _softmax_update(s, vt_scr[kj], m_scr, l_scr, acc_scr)
            return c

        lax.fori_loop(0, qi, k_body, 0)
        s = jnp.where(key_i <= qry_i, scores(q_cat, q0), NEG_BIG)
        _softmax_update(s, vt_scr[qi], m_scr, l_scr, acc_scr)
        out = (acc_scr[...] / l_scr[...]).T
        o_ref[pl.ds(q0, TQ), :] = (out * _silu(g_ref[pl.ds(q0, TQ), :])).astype(BF16)
        return carry

    lax.fori_loop(0, seq // TQ, q_body, 0)


def _attn_scratch(seq):
    return [pltpu.VMEM((seq // TK, HEAD_DIM, TK), BF16),
            pltpu.VMEM((1, TQ), F32), pltpu.VMEM((1, TQ), F32), pltpu.VMEM((HEAD_DIM, TQ), F32)]


def _head_spec(width, slab0, n_heads, seq):
    return pl.BlockSpec((None, None, seq, width), lambda g: (slab0 + g % n_heads, g // n_heads, 0, 0))


def _fox_attn(ob4, kb4, of4, batch, seq):
    hs = functools.partial(_head_spec, n_heads=A_HEADS, seq=seq)
    return pl.pallas_call(
        _fox_attn_kernel,
        grid=(batch * A_HEADS,),
        in_specs=[hs(LANE, SLAB_AQ), hs(LANE, SLAB_AK), hs(LANE, 0), hs(LANE, SLAB_AV), hs(LANE, FSLAB_AGATE)],
        out_specs=hs(LANE, 0),
        out_shape=jax.ShapeDtypeStruct((A_HEADS, batch, seq, LANE), BF16),
        scratch_shapes=_attn_scratch(seq),
        compiler_params=_cparams(1),
        name="fox_attn",
    )(ob4, ob4, kb4, ob4, of4)


def _mla_attn_kernel(q_ref, k_ref, v_ref, g_ref, o_ref, vt_scr, m_scr, l_scr, acc_scr):
    seq = q_ref.shape[0]
    _load_vt(v_ref, vt_scr)
    key_i, qry_i = _tile_iotas()
    chunk_shift = CHUNK.bit_length() - 1

    def q_body(qi, carry):
        q0 = pl.multiple_of(qi * TQ, TQ)
        q = q_ref[pl.ds(q0, TQ), :]
        _init_state(m_scr, l_scr, acc_scr)

        def k_body(kj, c):
            k0 = pl.multiple_of(kj * TK, TK)
            _softmax_update(_nt_dot(k_ref[pl.ds(k0, TK), :], q), vt_scr[kj], m_scr, l_scr, acc_scr)
            return c

        lax.fori_loop(0, qi, k_body, 0)
        s = _nt_dot(k_ref[pl.ds(q0, TK), :], q)
        s = jnp.where((key_i >> chunk_shift) <= (qry_i >> chunk_shift), s, NEG_BIG)
        _softmax_update(s, vt_scr[qi], m_scr, l_scr, acc_scr)
        out = (acc_scr[...] / l_scr[...]).T
        o_ref[pl.ds(q0, TQ), :] = (out * _silu(g_ref[pl.ds(q0, TQ), :])).astype(BF16)
        return carry

    lax.fori_loop(0, seq // TQ, q_body, 0)


def _mla_attn(q4, k4, v4, of4, batch, seq):
    hs = functools.partial(_head_spec, n_heads=B_HEADS, seq=seq)
    return pl.pallas_call(
        _mla_attn_kernel,
        grid=(batch * B_HEADS,),
        in_specs=[hs(2 * LANE, 0), hs(2 * LANE, 0), hs(LANE, 0), hs(LANE, FSLAB_BGATE)],
        out_specs=hs(LANE, 0),
        out_shape=jax.ShapeDtypeStruct((B_HEADS, batch, seq, LANE), BF16),
        scratch_shapes=_attn_scratch(seq),
        compiler_params=_cparams(1),
        name="mla_attn",
    )(q4, k4, v4, of4)


def _diff_attn_kernel(q_ref, k_ref, v_ref, g_ref, bias_ref, lq1_ref, lk1_ref, lq2_ref, lk2_ref, sg_ref,
                      o_ref, vt_scr, m1_scr, l1_scr, acc1_scr, m2_scr, l2_scr, acc2_scr, *, lam_init):
    seq = q_ref.shape[0]
    _load_vt(v_ref, vt_scr)
    key_i, qry_i = _tile_iotas()
    chunk_shift = CHUNK.bit_length() - 1
    lane = lax.broadcasted_iota(jnp.int32, (TQ, LANE), 1)
    lam = (jnp.exp(jnp.sum(lq1_ref[...] * lk1_ref[...], axis=-1, keepdims=True))
           - jnp.exp(jnp.sum(lq2_ref[...] * lk2_ref[...], axis=-1, keepdims=True)) + lam_init)
    st1 = (m1_scr, l1_scr, acc1_scr)
    st2 = (m2_scr, l2_scr, acc2_scr)

    def q_body(qi, carry):
        q0 = pl.multiple_of(qi * TQ, TQ)
        q = q_ref[pl.ds(q0, TQ), :]
        zero = jnp.zeros_like(q)
        q1 = jnp.where(lane < C_QK, q, zero)
        q2 = jnp.where(lane >= C_QK, q, zero)
        _init_state(*st1)
        _init_state(*st2)

        def block(kj, bias, mask):
            k = k_ref[pl.ds(pl.multiple_of(kj * TK, TK), TK), :]
            for qz, st in ((q1, st1), (q2, st2)):
                s = _nt_dot(k, qz)
                if bias is not None:
                    s = s + bias
                if mask is not None:
                    s = jnp.where(mask, s, NEG_BIG)
                _softmax_update(s, vt_scr[kj], *st)

        def k_body(kj, c):
            block(kj, None, None)
            return c

        lax.fori_loop(0, jnp.maximum(qi - 1, 0), k_body, 0)

        @pl.when(qi >= 1)
        def _():
            block(qi - 1, bias_ref[1], None)

        block(qi, bias_ref[0], (key_i >> chunk_shift) <= (qry_i >> chunk_shift))
        o = acc1_scr[...] / l1_scr[...] - lam * (acc2_scr[...] / l2_scr[...])
        o = o * lax.rsqrt(jnp.mean(o * o, axis=0, keepdims=True) + EPS)
        out = o.T * sg_ref[...] * (1.0 - lam_init)
        o_ref[pl.ds(q0, TQ), :] = (out * _silu(g_ref[pl.ds(q0, TQ), :])).astype(BF16)
        return carry

    lax.fori_loop(0, seq // TQ, q_body, 0)


def _diff_attn(ob4, of4, bias, lq1, lk1, lq2, lk2, sg, lam_init, batch, seq):
    hs = functools.partial(_head_spec, n_heads=C_HEADS, seq=seq)
    vec = lambda n: pl.BlockSpec((1, n), lambda g: (0, 0))
    return pl.pallas_call(
        functools.partial(_diff_attn_kernel, lam_init=lam_init),
        grid=(batch * C_HEADS,),
        in_specs=[hs(LANE, SLAB_CQ), hs(LANE, SLAB_CK), hs(LANE, SLAB_CV), hs(LANE, FSLAB_CGATE),
                  pl.BlockSpec((None, 2, TK, TQ), lambda g: (g % C_HEADS, 0, 0, 0)),
                  vec(C_QK), vec(C_QK), vec(C_QK), vec(C_QK), vec(C_V)],
        out_specs=hs(LANE, 0),
        out_shape=jax.ShapeDtypeStruct((C_HEADS, batch, seq, LANE), BF16),
        scratch_shapes=_attn_scratch(seq) + _attn_scratch(seq)[1:],
        compiler_params=_cparams(1),
        name="diff_attn",
    )(ob4, ob4, ob4, of4, bias, lq1, lk1, lq2, lk2, sg)


def _outproj_kernel(a_ref, b_ref, c_ref, w_ref, x_ref, g_ref, o_ref, *, final_norm):
    mixed = jnp.concatenate([a_ref[h] for h in range(A_HEADS)] + [b_ref[h] for h in range(B_HEADS)]
                            + [c_ref[h] for h in range(C_HEADS)], axis=1)
    y = x_ref[...] + jnp.dot(mixed, w_ref[...], preferred_element_type=F32)
    if final_norm:
        y = _rms(y, g_ref[...])
    o_ref[...] = y


def _outproj(mix_a, mix_b, mix_c, w, x2, g, final_norm):
    m = x2.shape[0]
    tm = TM_PROJ
    return pl.pallas_call(
        functools.partial(_outproj_kernel, final_norm=final_norm),
        grid=(m // tm,),
        in_specs=[
            pl.BlockSpec((A_HEADS, tm, LANE), lambda i: (0, i, 0)),
            pl.BlockSpec((B_HEADS, tm, LANE), lambda i: (0, i, 0)),
            pl.BlockSpec((C_HEADS, tm, LANE), lambda i: (0, i, 0)),
            pl.BlockSpec((D_MIX, D_MODEL), lambda i: (0, 0)),
            pl.BlockSpec((tm, D_MODEL), lambda i: (i, 0)),
            pl.BlockSpec((1, D_MODEL), lambda i: (0, 0)),
        ],
        out_specs=pl.BlockSpec((tm, D_MODEL), lambda i: (i, 0)),
        out_shape=jax.ShapeDtypeStruct((m, D_MODEL), F32),
        compiler_params=_cparams(1),
        name="outproj",
    )(mix_a, mix_b, mix_c, w, x2, g)


def kernel(x, norm_g, w_in, b_forget, mla_q_norm_g, w_uq, mla_kv_norm_g, w_ukv, lambda_q1, lambda_k1,
           lambda_q2, lambda_k2, diff_subln_g, rel_bias, w_out, final_norm_g):
    batch, seq, d = x.shape
    assert d == D_MODEL and seq % TQ == 0 and seq % TM_PROJ == 0
    m = batch * seq
    x2 = x.reshape(m, d)
    cs = _proj_col_scale()
    cos, sin = _rope_tables(seq)
    bias_tiles = _t5_bias_tiles(rel_bias)
    for l in range(N_LAYERS):
        ob, of = _inproj(x2, norm_g[l].reshape(1, d), _prep_w_in(w_in[l]), cs)
        ob4 = ob.reshape(N_BF_SLABS, batch, seq, LANE)
        of4 = of.reshape(N_F_SLABS, batch, seq, LANE)
        kb4 = _fox_bias(of, b_forget[l], batch, seq)
        mix_a = _fox_attn(ob4, kb4, of4, batch, seq)
        qb, kbm, vb = _mla_prep(of, mla_q_norm_g[l].reshape(1, -1), mla_kv_norm_g[l].reshape(1, -1),
                                _prep_w_uq(w_uq[l]), _prep_w_ukv(w_ukv[l]), cos, sin, seq)
        r4 = lambda a: a.reshape(a.shape[0], batch, seq, a.shape[-1])
        mix_b = _mla_attn(r4(qb), r4(kbm), r4(vb), of4, batch, seq)
        lam_init = 0.8 - 0.6 * math.exp(-0.3 * l)
        mix_c = _diff_attn(ob4, of4, bias_tiles, lambda_q1[l].reshape(1, -1), lambda_k1[l].reshape(1, -1),
                           lambda_q2[l].reshape(1, -1), lambda_k2[l].reshape(1, -1),
                           diff_subln_g[l].reshape(1, -1), lam_init, batch, seq)
        x2 = _outproj(mix_a.reshape(A_HEADS, m, LANE), mix_b.reshape(B_HEADS, m, LANE),
                      mix_c.reshape(C_HEADS, m, LANE), w_out[l].astype(BF16), x2,
                      final_norm_g.reshape(1, d), final_norm=(l == N_LAYERS - 1))
    return x2.reshape(batch, seq, d)
```

```python
import functools
import math

import jax
import jax.numpy as jnp
from jax import lax
from jax.experimental import pallas as pl
from jax.experimental.pallas import tpu as pltpu

F32 = jnp.float32
BF16 = jnp.bfloat16

D_MODEL = 2048
N_LAYERS = 2
CHUNK = 64
HEAD_DIM = 128
EPS = 1e-6
A_HEADS = 6
A_DIM = A_HEADS * HEAD_DIM
B_HEADS = 6
B_Q_LORA = 512
B_KV_LORA = 256
B_NOPE = 128
B_ROPE = 64
B_V = 128
B_DIM = B_HEADS * B_V
ROPE_THETA = 10000.0
C_HEADS = 4
C_QK = 64
C_V = 2 * C_QK
C_DIM = C_HEADS * C_V
REL_BUCKETS = 32
REL_MAX_DIST = 128
D_MIX = A_DIM + B_DIM + C_DIM

LANE = 128
V7X_VMEM_BYTES = 64 * 1024 * 1024
VMEM_LIMIT = 56 * 1024 * 1024

LOG2E = math.log2(math.e)
NEG_BIG = -1e30

TN = 768
SLABS_PER_TILE = TN // LANE
N_BF_SLABS = 30
N_F_SLABS = 24
N_BF_TILES = N_BF_SLABS // SLABS_PER_TILE
N_F_TILES = N_F_SLABS // SLABS_PER_TILE
N_PROJ = (N_BF_SLABS + N_F_SLABS) * LANE
SLAB_AQ, SLAB_AK, SLAB_AV, SLAB_CQ, SLAB_CK, SLAB_CV = 0, 6, 12, 18, 22, 26
FSLAB_AGATE, FSLAB_BGATE, FSLAB_CGATE, FSLAB_CQ, FSLAB_CKV, FSLAB_KROPE = 0, 6, 12, 16, 20, 22
AF_LANE = 64

TM_PROJ = 512
TQ = 512
TK = 512


def _cparams(n_grid):
    return pltpu.CompilerParams(dimension_semantics=("arbitrary",) * n_grid,
                                vmem_limit_bytes=VMEM_LIMIT)


def _prep_w_in(w):
    o = 0
    seg = {}
    for name, size in (("a_q", A_DIM), ("a_k", A_DIM), ("a_v", A_DIM), ("a_f", A_HEADS), ("a_gate", A_DIM),
                       ("b_cq", B_Q_LORA), ("b_ckv", B_KV_LORA), ("b_krope", B_ROPE), ("b_gate", B_DIM),
                       ("c_q", C_DIM), ("c_k", C_DIM), ("c_v", C_DIM), ("c_gate", C_DIM)):
        seg[name] = w[:, o:o + size]
        o += size
    half = B_ROPE // 2
    kr = seg["b_krope"]
    kr_rot = jnp.concatenate([-kr[:, half:], kr[:, :half]], axis=1)
    z = lambda n: jnp.zeros((w.shape[0], n), w.dtype)
    cols = [seg["a_q"], seg["a_k"], seg["a_v"], seg["c_q"], seg["c_k"], seg["c_v"],
            seg["a_gate"], seg["b_gate"], seg["c_gate"], seg["b_cq"], seg["b_ckv"],
            kr, z(LANE - B_ROPE),
            kr_rot, seg["a_f"], z(LANE - B_ROPE - A_HEADS)]
    out = jnp.concatenate(cols, axis=1)
    assert out.shape[1] == N_PROJ
    return out.astype(BF16)


def _proj_col_scale():
    s = jnp.ones((N_PROJ,), F32)
    s = s.at[SLAB_AQ * LANE:(SLAB_AQ + A_HEADS) * LANE].set(LOG2E / math.sqrt(HEAD_DIM))
    s = s.at[SLAB_CQ * LANE:(SLAB_CQ + C_HEADS) * LANE].set(LOG2E / math.sqrt(C_QK))
    return s.reshape(1, N_PROJ)


def _prep_w_uq(w):
    w = w.reshape(B_Q_LORA, B_HEADS, B_NOPE + B_ROPE)
    nope = w[:, :, :B_NOPE].reshape(B_Q_LORA, B_HEADS * B_NOPE)
    rope = w[:, :, B_NOPE:]
    half = B_ROPE // 2
    rot = jnp.concatenate([-rope[:, :, half:], rope[:, :, :half]], axis=2)
    pad = jnp.zeros((B_Q_LORA, B_HEADS, LANE - B_ROPE), w.dtype)
    rope_p = jnp.concatenate([rope, pad], axis=2).reshape(B_Q_LORA, B_HEADS * LANE)
    rot_p = jnp.concatenate([rot, pad], axis=2).reshape(B_Q_LORA, B_HEADS * LANE)
    return jnp.concatenate([nope, rope_p, rot_p], axis=1).astype(BF16)


def _prep_w_ukv(w):
    w = w.reshape(B_KV_LORA, B_HEADS, B_NOPE + B_V)
    k = w[:, :, :B_NOPE].reshape(B_KV_LORA, B_HEADS * B_NOPE)
    v = w[:, :, B_NOPE:].reshape(B_KV_LORA, B_HEADS * B_V)
    return jnp.concatenate([k, v], axis=1).astype(BF16)


def _rope_tables(seq):
    half = B_ROPE // 2
    inv = ROPE_THETA ** (-jnp.arange(half, dtype=F32) / half)
    ang = jnp.arange(seq).astype(F32)[:, None] * inv[None, :]
    pad = jnp.zeros((seq, LANE - B_ROPE), F32)
    cos = jnp.concatenate([jnp.cos(ang), jnp.cos(ang), pad], axis=1)
    sin = jnp.concatenate([jnp.sin(ang), jnp.sin(ang), pad], axis=1)
    return cos, sin


def _t5_bucket(rel):
    nb = REL_BUCKETS // 2
    max_exact = nb // 2
    ret = (rel > 0).astype(jnp.int32) * nb
    n = jnp.abs(rel)
    nf = jnp.maximum(n, 1).astype(F32)
    large = max_exact + (jnp.log(nf / max_exact) / math.log(REL_MAX_DIST / max_exact)
                         * (nb - max_exact)).astype(jnp.int32)
    large = jnp.minimum(large, nb - 1)
    return ret + jnp.where(n < max_exact, n, large)


def _t5_bias_tiles(rel_bias):
    assert TQ == TK and TQ > REL_MAX_DIST
    kb = jnp.arange(TK)[:, None]
    qa = jnp.arange(TQ)[None, :]
    bucket = jnp.stack([_t5_bucket(kb - qa - delta) for delta in (0, TQ)])
    rel = rel_bias - rel_bias[REL_BUCKETS // 2 - 1][None, :]
    t = jnp.zeros((C_HEADS,) + bucket.shape, F32)
    for b in range(REL_BUCKETS):
        t = jnp.where(bucket[None] == b, rel[b][:, None, None, None], t)
    return t * LOG2E


def _inproj_kernel(x_ref, g_ref, w_ref, cs_ref, ob_ref, of_ref, h_scr):
    j = pl.program_id(1)

    @pl.when(j == 0)
    def _():
        x = x_ref[...]
        ms = jnp.mean(x * x, axis=-1, keepdims=True)
        h_scr[...] = (x * lax.rsqrt(ms + EPS) * g_ref[...]).astype(BF16)

    acc = jnp.dot(h_scr[...], w_ref[...], preferred_element_type=F32) * cs_ref[...]

    @pl.when(j < N_BF_TILES)
    def _():
        for s in range(SLABS_PER_TILE):
            ob_ref[s] = acc[:, s * LANE:(s + 1) * LANE].astype(BF16)

    @pl.when(j >= N_BF_TILES)
    def _():
        for s in range(SLABS_PER_TILE):
            of_ref[s] = acc[:, s * LANE:(s + 1) * LANE]


def _inproj(x2, g, w, cs):
    m = x2.shape[0]
    tm = TM_PROJ
    return pl.pallas_call(
        _inproj_kernel,
        grid=(m // tm, N_BF_TILES + N_F_TILES),
        in_specs=[
            pl.BlockSpec((tm, D_MODEL), lambda i, j: (i, 0)),
            pl.BlockSpec((1, D_MODEL), lambda i, j: (0, 0)),
            pl.BlockSpec((D_MODEL, TN), lambda i, j: (0, j)),
            pl.BlockSpec((1, TN), lambda i, j: (0, j)),
        ],
        out_specs=[
            pl.BlockSpec((SLABS_PER_TILE, tm, LANE), lambda i, j: (jnp.minimum(j, N_BF_TILES - 1), i, 0)),
            pl.BlockSpec((SLABS_PER_TILE, tm, LANE), lambda i, j: (jnp.maximum(j - N_BF_TILES, 0), i, 0)),
        ],
        out_shape=[jax.ShapeDtypeStruct((N_BF_SLABS, m, LANE), BF16),
                   jax.ShapeDtypeStruct((N_F_SLABS, m, LANE), F32)],
        scratch_shapes=[pltpu.VMEM((tm, D_MODEL), BF16)],
        compiler_params=_cparams(2),
        name="inproj",
    )(x2, g, w, cs)


def _rms(x, g):
    return x * lax.rsqrt(jnp.mean(x * x, axis=-1, keepdims=True) + EPS) * g


def _mla_prep_kernel(cq_ref, ckv_ref, kr_ref, gq_ref, gkv_ref, wq_ref, wkv_ref, cos_ref, sin_ref,
                     qo_ref, ko_ref, vo_ref):
    scale = LOG2E / math.sqrt(B_NOPE + B_ROPE)
    cos = cos_ref[...]
    sin = sin_ref[...]
    cq = jnp.concatenate([cq_ref[s] for s in range(B_Q_LORA // LANE)], axis=1)
    q = jnp.dot(_rms(cq, gq_ref[...]).astype(BF16), wq_ref[...], preferred_element_type=F32)
    ckv = jnp.concatenate([ckv_ref[s] for s in range(B_KV_LORA // LANE)], axis=1)
    kv = jnp.dot(_rms(ckv, gkv_ref[...]).astype(BF16), wkv_ref[...], preferred_element_type=F32)
    k_rope = (kr_ref[0] * cos + kr_ref[1] * sin).astype(BF16)
    n_h = B_HEADS * LANE
    for h in range(B_HEADS):
        sl = slice(h * LANE, (h + 1) * LANE)
        q_rope = q[:, n_h + h * LANE:n_h + (h + 1) * LANE] * cos + q[:, 2 * n_h + h * LANE:2 * n_h + (h + 1) * LANE] * sin
        qo_ref[h, :, :LANE] = (q[:, sl] * scale).astype(BF16)
        qo_ref[h, :, LANE:] = (q_rope * scale).astype(BF16)
        ko_ref[h, :, :LANE] = kv[:, sl].astype(BF16)
        ko_ref[h, :, LANE:] = k_rope
        vo_ref[h] = kv[:, n_h + h * LANE:n_h + (h + 1) * LANE].astype(BF16)


def _mla_prep(of, gq, gkv, wq, wkv, cos, sin, seq):
    m = of.shape[1]
    tm = TM_PROJ
    n_pos = seq // tm
    nq = B_Q_LORA // LANE
    nkv = B_KV_LORA // LANE
    return pl.pallas_call(
        _mla_prep_kernel,
        grid=(m // tm,),
        in_specs=[
            pl.BlockSpec((nq, tm, LANE), lambda i: (FSLAB_CQ // nq, i, 0)),
            pl.BlockSpec((nkv, tm, LANE), lambda i: (FSLAB_CKV // nkv, i, 0)),
            pl.BlockSpec((2, tm, LANE), lambda i: (FSLAB_KROPE // 2, i, 0)),
            pl.BlockSpec((1, B_Q_LORA), lambda i: (0, 0)),
            pl.BlockSpec((1, B_KV_LORA), lambda i: (0, 0)),
            pl.BlockSpec(wq.shape, lambda i: (0, 0)),
            pl.BlockSpec(wkv.shape, lambda i: (0, 0)),
            pl.BlockSpec((tm, LANE), lambda i: (i % n_pos, 0)),
            pl.BlockSpec((tm, LANE), lambda i: (i % n_pos, 0)),
        ],
        out_specs=[
            pl.BlockSpec((B_HEADS, tm, 2 * LANE), lambda i: (0, i, 0)),
            pl.BlockSpec((B_HEADS, tm, 2 * LANE), lambda i: (0, i, 0)),
            pl.BlockSpec((B_HEADS, tm, LANE), lambda i: (0, i, 0)),
        ],
        out_shape=[jax.ShapeDtypeStruct((B_HEADS, m, 2 * LANE), BF16),
                   jax.ShapeDtypeStruct((B_HEADS, m, 2 * LANE), BF16),
                   jax.ShapeDtypeStruct((B_HEADS, m, LANE), BF16)],
        compiler_params=_cparams(1),
        name="mla_prep",
    )(of, of, of, gq, gkv, wq, wkv, cos, sin)


CUM_CHUNK = 128


def _fox_bias_kernel(af_ref, bf_ref, sel_ref, kb_ref, cum_scr):
    seq = af_ref.shape[1]
    r = lax.broadcasted_iota(jnp.int32, (CUM_CHUNK, CUM_CHUNK), 0)
    c = lax.broadcasted_iota(jnp.int32, (CUM_CHUNK, CUM_CHUNK), 1)
    tri = (r >= c).astype(F32)
    bvec = bf_ref[...]

    def body(t, carry):
        r0 = pl.multiple_of(t * CUM_CHUNK, CUM_CHUNK)
        x = af_ref[0, pl.ds(r0, CUM_CHUNK), :] + bvec
        logf = jnp.minimum(x, 0.0) - jnp.log1p(jnp.exp(-jnp.abs(x)))
        cs = jnp.dot(tri, logf, preferred_element_type=F32, precision=lax.Precision.HIGHEST) + carry
        cum_scr[pl.ds(r0, CUM_CHUNK), :] = cs
        return cs[CUM_CHUNK - 1:CUM_CHUNK, :]

    lax.fori_loop(0, seq // CUM_CHUNK, body, jnp.zeros((1, LANE), F32))
    bias = cum_scr[...] * (-LOG2E)
    hi = bias.astype(BF16)
    r1 = bias - hi.astype(F32)
    mid = r1.astype(BF16)
    lo = (r1 - mid.astype(F32)).astype(BF16)
    parts = jnp.concatenate([hi, mid, lo], axis=1)
    kb = jnp.dot(parts, sel_ref[...], preferred_element_type=F32)
    for h in range(A_HEADS):
        kb_ref[h, 0] = kb[:, h * LANE:(h + 1) * LANE].astype(BF16)


def _fox_bias(of, b_forget, batch, seq):
    bvec = jnp.zeros((1, LANE), F32).at[0, AF_LANE:AF_LANE + A_HEADS].set(b_forget)
    rows = jnp.arange(3 * LANE)
    part, lane = rows // LANE, rows % LANE
    head = lane - AF_LANE
    col = jnp.where((head >= 0) & (head < A_HEADS), head * LANE + part, -1)
    sel = (col[:, None] == jnp.arange(A_HEADS * LANE)[None, :]).astype(BF16)
    of4 = of.reshape(N_F_SLABS, batch, seq, LANE)
    return pl.pallas_call(
        _fox_bias_kernel,
        grid=(batch,),
        in_specs=[
            pl.BlockSpec((None, 1, seq, LANE), lambda b: (N_F_SLABS - 1, b, 0, 0)),
            pl.BlockSpec((1, LANE), lambda b: (0, 0)),
            pl.BlockSpec(sel.shape, lambda b: (0, 0)),
        ],
        out_specs=pl.BlockSpec((A_HEADS, 1, seq, LANE), lambda b: (0, b, 0, 0)),
        out_shape=jax.ShapeDtypeStruct((A_HEADS, batch, seq, LANE), BF16),
        scratch_shapes=[pltpu.VMEM((seq, LANE), F32)],
        compiler_params=_cparams(1),
        name="fox_bias",
    )(of4, bvec, sel)


def _nt_dot(a, b):
    return lax.dot_general(a, b, (((1,), (1,)), ((), ())), preferred_element_type=F32)


def _softmax_update(s, vt, m_scr, l_scr, acc_scr):
    m_prev = m_scr[...]
    m_new = jnp.maximum(m_prev, jnp.max(s, axis=0, keepdims=True))
    alpha = jnp.exp2(m_prev - m_new)
    p = jnp.exp2(s - m_new)
    l_scr[...] = alpha * l_scr[...] + jnp.sum(p, axis=0, keepdims=True)
    acc_scr[...] = alpha * acc_scr[...] + jnp.dot(vt, p.astype(BF16), preferred_element_type=F32)
    m_scr[...] = m_new


def _init_state(m_scr, l_scr, acc_scr):
    m_scr[...] = jnp.full(m_scr.shape, NEG_BIG, F32)
    l_scr[...] = jnp.zeros(l_scr.shape, F32)
    acc_scr[...] = jnp.zeros(acc_scr.shape, F32)


def _tile_iotas():
    return (lax.broadcasted_iota(jnp.int32, (TK, TQ), 0), lax.broadcasted_iota(jnp.int32, (TK, TQ), 1))


def _silu(g):
    return g / (1.0 + jnp.exp(-g))


def _load_vt(v_ref, vt_scr):
    for j in range(vt_scr.shape[0]):
        vt_scr[j] = v_ref[j * TK:(j + 1) * TK, :].T


def _fox_attn_kernel(q_ref, k_ref, kb_ref, v_ref, g_ref, o_ref, vt_scr, m_scr, l_scr, acc_scr):
    seq = q_ref.shape[0]
    _load_vt(v_ref, vt_scr)
    ones3 = jnp.where(lax.broadcasted_iota(jnp.int32, (TQ, LANE), 1) < 3, 1.0, 0.0).astype(BF16)
    key_i, qry_i = _tile_iotas()

    def scores(q_cat, k0):
        k_cat = jnp.concatenate([k_ref[pl.ds(k0, TK), :], kb_ref[pl.ds(k0, TK), :]], axis=1)
        return _nt_dot(k_cat, q_cat)

    def q_body(qi, carry):
        q0 = pl.multiple_of(qi * TQ, TQ)
        q_cat = jnp.concatenate([q_ref[pl.ds(q0, TQ), :], ones3], axis=1)
        _init_state(m_scr, l_scr, acc_scr)

        def k_body(kj, c):
            s = scores(q_cat, pl.multiple_of(kj * TK, TK))
            _softmax_update(s, vt_scr[kj], m_scr, l_scr, acc_scr)
            return c

        lax.fori_loop(0, qi, k_body, 0)
        s = jnp.where(key_i <= qry_i, scores(q_cat, q0), NEG_BIG)
        _softmax_update(s, vt_scr[qi], m_scr, l_scr, acc_scr)
        out = (acc_scr[...] / l_scr[...]).T
        o_ref[pl.ds(q0, TQ), :] = (out * _silu(g_ref[pl.ds(q0, TQ), :])).astype(BF16)
        return carry

    lax.fori_loop(0, seq // TQ, q_body, 0)


def _attn_scratch(seq):
    return [pltpu.VMEM((seq // TK, HEAD_DIM, TK), BF16),
            pltpu.VMEM((1, TQ), F32), pltpu.VMEM((1, TQ), F32), pltpu.VMEM((HEAD_DIM, TQ), F32)]


def _head_spec(width, slab0, n_heads, seq):
    return pl.BlockSpec((None, None, seq, width), lambda g: (slab0 + g % n_heads, g // n_heads, 0, 0))


def _fox_attn(ob4, kb4, of4, batch, seq):
    hs = functools.partial(_head_spec, n_heads=A_HEADS, seq=seq)
    return pl.pallas_call(
        _fox_attn_kernel,
        grid=(batch * A_HEADS,),
        in_specs=[hs(LANE, SLAB_AQ), hs(LANE, SLAB_AK), hs(LANE, 0), hs(LANE, SLAB_AV), hs(LANE, FSLAB_AGATE)],
        out_specs=hs(LANE, 0),
        out_shape=jax.ShapeDtypeStruct((A_HEADS, batch, seq, LANE), BF16),
        scratch_shapes=_attn_scratch(seq),
        compiler_params=_cparams(1),
        name="fox_attn",
    )(ob4, ob4, kb4, ob4, of4)


def _mla_attn_kernel(q_ref, k_ref, v_ref, g_ref, o_ref, vt_scr, m_scr, l_scr, acc_scr):
    seq = q_ref.shape[0]
    _load_vt(v_ref, vt_scr)
    key_i, qry_i = _tile_iotas()
    chunk_shift = CHUNK.bit_length() - 1

    def q_body(qi, carry):
        q0 = pl.multiple_of(qi * TQ, TQ)
        q = q_ref[pl.ds(q0, TQ), :]
        _init_state(m_scr, l_scr, acc_scr)

        def k_body(kj, c):
            k0 = pl.multiple_of(kj * TK, TK)
            _softmax_update(_nt_dot(k_ref[pl.ds(k0, TK), :], q), vt_scr[kj], m_scr, l_scr, acc_scr)
            return c

        lax.fori_loop(0, qi, k_body, 0)
        s = _nt_dot(k_ref[pl.ds(q0, TK), :], q)
        s = jnp.where((key_i >> chunk_shift) <= (qry_i >> chunk_shift), s, NEG_BIG)
        _softmax_update(s, vt_scr[qi], m_scr, l_scr, acc_scr)
        out = (acc_scr[...] / l_scr[...]).T
        o_ref[pl.ds(q0, TQ), :] = (out * _silu(g_ref[pl.ds(q0, TQ), :])).astype(BF16)
        return carry

    lax.fori_loop(0, seq // TQ, q_body, 0)


def _mla_attn(q4, k4, v4, of4, batch, seq):
    hs = functools.partial(_head_spec, n_heads=B_HEADS, seq=seq)
    return pl.pallas_call(
        _mla_attn_kernel,
        grid=(batch * B_HEADS,),
        in_specs=[hs(2 * LANE, 0), hs(2 * LANE, 0), hs(LANE, 0), hs(LANE, FSLAB_BGATE)],
        out_specs=hs(LANE, 0),
        out_shape=jax.ShapeDtypeStruct((B_HEADS, batch, seq, LANE), BF16),
        scratch_shapes=_attn_scratch(seq),
        compiler_params=_cparams(1),
        name="mla_attn",
    )(q4, k4, v4, of4)


def _diff_attn_kernel(q_ref, k_ref, v_ref, g_ref, bias_ref, lq1_ref, lk1_ref, lq2_ref, lk2_ref, sg_ref,
                      o_ref, vt_scr, m1_scr, l1_scr, acc1_scr, m2_scr, l2_scr, acc2_scr, *, lam_init):
    seq = q_ref.shape[0]
    _load_vt(v_ref, vt_scr)
    key_i, qry_i = _tile_iotas()
    chunk_shift = CHUNK.bit_length() - 1
    lane = lax.broadcasted_iota(jnp.int32, (TQ, LANE), 1)
    lam = (jnp.exp(jnp.sum(lq1_ref[...] * lk1_ref[...], axis=-1, keepdims=True))
           - jnp.exp(jnp.sum(lq2_ref[...] * lk2_ref[...], axis=-1, keepdims=True)) + lam_init)
    st1 = (m1_scr, l1_scr, acc1_scr)
    st2 = (m2_scr, l2_scr, acc2_scr)

    def q_body(qi, carry):
        q0 = pl.multiple_of(qi * TQ, TQ)
        q = q_ref[pl.ds(q0, TQ), :]
        zero = jnp.zeros_like(q)
        q1 = jnp.where(lane < C_QK, q, zero)
        q2 = jnp.where(lane >= C_QK, q, zero)
        _init_state(*st1)
        _init_state(*st2)

        def block(kj, bias, mask):
            k = k_ref[pl.ds(pl.multiple_of(kj * TK, TK), TK), :]
            for qz, st in ((q1, st1), (q2, st2)):
                s = _nt_dot(k, qz)
                if bias is not None:
                    s = s + bias
                if mask is not None:
                    s = jnp.where(mask, s, NEG_BIG)
                _softmax_update(s, vt_scr[kj], *st)

        def k_body(kj, c):
            block(kj, None, None)
            return c

        lax.fori_loop(0, jnp.maximum(qi - 1, 0), k_body, 0)

        @pl.when(qi >= 1)
        def _():
            block(qi - 1, bias_ref[1], None)

        block(qi, bias_ref[0], (key_i >> chunk_shift) <= (qry_i >> chunk_shift))
        o = acc1_scr[...] / l1_scr[...] - lam * (acc2_scr[...] / l2_scr[...])
        o = o * lax.rsqrt(jnp.mean(o * o, axis=0, keepdims=True) + EPS)
        out = o.T * sg_ref[...] * (1.0 - lam_init)
        o_ref[pl.ds(q0, TQ), :] = (out * _silu(g_ref[pl.ds(q0, TQ), :])).astype(BF16)
        return carry

    lax.fori_loop(0, seq // TQ, q_body, 0)


def _diff_attn(ob4, of4, bias, lq1, lk1, lq2, lk2, sg, lam_init, batch, seq):
    hs = functools.partial(_head_spec, n_heads=C_HEADS, seq=seq)
    vec = lambda n: pl.BlockSpec((1, n), lambda g: (0, 0))
    return pl.pallas_call(
        functools.partial(_diff_attn_kernel, lam_init=lam_init),
        grid=(batch * C_HEADS,),
        in_specs=[hs(LANE, SLAB_CQ), hs(LANE, SLAB_CK), hs(LANE, SLAB_CV), hs(LANE, FSLAB_CGATE),
                  pl.BlockSpec((None, 2, TK, TQ), lambda g: (g % C_HEADS, 0, 0, 0)),
                  vec(C_QK), vec(C_QK), vec(C_QK), vec(C_QK), vec(C_V)],
        out_specs=hs(LANE, 0),
        out_shape=jax.ShapeDtypeStruct((C_HEADS, batch, seq, LANE), BF16),
        scratch_shapes=_attn_scratch(seq) + _attn_scratch(seq)[1:],
        compiler_params=_cparams(1),
        name="diff_attn",
    )(ob4, ob4, ob4, of4, bias, lq1, lk1, lq2, lk2, sg)


def _outproj_kernel(a_ref, b_ref, c_ref, w_ref, x_ref, g_ref, o_ref, *, final_norm):
    mixed = jnp.concatenate([a_ref[h] for h in range(A_HEADS)] + [b_ref[h] for h in range(B_HEADS)]
                            + [c_ref[h] for h in range(C_HEADS)], axis=1)
    y = x_ref[...] + jnp.dot(mixed, w_ref[...], preferred_element_type=F32)
    if final_norm:
        y = _rms(y, g_ref[...])
    o_ref[...] = y


def _outproj(mix_a, mix_b, mix_c, w, x2, g, final_norm):
    m = x2.shape[0]
    tm = TM_PROJ
    return pl.pallas_call(
        functools.partial(_outproj_kernel, final_norm=final_norm),
        grid=(m // tm,),
        in_specs=[
            pl.BlockSpec((A_HEADS, tm, LANE), lambda i: (0, i, 0)),
            pl.BlockSpec((B_HEADS, tm, LANE), lambda i: (0, i, 0)),
            pl.BlockSpec((C_HEADS, tm, LANE), lambda i: (0, i, 0)),
            pl.BlockSpec((D_MIX, D_MODEL), lambda i: (0, 0)),
            pl.BlockSpec((tm, D_MODEL), lambda i: (i, 0)),
            pl.BlockSpec((1, D_MODEL), lambda i: (0, 0)),
        ],
        out_specs=pl.BlockSpec((tm, D_MODEL), lambda i: (i, 0)),
        out_shape=jax.ShapeDtypeStruct((m, D_MODEL), F32),
        compiler_params=_cparams(1),
        name="outproj",
    )(mix_a, mix_b, mix_c, w, x2, g)


def kernel(x, norm_g, w_in, b_forget, mla_q_norm_g, w_uq, mla_kv_norm_g, w_ukv, lambda_q1, lambda_k1,
           lambda_q2, lambda_k2, diff_subln_g, rel_bias, w_out, final_norm_g):
    batch, seq, d = x.shape
    assert d == D_MODEL and seq % TQ == 0 and seq % TM_PROJ == 0
    m = batch * seq
    x2 = x.reshape(m, d)
    cs = _proj_col_scale()
    cos, sin = _rope_tables(seq)
    bias_tiles = _t5_bias_tiles(rel_bias)
    for l in range(N_LAYERS):
        ob, of = _inproj(x2, norm_g[l].reshape(1, d), _prep_w_in(w_in[l]), cs)
        ob4 = ob.reshape(N_BF_SLABS, batch, seq, LANE)
        of4 = of.reshape(N_F_SLABS, batch, seq, LANE)
        kb4 = _fox_bias(of, b_forget[l], batch, seq)
        mix_a = _fox_attn(ob4, kb4, of4, batch, seq)
        qb, kbm, vb = _mla_prep(of, mla_q_norm_g[l].reshape(1, -1), mla_kv_norm_g[l].reshape(1, -1),
                                _prep_w_uq(w_uq[l]), _prep_w_ukv(w_ukv[l]), cos, sin, seq)
        r4 = lambda a: a.reshape(a.shape[0], batch, seq, a.shape[-1])
        mix_b = _mla_attn(r4(qb), r4(kbm), r4(vb), of4, batch, seq)
        lam_init = 0.8 - 0.6 * math.exp(-0.3 * l)
        mix_c = _diff_attn(ob4, of4, bias_tiles, lambda_q1[l].reshape(1, -1), lambda_k1[l].reshape(1, -1),
                           lambda_q2[l].reshape(1, -1), lambda_k2[l].reshape(1, -1),
                           diff_subln_g[l].reshape(1, -1), lam_init, batch, seq)
        x2 = _outproj(mix_a.reshape(A_HEADS, m, LANE), mix_b.reshape(B_HEADS, m, LANE),
                      mix_c.reshape(C_HEADS, m, LANE), w_out[l].astype(BF16), x2,
                      final_norm_g.reshape(1, d), final_norm=(l == N_LAYERS - 1))
    return x2.reshape(batch, seq, d)
```

```python
import functools
import math

import numpy as np
import jax
import jax.numpy as jnp
from jax import lax
from jax.experimental import pallas as pl
from jax.experimental.pallas import tpu as pltpu

F32 = jnp.float32
BF16 = jnp.bfloat16

D_MODEL = 2048
N_LAYERS = 2
CHUNK = 64
HEAD_DIM = 128
EPS = 1e-6
A_HEADS = 6
A_DIM = A_HEADS * HEAD_DIM
B_HEADS = 6
B_Q_LORA = 512
B_KV_LORA = 256
B_NOPE = 128
B_ROPE = 64
B_V = 128
B_DIM = B_HEADS * B_V
ROPE_THETA = 10000.0
C_HEADS = 4
C_QK = 64
C_V = 2 * C_QK
C_DIM = C_HEADS * C_V
REL_BUCKETS = 32
REL_MAX_DIST = 128
D_MIX = A_DIM + B_DIM + C_DIM

LANE = 128
V7X_VMEM_BYTES = 64 * 1024 * 1024
VMEM_LIMIT = 56 * 1024 * 1024

LOG2E = math.log2(math.e)
NEG_BIG = -1e30

TN = 768
SLABS_PER_TILE = TN // LANE
N_BF_SLABS = 30
N_F_SLABS = 24
N_BF_TILES = N_BF_SLABS // SLABS_PER_TILE
N_F_TILES = N_F_SLABS // SLABS_PER_TILE
N_PROJ = (N_BF_SLABS + N_F_SLABS) * LANE
SLAB_AQ, SLAB_AK, SLAB_AV, SLAB_CQ, SLAB_CK, SLAB_CV = 0, 6, 12, 18, 22, 26
FSLAB_AGATE, FSLAB_BGATE, FSLAB_CGATE, FSLAB_CQ, FSLAB_CKV, FSLAB_KROPE = 0, 6, 12, 16, 20, 22
AF_LANE = 64

TM_PROJ = 512
TM_IN = 1024
TQ = 512
TK = 512

CHUNK_SHIFT = CHUNK.bit_length() - 1
N_TILE_CHUNKS = TK // CHUNK
FOX_MASK_LANE = 3


def _chunk_onehot(row_chunk, lane, lane0):
    return jnp.where(lane - lane0 == row_chunk, 1.0, 0.0)


def _chunk_maskq(row_chunk, lane, lane0):
    c = lane - lane0
    return jnp.where((c >= 0) & (c < N_TILE_CHUNKS) & (row_chunk < c), NEG_BIG, 0.0)


def _cparams(n_grid):
    return pltpu.CompilerParams(dimension_semantics=("arbitrary",) * n_grid,
                                vmem_limit_bytes=VMEM_LIMIT)


def _prep_w_in(w):
    o = 0
    seg = {}
    for name, size in (("a_q", A_DIM), ("a_k", A_DIM), ("a_v", A_DIM), ("a_f", A_HEADS), ("a_gate", A_DIM),
                       ("b_cq", B_Q_LORA), ("b_ckv", B_KV_LORA), ("b_krope", B_ROPE), ("b_gate", B_DIM),
                       ("c_q", C_DIM), ("c_k", C_DIM), ("c_v", C_DIM), ("c_gate", C_DIM)):
        seg[name] = w[:, o:o + size]
        o += size
    half = B_ROPE // 2
    kr = seg["b_krope"]
    kr_rot = jnp.concatenate([-kr[:, half:], kr[:, :half]], axis=1)
    z = lambda n: jnp.zeros((w.shape[0], n), w.dtype)
    cols = [seg["a_q"], seg["a_k"], seg["a_v"], seg["c_q"], seg["c_k"], seg["c_v"],
            seg["a_gate"], seg["b_gate"], seg["c_gate"], seg["b_cq"], seg["b_ckv"],
            kr, z(LANE - B_ROPE),
            kr_rot, seg["a_f"], z(LANE - B_ROPE - A_HEADS)]
    out = jnp.concatenate(cols, axis=1)
    assert out.shape[1] == N_PROJ
    return out.astype(BF16)


def _proj_col_scale():
    s = jnp.ones((N_PROJ,), F32)
    s = s.at[SLAB_AQ * LANE:(SLAB_AQ + A_HEADS) * LANE].set(LOG2E / math.sqrt(HEAD_DIM))
    s = s.at[SLAB_CQ * LANE:(SLAB_CQ + C_HEADS) * LANE].set(LOG2E / math.sqrt(C_QK))
    return s.reshape(1, N_PROJ)


def _prep_w_uq(w):
    w = w.reshape(B_Q_LORA, B_HEADS, B_NOPE + B_ROPE)
    nope = w[:, :, :B_NOPE].reshape(B_Q_LORA, B_HEADS * B_NOPE)
    rope = w[:, :, B_NOPE:]
    half = B_ROPE // 2
    rot = jnp.concatenate([-rope[:, :, half:], rope[:, :, :half]], axis=2)
    pad = jnp.zeros((B_Q_LORA, B_HEADS, LANE - B_ROPE), w.dtype)
    rope_p = jnp.concatenate([rope, pad], axis=2).reshape(B_Q_LORA, B_HEADS * LANE)
    rot_p = jnp.concatenate([rot, pad], axis=2).reshape(B_Q_LORA, B_HEADS * LANE)
    return jnp.concatenate([nope, rope_p, rot_p], axis=1).astype(BF16)


def _prep_w_ukv(w):
    w = w.reshape(B_KV_LORA, B_HEADS, B_NOPE + B_V)
    k = w[:, :, :B_NOPE].reshape(B_KV_LORA, B_HEADS * B_NOPE)
    v = w[:, :, B_NOPE:].reshape(B_KV_LORA, B_HEADS * B_V)
    return jnp.concatenate([k, v], axis=1).astype(BF16)


def _rope_tables(seq):
    half = B_ROPE // 2
    inv = ROPE_THETA ** (-jnp.arange(half, dtype=F32) / half)
    ang = jnp.arange(seq).astype(F32)[:, None] * inv[None, :]
    pad = jnp.zeros((seq, LANE - B_ROPE), F32)
    cos = jnp.concatenate([jnp.cos(ang), jnp.cos(ang), pad], axis=1)
    sin = jnp.concatenate([jnp.sin(ang), jnp.sin(ang), pad], axis=1)
    return cos, sin


def _t5_bucket(rel):
    nb = REL_BUCKETS // 2
    max_exact = nb // 2
    ret = (rel > 0).astype(jnp.int32) * nb
    n = jnp.abs(rel)
    nf = jnp.maximum(n, 1).astype(F32)
    large = max_exact + (jnp.log(nf / max_exact) / math.log(REL_MAX_DIST / max_exact)
                         * (nb - max_exact)).astype(jnp.int32)
    large = jnp.minimum(large, nb - 1)
    return ret + jnp.where(n < max_exact, n, large)


def _t5_bias_tiles(rel_bias):
    assert TQ == TK and TQ > REL_MAX_DIST
    kb = jnp.arange(TK)[:, None]
    qa = jnp.arange(TQ)[None, :]
    bucket = jnp.stack([_t5_bucket(kb - qa - delta) for delta in (0, TQ)])
    rel = rel_bias - rel_bias[REL_BUCKETS // 2 - 1][None, :]
    t = jnp.zeros((C_HEADS,) + bucket.shape, F32)
    for b in range(REL_BUCKETS):
        t = jnp.where(bucket[None] == b, rel[b][:, None, None, None], t)
    far = jnp.zeros((C_HEADS, 1, TK, TQ), F32)
    return jnp.concatenate([t * LOG2E, far], axis=1)


def _inproj_kernel(x_ref, g_ref, w_ref, cs_ref, ob_ref, of_ref, h_scr):
    j = pl.program_id(1)

    @pl.when(j == 0)
    def _():
        x = x_ref[...]
        ms = jnp.mean(x * x, axis=-1, keepdims=True)
        h_scr[...] = (x * lax.rsqrt(ms + EPS) * g_ref[...]).astype(BF16)

    acc = jnp.dot(h_scr[...], w_ref[...], preferred_element_type=F32) * cs_ref[...]

    @pl.when(j < N_BF_TILES)
    def _():
        for s in range(SLABS_PER_TILE):
            ob_ref[s] = acc[:, s * LANE:(s + 1) * LANE].astype(BF16)

    @pl.when(j >= N_BF_TILES)
    def _():
        for s in range(SLABS_PER_TILE):
            of_ref[s] = acc[:, s * LANE:(s + 1) * LANE]


def _inproj(x2, g, w, cs):
    m = x2.shape[0]
    tm = TM_IN
    return pl.pallas_call(
        _inproj_kernel,
        grid=(m // tm, N_BF_TILES + N_F_TILES),
        in_specs=[
            pl.BlockSpec((tm, D_MODEL), lambda i, j: (i, 0)),
            pl.BlockSpec((1, D_MODEL), lambda i, j: (0, 0)),
            pl.BlockSpec((D_MODEL, TN), lambda i, j: (0, j)),
            pl.BlockSpec((1, TN), lambda i, j: (0, j)),
        ],
        out_specs=[
            pl.BlockSpec((SLABS_PER_TILE, tm, LANE), lambda i, j: (jnp.minimum(j, N_BF_TILES - 1), i, 0)),
            pl.BlockSpec((SLABS_PER_TILE, tm, LANE), lambda i, j: (jnp.maximum(j - N_BF_TILES, 0), i, 0)),
        ],
        out_shape=[jax.ShapeDtypeStruct((N_BF_SLABS, m, LANE), BF16),
                   jax.ShapeDtypeStruct((N_F_SLABS, m, LANE), F32)],
        scratch_shapes=[pltpu.VMEM((tm, D_MODEL), BF16)],
        compiler_params=_cparams(2),
        name="inproj",
    )(x2, g, w, cs)


def _rms(x, g):
    return x * lax.rsqrt(jnp.mean(x * x, axis=-1, keepdims=True) + EPS) * g


def _mla_prep_kernel(cq_ref, ckv_ref, kr_ref, gq_ref, gkv_ref, wq_ref, wkv_ref, cos_ref, sin_ref,
                     qo_ref, ko_ref, vo_ref, *, n_pos):
    scale = LOG2E / math.sqrt(B_NOPE + B_ROPE)
    cos = cos_ref[...]
    sin = sin_ref[...]
    cq = jnp.concatenate([cq_ref[s] for s in range(B_Q_LORA // LANE)], axis=1)
    q = jnp.dot(_rms(cq, gq_ref[...]).astype(BF16), wq_ref[...], preferred_element_type=F32)
    ckv = jnp.concatenate([ckv_ref[s] for s in range(B_KV_LORA // LANE)], axis=1)
    kv = jnp.dot(_rms(ckv, gkv_ref[...]).astype(BF16), wkv_ref[...], preferred_element_type=F32)
    tm = cos.shape[0]
    pos = (pl.program_id(0) % n_pos) * tm + lax.broadcasted_iota(jnp.int32, (tm, LANE), 0)
    chunk = (pos % TK) >> CHUNK_SHIFT
    lane = lax.broadcasted_iota(jnp.int32, (tm, LANE), 1)
    k_rope = (kr_ref[0] * cos + kr_ref[1] * sin + _chunk_onehot(chunk, lane, B_ROPE)).astype(BF16)
    mask_q = _chunk_maskq(chunk, lane, B_ROPE)
    n_h = B_HEADS * LANE
    for h in range(B_HEADS):
        sl = slice(h * LANE, (h + 1) * LANE)
        q_rope = q[:, n_h + h * LANE:n_h + (h + 1) * LANE] * cos + q[:, 2 * n_h + h * LANE:2 * n_h + (h + 1) * LANE] * sin
        qo_ref[h, :, :LANE] = (q[:, sl] * scale).astype(BF16)
        qo_ref[h, :, LANE:] = (q_rope * scale + mask_q).astype(BF16)
        ko_ref[h, :, :LANE] = kv[:, sl].astype(BF16)
        ko_ref[h, :, LANE:] = k_rope
        vo_ref[h] = kv[:, n_h + h * LANE:n_h + (h + 1) * LANE].astype(BF16)


def _mla_prep(of, gq, gkv, wq, wkv, cos, sin, seq):
    m = of.shape[1]
    tm = TM_PROJ
    n_pos = seq // tm
    nq = B_Q_LORA // LANE
    nkv = B_KV_LORA // LANE
    return pl.pallas_call(
        functools.partial(_mla_prep_kernel, n_pos=n_pos),
        grid=(m // tm,),
        in_specs=[
            pl.BlockSpec((nq, tm, LANE), lambda i: (FSLAB_CQ // nq, i, 0)),
            pl.BlockSpec((nkv, tm, LANE), lambda i: (FSLAB_CKV // nkv, i, 0)),
            pl.BlockSpec((2, tm, LANE), lambda i: (FSLAB_KROPE // 2, i, 0)),
            pl.BlockSpec((1, B_Q_LORA), lambda i: (0, 0)),
            pl.BlockSpec((1, B_KV_LORA), lambda i: (0, 0)),
            pl.BlockSpec(wq.shape, lambda i: (0, 0)),
            pl.BlockSpec(wkv.shape, lambda i: (0, 0)),
            pl.BlockSpec((tm, LANE), lambda i: (i % n_pos, 0)),
            pl.BlockSpec((tm, LANE), lambda i: (i % n_pos, 0)),
        ],
        out_specs=[
            pl.BlockSpec((B_HEADS, tm, 2 * LANE), lambda i: (0, i, 0)),
            pl.BlockSpec((B_HEADS, tm, 2 * LANE), lambda i: (0, i, 0)),
            pl.BlockSpec((B_HEADS, tm, LANE), lambda i: (0, i, 0)),
        ],
        out_shape=[jax.ShapeDtypeStruct((B_HEADS, m, 2 * LANE), BF16),
                   jax.ShapeDtypeStruct((B_HEADS, m, 2 * LANE), BF16),
                   jax.ShapeDtypeStruct((B_HEADS, m, LANE), BF16)],
        compiler_params=_cparams(1),
        name="mla_prep",
    )(of, of, of, gq, gkv, wq, wkv, cos, sin)


CUM_CHUNK = 128


def _fox_bias_kernel(af_ref, bf_ref, sel_ref, kb_ref, cum_scr):
    seq = af_ref.shape[1]
    r = lax.broadcasted_iota(jnp.int32, (CUM_CHUNK, CUM_CHUNK), 0)
    c = lax.broadcasted_iota(jnp.int32, (CUM_CHUNK, CUM_CHUNK), 1)
    tri = (r >= c).astype(F32)
    bvec = bf_ref[...]

    def body(t, carry):
        r0 = pl.multiple_of(t * CUM_CHUNK, CUM_CHUNK)
        x = af_ref[0, pl.ds(r0, CUM_CHUNK), :] + bvec
        logf = jnp.minimum(x, 0.0) - jnp.log1p(jnp.exp(-jnp.abs(x)))
        cs = jnp.dot(tri, logf, preferred_element_type=F32, precision=lax.Precision.HIGHEST) + carry
        cum_scr[pl.ds(r0, CUM_CHUNK), :] = cs
        return cs[CUM_CHUNK - 1:CUM_CHUNK, :]

    lax.fori_loop(0, seq // CUM_CHUNK, body, jnp.zeros((1, LANE), F32))
    bias = cum_scr[...] * (-LOG2E)
    hi = bias.astype(BF16)
    r1 = bias - hi.astype(F32)
    mid = r1.astype(BF16)
    lo = (r1 - mid.astype(F32)).astype(BF16)
    parts = jnp.concatenate([hi, mid, lo], axis=1)
    kb = jnp.dot(parts, sel_ref[...], preferred_element_type=F32)
    chunk = (lax.broadcasted_iota(jnp.int32, (seq, LANE), 0) % TK) >> CHUNK_SHIFT
    onehot = _chunk_onehot(chunk, lax.broadcasted_iota(jnp.int32, (seq, LANE), 1), FOX_MASK_LANE)
    for h in range(A_HEADS):
        kb_ref[h, 0] = (kb[:, h * LANE:(h + 1) * LANE] + onehot).astype(BF16)


def _fox_bias(of, b_forget, batch, seq):
    bvec = jnp.zeros((1, LANE), F32).at[0, AF_LANE:AF_LANE + A_HEADS].set(b_forget)
    rows = jnp.arange(3 * LANE)
    part, lane = rows // LANE, rows % LANE
    head = lane - AF_LANE
    col = jnp.where((head >= 0) & (head < A_HEADS), head * LANE + part, -1)
    sel = (col[:, None] == jnp.arange(A_HEADS * LANE)[None, :]).astype(BF16)
    of4 = of.reshape(N_F_SLABS, batch, seq, LANE)
    return pl.pallas_call(
        _fox_bias_kernel,
        grid=(batch,),
        in_specs=[
            pl.BlockSpec((None, 1, seq, LANE), lambda b: (N_F_SLABS - 1, b, 0, 0)),
            pl.BlockSpec((1, LANE), lambda b: (0, 0)),
            pl.BlockSpec(sel.shape, lambda b: (0, 0)),
        ],
        out_specs=pl.BlockSpec((A_HEADS, 1, seq, LANE), lambda b: (0, b, 0, 0)),
        out_shape=jax.ShapeDtypeStruct((A_HEADS, batch, seq, LANE), BF16),
        scratch_shapes=[pltpu.VMEM((seq, LANE), F32)],
        compiler_params=_cparams(1),
        name="fox_bias",
    )(of4, bvec, sel)


PAIR_Q, PAIR_K, PAIR_DIAG, PAIR_BIAS, PAIR_STATE = 0, 1, 2, 3, 4
ACC_ROWS = HEAD_DIM + 16
PAIRS_PER_TRIP = 2


def _pair_table(n_tiles):
    pad = (0, 0, 1, 0, n_tiles)
    rows = []
    for qi in range(n_tiles):
        for kj in range(qi + 1):
            rows.append((qi, kj, int(kj == qi), 0 if kj == qi else (1 if kj == qi - 1 else 2), qi))
    rows += [pad] * (-len(rows) % PAIRS_PER_TRIP)
    rows = [(0, 0, 0, 2, n_tiles)] + rows + [pad]
    return np.asarray(rows, np.int32).T.copy()


def _nt_dot(a, b):
    return lax.dot_general(a, b, (((1,), (1,)), ((), ())), preferred_element_type=F32)


def _tile_rows(ref, tile):
    return ref[pl.ds(pl.multiple_of(tile * TK, TK), TK), :]


def _load_vt(v_ref, vt_scr):
    extra = jnp.where(lax.broadcasted_iota(jnp.int32, (ACC_ROWS - HEAD_DIM, TK), 0) == 0, 1.0, 0.0).astype(BF16)
    for j in range(vt_scr.shape[0]):
        vt_scr[j, :HEAD_DIM, :] = v_ref[j * TK:(j + 1) * TK, :].T
        vt_scr[j, HEAD_DIM:, :] = extra


class _PairBufs:
    def __init__(self, s, mb, p, al):
        self.s, self.mb, self.p, self.al = s, mb, p, al


def _pair_scratch():
    return [pltpu.VMEM((TK, TQ), F32), pltpu.VMEM((1, TQ), F32), pltpu.VMEM((TK, TQ), BF16),
            pltpu.VMEM((1, TQ), F32)]


def _state_scratch(n_tiles):
    return [pltpu.VMEM((n_tiles + 1, 1, TQ), F32), pltpu.VMEM((n_tiles + 1, ACC_ROWS, TQ), F32)]


def _init_pipeline(m_all, acc_all, first):
    m_all[...] = jnp.full(m_all.shape, NEG_BIG, F32)
    acc_all[...] = jnp.zeros(acc_all.shape, F32)
    first.p[...] = jnp.zeros(first.p.shape, BF16)
    first.al[...] = jnp.ones(first.al.shape, F32)


def _pair_bufs(pair_scr):
    return [_PairBufs(*pair_scr[4 * i:4 * i + 4]) for i in range(len(pair_scr) // 4)]


def _qk_store(s, bufs):
    bufs.s[...] = s
    bufs.mb[...] = jnp.max(s, axis=0, keepdims=True)


def _softmax_stage(tab_ref, t, bufs, m_all):
    st = tab_ref[PAIR_STATE, t]
    m_prev = m_all[st]
    m_new = jnp.maximum(m_prev, bufs.mb[...])
    bufs.al[...] = jnp.exp2(m_prev - m_new)
    bufs.p[...] = jnp.exp2(bufs.s[...] - m_new).astype(BF16)
    m_all[st] = m_new


def _pv_stage(tab_ref, t, bufs, vt_scr, acc_all):
    st = tab_ref[PAIR_STATE, t]
    acc_all[st] = bufs.al[...] * acc_all[st] + jnp.dot(vt_scr[tab_ref[PAIR_K, t]], bufs.p[...],
                                                        preferred_element_type=F32)


def _run_pipeline(tab_ref, bufs, qk_stage, sm_stage, pv_stage):
    n_pairs = tab_ref.shape[1] - 2
    u = PAIRS_PER_TRIP

    def body(i, carry):
        for j in range(u):
            t = u * i + 1 + j
            pv_stage(t - 1, bufs[j % u])
            sm_stage(t, bufs[(j + 1) % u])
            qk_stage(t + 1, bufs[(j + 2) % u])
        return carry

    qk_stage(1, bufs[1 % u])
    lax.fori_loop(0, n_pairs // u, body, 0)
    pv_stage(n_pairs, bufs[0])


def _normalized(acc_all, qi):
    return acc_all[qi, :HEAD_DIM, :] * (1.0 / acc_all[qi, HEAD_DIM:HEAD_DIM + 1, :])


def _silu(g):
    return g / (1.0 + jnp.exp(-g))


def _head_spec(width, slab0, n_heads, seq):
    return pl.BlockSpec((None, None, seq, width), lambda g: (slab0 + g % n_heads, g // n_heads, 0, 0))


_SMEM_SPEC = pl.BlockSpec(memory_space=pltpu.SMEM)


def _fox_attn_kernel(tab_ref, q_ref, k_ref, kb_ref, v_ref, g_ref, o_ref, vt_scr, qext_scr, m_all, acc_all, *pair_scr):
    bufs = _pair_bufs(pair_scr)
    _load_vt(v_ref, vt_scr)
    _init_pipeline(m_all, acc_all, bufs[0])
    row = lax.broadcasted_iota(jnp.int32, (TQ, LANE), 0)
    lane = lax.broadcasted_iota(jnp.int32, (TQ, LANE), 1)
    ones3 = jnp.where(lane < FOX_MASK_LANE, 1.0, 0.0)
    qext_scr[0] = ones3.astype(BF16)
    qext_scr[1] = (ones3 + _chunk_maskq(row >> CHUNK_SHIFT, lane, FOX_MASK_LANE)).astype(BF16)
    sub_key = lax.broadcasted_iota(jnp.int32, (LANE, LANE), 0)
    sub_qry = lax.broadcasted_iota(jnp.int32, (LANE, LANE), 1)

    def qk_stage(t, bufs):
        q_cat = jnp.concatenate([_tile_rows(q_ref, tab_ref[PAIR_Q, t]), qext_scr[tab_ref[PAIR_DIAG, t]]], axis=1)
        kj = tab_ref[PAIR_K, t]
        k_cat = jnp.concatenate([_tile_rows(k_ref, kj), _tile_rows(kb_ref, kj)], axis=1)
        s = _nt_dot(k_cat, q_cat)
        masked = sub_key - sub_qry > jnp.where(tab_ref[PAIR_DIAG, t] > 0, 0, TK)
        blocks = []
        for r in range(TK // LANE):
            blk = s[r * LANE:(r + 1) * LANE, :]
            mid = jnp.where(masked, NEG_BIG, blk[:, r * LANE:(r + 1) * LANE])
            parts = ([blk[:, :r * LANE]] if r > 0 else []) + [mid]
            parts += [blk[:, (r + 1) * LANE:]] if (r + 1) * LANE < TQ else []
            blocks.append(jnp.concatenate(parts, axis=1))
        _qk_store(jnp.concatenate(blocks, axis=0), bufs)

    _run_pipeline(tab_ref, bufs, qk_stage,
                  lambda t, bufs: _softmax_stage(tab_ref, t, bufs, m_all),
                  lambda t, bufs: _pv_stage(tab_ref, t, bufs, vt_scr, acc_all))
    for qi in range(m_all.shape[0] - 1):
        rows = slice(qi * TQ, (qi + 1) * TQ)
        o_ref[rows, :] = (_normalized(acc_all, qi).T * _silu(g_ref[rows, :])).astype(BF16)


def _fox_attn(tab, ob4, kb4, of4, batch, seq):
    hs = functools.partial(_head_spec, n_heads=A_HEADS, seq=seq)
    n_tiles = seq // TQ
    return pl.pallas_call(
        _fox_attn_kernel,
        grid=(batch * A_HEADS,),
        in_specs=[_SMEM_SPEC, hs(LANE, SLAB_AQ), hs(LANE, SLAB_AK), hs(LANE, 0), hs(LANE, SLAB_AV),
                  hs(LANE, FSLAB_AGATE)],
        out_specs=hs(LANE, 0),
        out_shape=jax.ShapeDtypeStruct((A_HEADS, batch, seq, LANE), BF16),
        scratch_shapes=[pltpu.VMEM((n_tiles, ACC_ROWS, TK), BF16), pltpu.VMEM((2, TQ, LANE), BF16)]
        + _state_scratch(n_tiles) + PAIRS_PER_TRIP * _pair_scratch(),
        compiler_params=_cparams(1),
        name="fox_attn",
    )(tab, ob4, ob4, kb4, ob4, of4)


def _mla_attn_kernel(tab_ref, q_ref, k_ref, v_ref, g_ref, o_ref, vt_scr, m_all, acc_all, *pair_scr):
    bufs = _pair_bufs(pair_scr)
    _load_vt(v_ref, vt_scr)
    _init_pipeline(m_all, acc_all, bufs[0])
    lane = lax.broadcasted_iota(jnp.int32, (TQ, LANE), 1)

    def qk_stage(t, bufs):
        q = _tile_rows(q_ref, tab_ref[PAIR_Q, t])
        first_off = jnp.where(tab_ref[PAIR_DIAG, t] > 0, LANE, B_ROPE)
        q_hi = jnp.where(lane >= first_off, jnp.zeros((), BF16), q[:, LANE:])
        q_cat = jnp.concatenate([q[:, :LANE], q_hi], axis=1)
        _qk_store(_nt_dot(_tile_rows(k_ref, tab_ref[PAIR_K, t]), q_cat), bufs)

    _run_pipeline(tab_ref, bufs, qk_stage,
                  lambda t, bufs: _softmax_stage(tab_ref, t, bufs, m_all),
                  lambda t, bufs: _pv_stage(tab_ref, t, bufs, vt_scr, acc_all))
    for qi in range(m_all.shape[0] - 1):
        rows = slice(qi * TQ, (qi + 1) * TQ)
        o_ref[rows, :] = (_normalized(acc_all, qi).T * _silu(g_ref[rows, :])).astype(BF16)


def _mla_attn(tab, q4, k4, v4, of4, batch, seq):
    hs = functools.partial(_head_spec, n_heads=B_HEADS, seq=seq)
    n_tiles = seq // TQ
    return pl.pallas_call(
        _mla_attn_kernel,
        grid=(batch * B_HEADS,),
        in_specs=[_SMEM_SPEC, hs(2 * LANE, 0), hs(2 * LANE, 0), hs(LANE, 0), hs(LANE, FSLAB_BGATE)],
        out_specs=hs(LANE, 0),
        out_shape=jax.ShapeDtypeStruct((B_HEADS, batch, seq, LANE), BF16),
        scratch_shapes=[pltpu.VMEM((n_tiles, ACC_ROWS, TK), BF16)]
        + _state_scratch(n_tiles) + PAIRS_PER_TRIP * _pair_scratch(),
        compiler_params=_cparams(1),
        name="mla_attn",
    )(tab, q4, k4, v4, of4)


def _diff_attn_kernel(tab_ref, q_ref, k_ref, v_ref, g_ref, bias_ref, lq1_ref, lk1_ref, lq2_ref, lk2_ref, sg_ref,
                      o_ref, vt_scr, kext_scr, qext_scr, m1_all, acc1_all, m2_all, acc2_all, *pair_scr, lam_init):
    flat = _pair_bufs(pair_scr)
    bufs = [(flat[2 * i], flat[2 * i + 1]) for i in range(PAIRS_PER_TRIP)]
    states = ((m1_all, acc1_all), (m2_all, acc2_all))
    _load_vt(v_ref, vt_scr)
    for (m_all, acc_all), first in zip(states, bufs[0]):
        _init_pipeline(m_all, acc_all, first)
    row = lax.broadcasted_iota(jnp.int32, (TQ, LANE), 0)
    lane = lax.broadcasted_iota(jnp.int32, (TQ, LANE), 1)
    kext_scr[...] = _chunk_onehot(row >> CHUNK_SHIFT, lane, 0).astype(BF16)
    qext_scr[0] = jnp.zeros((TQ, LANE), BF16)
    qext_scr[1] = _chunk_maskq(row >> CHUNK_SHIFT, lane, 0).astype(BF16)
    lam = (jnp.exp(jnp.sum(lq1_ref[...] * lk1_ref[...], axis=-1, keepdims=True))
           - jnp.exp(jnp.sum(lq2_ref[...] * lk2_ref[...], axis=-1, keepdims=True)) + lam_init)

    def qk_stage(t, bufs2):
        q = _tile_rows(q_ref, tab_ref[PAIR_Q, t])
        qext = qext_scr[tab_ref[PAIR_DIAG, t]]
        zero = jnp.zeros_like(q)
        k_cat = jnp.concatenate([_tile_rows(k_ref, tab_ref[PAIR_K, t]), kext_scr[...]], axis=1)
        bias = bias_ref[tab_ref[PAIR_BIAS, t]]
        for bufs, q_map in zip(bufs2, (jnp.where(lane < C_QK, q, zero), jnp.where(lane >= C_QK, q, zero))):
            _qk_store(_nt_dot(k_cat, jnp.concatenate([q_map, qext], axis=1)) + bias, bufs)

    def sm_stage(t, bufs2):
        for (m_all, _), bufs in zip(states, bufs2):
            _softmax_stage(tab_ref, t, bufs, m_all)

    def pv_stage(t, bufs2):
        for (_, acc_all), bufs in zip(states, bufs2):
            _pv_stage(tab_ref, t, bufs, vt_scr, acc_all)

    _run_pipeline(tab_ref, bufs, qk_stage, sm_stage, pv_stage)
    for qi in range(m1_all.shape[0] - 1):
        rows = slice(qi * TQ, (qi + 1) * TQ)
        o = _normalized(acc1_all, qi) - lam * _normalized(acc2_all, qi)
        o = o * lax.rsqrt(jnp.mean(o * o, axis=0, keepdims=True) + EPS)
        out = o.T * sg_ref[...] * (1.0 - lam_init)
        o_ref[rows, :] = (out * _silu(g_ref[rows, :])).astype(BF16)


def _diff_attn(tab, ob4, of4, bias, lq1, lk1, lq2, lk2, sg, lam_init, batch, seq):
    hs = functools.partial(_head_spec, n_heads=C_HEADS, seq=seq)
    vec = lambda n: pl.BlockSpec((1, n), lambda g: (0, 0))
    n_tiles = seq // TQ
    return pl.pallas_call(
        functools.partial(_diff_attn_kernel, lam_init=lam_init),
        grid=(batch * C_HEADS,),
        in_specs=[_SMEM_SPEC, hs(LANE, SLAB_CQ), hs(LANE, SLAB_CK), hs(LANE, SLAB_CV), hs(LANE, FSLAB_CGATE),
                  pl.BlockSpec((None, 3, TK, TQ), lambda g: (g % C_HEADS, 0, 0, 0)),
                  vec(C_QK), vec(C_QK), vec(C_QK), vec(C_QK), vec(C_V)],
        out_specs=hs(LANE, 0),
        out_shape=jax.ShapeDtypeStruct((C_HEADS, batch, seq, LANE), BF16),
        scratch_shapes=[pltpu.VMEM((n_tiles, ACC_ROWS, TK), BF16), pltpu.VMEM((TK, LANE), BF16),
                        pltpu.VMEM((2, TQ, LANE), BF16)]
        + 2 * _state_scratch(n_tiles) + 2 * PAIRS_PER_TRIP * _pair_scratch(),
        compiler_params=_cparams(1),
        name="diff_attn",
    )(tab, ob4, ob4, ob4, of4, bias, lq1, lk1, lq2, lk2, sg)


def _outproj_kernel(a_ref, b_ref, c_ref, w_ref, x_ref, g_ref, o_ref, *, final_norm):
    mixed = jnp.concatenate([a_ref[h] for h in range(A_HEADS)] + [b_ref[h] for h in range(B_HEADS)]
                            + [c_ref[h] for h in range(C_HEADS)], axis=1)
    y = x_ref[...] + jnp.dot(mixed, w_ref[...], preferred_element_type=F32)
    if final_norm:
        y = _rms(y, g_ref[...])
    o_ref[...] = y


def _outproj(mix_a, mix_b, mix_c, w, x2, g, final_norm):
    m = x2.shape[0]
    tm = TM_PROJ
    return pl.pallas_call(
        functools.partial(_outproj_kernel, final_norm=final_norm),
        grid=(m // tm,),
        in_specs=[
            pl.BlockSpec((A_HEADS, tm, LANE), lambda i: (0, i, 0)),
            pl.BlockSpec((B_HEADS, tm, LANE), lambda i: (0, i, 0)),
            pl.BlockSpec((C_HEADS, tm, LANE), lambda i: (0, i, 0)),
            pl.BlockSpec((D_MIX, D_MODEL), lambda i: (0, 0)),
            pl.BlockSpec((tm, D_MODEL), lambda i: (i, 0)),
            pl.BlockSpec((1, D_MODEL), lambda i: (0, 0)),
        ],
        out_specs=pl.BlockSpec((tm, D_MODEL), lambda i: (i, 0)),
        out_shape=jax.ShapeDtypeStruct((m, D_MODEL), F32),
        compiler_params=_cparams(1),
        name="outproj",
    )(mix_a, mix_b, mix_c, w, x2, g)


def kernel(x, norm_g, w_in, b_forget, mla_q_norm_g, w_uq, mla_kv_norm_g, w_ukv, lambda_q1, lambda_k1,
           lambda_q2, lambda_k2, diff_subln_g, rel_bias, w_out, final_norm_g):
    batch, seq, d = x.shape
    assert d == D_MODEL and seq % TQ == 0 and seq % TM_PROJ == 0 and (batch * seq) % TM_IN == 0
    m = batch * seq
    x2 = x.reshape(m, d)
    cs = _proj_col_scale()
    cos, sin = _rope_tables(seq)
    bias_tiles = _t5_bias_tiles(rel_bias)
    tab = jnp.asarray(_pair_table(seq // TQ))
    for l in range(N_LAYERS):
        ob, of = _inproj(x2, norm_g[l].reshape(1, d), _prep_w_in(w_in[l]), cs)
        ob4 = ob.reshape(N_BF_SLABS, batch, seq, LANE)
        of4 = of.reshape(N_F_SLABS, batch, seq, LANE)
        kb4 = _fox_bias(of, b_forget[l], batch, seq)
        mix_a = _fox_attn(tab, ob4, kb4, of4, batch, seq)
        qb, kbm, vb = _mla_prep(of, mla_q_norm_g[l].reshape(1, -1), mla_kv_norm_g[l].reshape(1, -1),
                                _prep_w_uq(w_uq[l]), _prep_w_ukv(w_ukv[l]), cos, sin, seq)
        r4 = lambda a: a.reshape(a.shape[0], batch, seq, a.shape[-1])
        mix_b = _mla_attn(tab, r4(qb), r4(kbm), r4(vb), of4, batch, seq)
        lam_init = 0.8 - 0.6 * math.exp(-0.3 * l)
        mix_c = _diff_attn(tab, ob4, of4, bias_tiles, lambda_q1[l].reshape(1, -1), lambda_k1[l].reshape(1, -1),
                           lambda_q2[l].reshape(1, -1), lambda_k2[l].reshape(1, -1),
                           diff_subln_g[l].reshape(1, -1), lam_init, batch, seq)
        x2 = _outproj(mix_a.reshape(A_HEADS, m, LANE), mix_b.reshape(B_HEADS, m, LANE),
                      mix_c.reshape(C_HEADS, m, LANE), w_out[l].astype(BF16), x2,
                      final_norm_g.reshape(1, d), final_norm=(l == N_LAYERS - 1))
    return x2.reshape(batch, seq, d)
```

```python
import functools
import math

import numpy as np
import jax
import jax.numpy as jnp
from jax import lax
from jax.experimental import pallas as pl
from jax.experimental.pallas import tpu as pltpu

F32 = jnp.float32
BF16 = jnp.bfloat16

D_MODEL = 2048
N_LAYERS = 2
CHUNK = 64
HEAD_DIM = 128
EPS = 1e-6
A_HEADS = 6
A_DIM = A_HEADS * HEAD_DIM
B_HEADS = 6
B_Q_LORA = 512
B_KV_LORA = 256
B_NOPE = 128
B_ROPE = 64
B_V = 128
B_DIM = B_HEADS * B_V
ROPE_THETA = 10000.0
C_HEADS = 4
C_QK = 64
C_V = 2 * C_QK
C_DIM = C_HEADS * C_V
REL_BUCKETS = 32
REL_MAX_DIST = 128
D_MIX = A_DIM + B_DIM + C_DIM

LANE = 128
V7X_VMEM_BYTES = 64 * 1024 * 1024
VMEM_LIMIT = 56 * 1024 * 1024

LOG2E = math.log2(math.e)
NEG_BIG = -1e30

TN = 768
SLABS_PER_TILE = TN // LANE
N_BF_SLABS = 30
N_F_SLABS = 24
N_BF_TILES = N_BF_SLABS // SLABS_PER_TILE
N_F_TILES = N_F_SLABS // SLABS_PER_TILE
N_PROJ = (N_BF_SLABS + N_F_SLABS) * LANE
SLAB_AQ, SLAB_AK, SLAB_AV, SLAB_CQ, SLAB_CK, SLAB_CV = 0, 6, 12, 18, 22, 26
FSLAB_AGATE, FSLAB_BGATE, FSLAB_CGATE, FSLAB_CQ, FSLAB_CKV, FSLAB_KROPE = 0, 6, 12, 16, 20, 22
AF_LANE = 64

TM_PROJ = 512
TM_IN = 1024
TQ = 512
TK = 512

CHUNK_SHIFT = CHUNK.bit_length() - 1
N_TILE_CHUNKS = TK // CHUNK
FOX_MASK_LANE = 3


def _chunk_onehot(row_chunk, lane, lane0):
    return jnp.where(lane - lane0 == row_chunk, 1.0, 0.0)


def _chunk_maskq(row_chunk, lane, lane0):
    c = lane - lane0
    return jnp.where((c >= 0) & (c < N_TILE_CHUNKS) & (row_chunk < c), NEG_BIG, 0.0)


def _cparams(n_grid):
    return pltpu.CompilerParams(dimension_semantics=("arbitrary",) * n_grid,
                                vmem_limit_bytes=VMEM_LIMIT)


def _prep_w_in(w):
    w = w.astype(BF16)
    o = 0
    seg = {}
    for name, size in (("a_q", A_DIM), ("a_k", A_DIM), ("a_v", A_DIM), ("a_f", A_HEADS), ("a_gate", A_DIM),
                       ("b_cq", B_Q_LORA), ("b_ckv", B_KV_LORA), ("b_krope", B_ROPE), ("b_gate", B_DIM),
                       ("c_q", C_DIM), ("c_k", C_DIM), ("c_v", C_DIM), ("c_gate", C_DIM)):
        seg[name] = w[:, o:o + size]
        o += size
    half = B_ROPE // 2
    kr = seg["b_krope"]
    kr_rot = jnp.concatenate([-kr[:, half:], kr[:, :half]], axis=1)
    z = lambda n: jnp.zeros((w.shape[0], n), w.dtype)
    cols = [seg["a_q"], seg["a_k"], seg["a_v"], seg["c_q"], seg["c_k"], seg["c_v"],
            seg["a_gate"], seg["b_gate"], seg["c_gate"], seg["b_cq"], seg["b_ckv"],
            kr, z(LANE - B_ROPE),
            kr_rot, seg["a_f"], z(LANE - B_ROPE - A_HEADS)]
    out = jnp.concatenate(cols, axis=1)
    assert out.shape[1] == N_PROJ
    return out


def _proj_col_scale():
    s = jnp.ones((N_PROJ,), F32)
    s = s.at[SLAB_AQ * LANE:(SLAB_AQ + A_HEADS) * LANE].set(LOG2E / math.sqrt(HEAD_DIM))
    s = s.at[SLAB_CQ * LANE:(SLAB_CQ + C_HEADS) * LANE].set(LOG2E / math.sqrt(C_QK))
    return s.reshape(1, N_PROJ)


def _prep_w_uq(w):
    w = w.reshape(B_Q_LORA, B_HEADS, B_NOPE + B_ROPE)
    nope = w[:, :, :B_NOPE].reshape(B_Q_LORA, B_HEADS * B_NOPE)
    rope = w[:, :, B_NOPE:]
    half = B_ROPE // 2
    rot = jnp.concatenate([-rope[:, :, half:], rope[:, :, :half]], axis=2)
    pad = jnp.zeros((B_Q_LORA, B_HEADS, LANE - B_ROPE), w.dtype)
    rope_p = jnp.concatenate([rope, pad], axis=2).reshape(B_Q_LORA, B_HEADS * LANE)
    rot_p = jnp.concatenate([rot, pad], axis=2).reshape(B_Q_LORA, B_HEADS * LANE)
    return jnp.concatenate([nope, rope_p, rot_p], axis=1).astype(BF16)


def _prep_w_ukv(w):
    w = w.reshape(B_KV_LORA, B_HEADS, B_NOPE + B_V)
    k = w[:, :, :B_NOPE].reshape(B_KV_LORA, B_HEADS * B_NOPE)
    v = w[:, :, B_NOPE:].reshape(B_KV_LORA, B_HEADS * B_V)
    return jnp.concatenate([k, v], axis=1).astype(BF16)


def _rope_tables(seq):
    half = B_ROPE // 2
    inv = ROPE_THETA ** (-jnp.arange(half, dtype=F32) / half)
    ang = jnp.arange(seq).astype(F32)[:, None] * inv[None, :]
    pad = jnp.zeros((seq, LANE - B_ROPE), F32)
    cos = jnp.concatenate([jnp.cos(ang), jnp.cos(ang), pad], axis=1)
    sin = jnp.concatenate([jnp.sin(ang), jnp.sin(ang), pad], axis=1)
    return cos, sin


def _t5_bucket(rel):
    nb = REL_BUCKETS // 2
    max_exact = nb // 2
    ret = (rel > 0).astype(jnp.int32) * nb
    n = jnp.abs(rel)
    nf = jnp.maximum(n, 1).astype(F32)
    large = max_exact + (jnp.log(nf / max_exact) / math.log(REL_MAX_DIST / max_exact)
                         * (nb - max_exact)).astype(jnp.int32)
    large = jnp.minimum(large, nb - 1)
    return ret + jnp.where(n < max_exact, n, large)


def _t5_bias_vectors(rel_bias):
    assert TQ == TK and TQ > REL_MAX_DIST
    x = jnp.arange(2 * TQ)
    x = jnp.where(x < TQ, x, x - 2 * TQ)
    bucket = jnp.stack([_t5_bucket(-x - delta) for delta in (0, TQ)])
    rel = rel_bias - rel_bias[REL_BUCKETS // 2 - 1][None, :]
    t = jnp.zeros((C_HEADS,) + bucket.shape, F32)
    for b in range(REL_BUCKETS):
        t = jnp.where(bucket[None] == b, rel[b][:, None, None], t)
    return (t * LOG2E).reshape(C_HEADS, 2, 1, 2 * TQ)


def _inproj_kernel(x_ref, g_ref, w_ref, cs_ref, ob_ref, of_ref, h_scr):
    j = pl.program_id(1)

    @pl.when(j == 0)
    def _():
        x = x_ref[...]
        ms = jnp.mean(x * x, axis=-1, keepdims=True)
        h_scr[...] = (x * lax.rsqrt(ms + EPS) * g_ref[...]).astype(BF16)

    def project(out_ref):
        acc = jnp.dot(h_scr[...], w_ref[...], preferred_element_type=F32) * cs_ref[...]
        for s in range(SLABS_PER_TILE):
            out_ref[s] = acc[:, s * LANE:(s + 1) * LANE].astype(out_ref.dtype)

    @pl.when(j < N_BF_TILES)
    def _():
        project(ob_ref)

    @pl.when(j >= N_BF_TILES)
    def _():
        project(of_ref)


def _inproj(x2, g, w, cs):
    m = x2.shape[0]
    tm = TM_IN
    return pl.pallas_call(
        _inproj_kernel,
        grid=(m // tm, N_BF_TILES + N_F_TILES),
        in_specs=[
            pl.BlockSpec((tm, D_MODEL), lambda i, j: (i, 0)),
            pl.BlockSpec((1, D_MODEL), lambda i, j: (0, 0)),
            pl.BlockSpec((D_MODEL, TN), lambda i, j: (0, j)),
            pl.BlockSpec((1, TN), lambda i, j: (0, j)),
        ],
        out_specs=[
            pl.BlockSpec((SLABS_PER_TILE, tm, LANE), lambda i, j: (jnp.minimum(j, N_BF_TILES - 1), i, 0)),
            pl.BlockSpec((SLABS_PER_TILE, tm, LANE), lambda i, j: (jnp.maximum(j - N_BF_TILES, 0), i, 0)),
        ],
        out_shape=[jax.ShapeDtypeStruct((N_BF_SLABS, m, LANE), BF16),
                   jax.ShapeDtypeStruct((N_F_SLABS, m, LANE), F32)],
        scratch_shapes=[pltpu.VMEM((tm, D_MODEL), BF16)],
        compiler_params=_cparams(2),
        name="inproj",
    )(x2, g, w, cs)


def _rms(x, g):
    return x * lax.rsqrt(jnp.mean(x * x, axis=-1, keepdims=True) + EPS) * g


def _mla_prep_kernel(cq_ref, ckv_ref, kr_ref, gq_ref, gkv_ref, wq_ref, wkv_ref, cos_ref, sin_ref,
                     qo_ref, ko_ref, vo_ref, *, n_pos):
    scale = LOG2E / math.sqrt(B_NOPE + B_ROPE)
    cos = cos_ref[...]
    sin = sin_ref[...]
    cq = jnp.concatenate([cq_ref[s] for s in range(B_Q_LORA // LANE)], axis=1)
    q = jnp.dot(_rms(cq, gq_ref[...]).astype(BF16), wq_ref[...], preferred_element_type=F32)
    ckv = jnp.concatenate([ckv_ref[s] for s in range(B_KV_LORA // LANE)], axis=1)
    kv = jnp.dot(_rms(ckv, gkv_ref[...]).astype(BF16), wkv_ref[...], preferred_element_type=F32)
    tm = cos.shape[0]
    pos = (pl.program_id(0) % n_pos) * tm + lax.broadcasted_iota(jnp.int32, (tm, LANE), 0)
    chunk = (pos % TK) >> CHUNK_SHIFT
    lane = lax.broadcasted_iota(jnp.int32, (tm, LANE), 1)
    k_rope = (kr_ref[0] * cos + kr_ref[1] * sin + _chunk_onehot(chunk, lane, B_ROPE)).astype(BF16)
    mask_q = _chunk_maskq(chunk, lane, B_ROPE)
    n_h = B_HEADS * LANE
    for h in range(B_HEADS):
        sl = slice(h * LANE, (h + 1) * LANE)
        q_rope = q[:, n_h + h * LANE:n_h + (h + 1) * LANE] * cos + q[:, 2 * n_h + h * LANE:2 * n_h + (h + 1) * LANE] * sin
        qo_ref[h, :, :LANE] = (q[:, sl] * scale).astype(BF16)
        qo_ref[h, :, LANE:] = (q_rope * scale + mask_q).astype(BF16)
        ko_ref[h, :, :LANE] = kv[:, sl].astype(BF16)
        ko_ref[h, :, LANE:] = k_rope
        vo_ref[h] = kv[:, n_h + h * LANE:n_h + (h + 1) * LANE].astype(BF16)


def _mla_prep(of, gq, gkv, wq, wkv, cos, sin, seq):
    m = of.shape[1]
    tm = TM_PROJ
    n_pos = seq // tm
    nq = B_Q_LORA // LANE
    nkv = B_KV_LORA // LANE
    return pl.pallas_call(
        functools.partial(_mla_prep_kernel, n_pos=n_pos),
        grid=(m // tm,),
        in_specs=[
            pl.BlockSpec((nq, tm, LANE), lambda i: (FSLAB_CQ // nq, i, 0)),
            pl.BlockSpec((nkv, tm, LANE), lambda i: (FSLAB_CKV // nkv, i, 0)),
            pl.BlockSpec((2, tm, LANE), lambda i: (FSLAB_KROPE // 2, i, 0)),
            pl.BlockSpec((1, B_Q_LORA), lambda i: (0, 0)),
            pl.BlockSpec((1, B_KV_LORA), lambda i: (0, 0)),
            pl.BlockSpec(wq.shape, lambda i: (0, 0)),
            pl.BlockSpec(wkv.shape, lambda i: (0, 0)),
            pl.BlockSpec((tm, LANE), lambda i: (i % n_pos, 0)),
            pl.BlockSpec((tm, LANE), lambda i: (i % n_pos, 0)),
        ],
        out_specs=[
            pl.BlockSpec((B_HEADS, tm, 2 * LANE), lambda i: (0, i, 0)),
            pl.BlockSpec((B_HEADS, tm, 2 * LANE), lambda i: (0, i, 0)),
            pl.BlockSpec((B_HEADS, tm, LANE), lambda i: (0, i, 0)),
        ],
        out_shape=[jax.ShapeDtypeStruct((B_HEADS, m, 2 * LANE), BF16),
                   jax.ShapeDtypeStruct((B_HEADS, m, 2 * LANE), BF16),
                   jax.ShapeDtypeStruct((B_HEADS, m, LANE), BF16)],
        compiler_params=_cparams(1),
        name="mla_prep",
    )(of, of, of, gq, gkv, wq, wkv, cos, sin)


CUM_CHUNK = 128


def _fox_bias_kernel(af_ref, bf_ref, sel_ref, kb_ref, cum_scr):
    seq = af_ref.shape[1]
    r = lax.broadcasted_iota(jnp.int32, (CUM_CHUNK, CUM_CHUNK), 0)
    c = lax.broadcasted_iota(jnp.int32, (CUM_CHUNK, CUM_CHUNK), 1)
    tri = (r >= c).astype(F32)
    bvec = bf_ref[...]

    def body(t, carry):
        r0 = pl.multiple_of(t * CUM_CHUNK, CUM_CHUNK)
        x = af_ref[0, pl.ds(r0, CUM_CHUNK), :] + bvec
        logf = jnp.minimum(x, 0.0) - jnp.log1p(jnp.exp(-jnp.abs(x)))
        cs = jnp.dot(tri, logf, preferred_element_type=F32, precision=lax.Precision.HIGHEST) + carry
        cum_scr[pl.ds(r0, CUM_CHUNK), :] = cs
        return cs[CUM_CHUNK - 1:CUM_CHUNK, :]

    lax.fori_loop(0, seq // CUM_CHUNK, body, jnp.zeros((1, LANE), F32))
    bias = cum_scr[...] * (-LOG2E)
    hi = bias.astype(BF16)
    r1 = bias - hi.astype(F32)
    mid = r1.astype(BF16)
    lo = (r1 - mid.astype(F32)).astype(BF16)
    parts = jnp.concatenate([hi, mid, lo], axis=1)
    kb = jnp.dot(parts, sel_ref[...], preferred_element_type=F32)
    chunk = (lax.broadcasted_iota(jnp.int32, (seq, LANE), 0) % TK) >> CHUNK_SHIFT
    onehot = _chunk_onehot(chunk, lax.broadcasted_iota(jnp.int32, (seq, LANE), 1), FOX_MASK_LANE)
    for h in range(A_HEADS):
        kb_ref[h, 0] = (kb[:, h * LANE:(h + 1) * LANE] + onehot).astype(BF16)


def _fox_bias(of, b_forget, batch, seq):
    bvec = jnp.zeros((1, LANE), F32).at[0, AF_LANE:AF_LANE + A_HEADS].set(b_forget)
    rows = jnp.arange(3 * LANE)
    part, lane = rows // LANE, rows % LANE
    head = lane - AF_LANE
    col = jnp.where((head >= 0) & (head < A_HEADS), head * LANE + part, -1)
    sel = (col[:, None] == jnp.arange(A_HEADS * LANE)[None, :]).astype(BF16)
    of4 = of.reshape(N_F_SLABS, batch, seq, LANE)
    return pl.pallas_call(
        _fox_bias_kernel,
        grid=(batch,),
        in_specs=[
            pl.BlockSpec((None, 1, seq, LANE), lambda b: (N_F_SLABS - 1, b, 0, 0)),
            pl.BlockSpec((1, LANE), lambda b: (0, 0)),
            pl.BlockSpec(sel.shape, lambda b: (0, 0)),
        ],
        out_specs=pl.BlockSpec((A_HEADS, 1, seq, LANE), lambda b: (0, b, 0, 0)),
        out_shape=jax.ShapeDtypeStruct((A_HEADS, batch, seq, LANE), BF16),
        scratch_shapes=[pltpu.VMEM((seq, LANE), F32)],
        compiler_params=_cparams(1),
        name="fox_bias",
    )(of4, bvec, sel)


PAIR_Q, PAIR_K, PAIR_DIAG, PAIR_BIAS, PAIR_STATE = 0, 1, 2, 3, 4
ACC_ROWS = HEAD_DIM + 16
PAIRS_PER_TRIP = 4
PIPE_DIST = 2


def _pair_table(n_tiles):
    pad = (0, 0, 1, 0, n_tiles)
    rows = []
    for qi in range(n_tiles):
        for kj in range(qi + 1):
            rows.append((qi, kj, int(kj == qi), 0 if kj == qi else (1 if kj == qi - 1 else 2), qi))
    rows += [pad] * (-len(rows) % PAIRS_PER_TRIP)
    rows = [(0, 0, 0, 2, n_tiles)] * PIPE_DIST + rows + [pad] * PIPE_DIST
    return np.asarray(rows, np.int32).T.copy()


def _nt_dot(a, b):
    return lax.dot_general(a, b, (((1,), (1,)), ((), ())), preferred_element_type=F32)


def _tile_rows(ref, tile):
    return ref[pl.ds(pl.multiple_of(tile * TK, TK), TK), :]


def _load_vt(v_ref, vt_scr):
    extra = jnp.where(lax.broadcasted_iota(jnp.int32, (ACC_ROWS - HEAD_DIM, TK), 0) == 0, 1.0, 0.0).astype(BF16)
    for j in range(vt_scr.shape[0]):
        vt_scr[j, :HEAD_DIM, :] = v_ref[j * TK:(j + 1) * TK, :].T
        vt_scr[j, HEAD_DIM:, :] = extra


class _PairBufs:
    def __init__(self, s, mb, p, al):
        self.s, self.mb, self.p, self.al = s, mb, p, al


def _pair_scratch():
    return [pltpu.VMEM((TK, TQ), F32), pltpu.VMEM((1, TQ), F32), pltpu.VMEM((TK, TQ), BF16),
            pltpu.VMEM((1, TQ), F32)]


def _state_scratch(n_tiles):
    return [pltpu.VMEM((n_tiles + 1, 1, TQ), F32), pltpu.VMEM((n_tiles + 1, ACC_ROWS, TQ), F32)]


def _init_pipeline(m_all, acc_all, behind):
    m_all[...] = jnp.full(m_all.shape, NEG_BIG, F32)
    acc_all[...] = jnp.zeros(acc_all.shape, F32)
    for bufs in behind:
        bufs.p[...] = jnp.zeros(bufs.p.shape, BF16)
        bufs.al[...] = jnp.ones(bufs.al.shape, F32)


def _pair_bufs(pair_scr):
    return [_PairBufs(*pair_scr[4 * i:4 * i + 4]) for i in range(len(pair_scr) // 4)]


def _qk_store(s, bufs):
    bufs.s[...] = s
    bufs.mb[...] = jnp.max(s, axis=0, keepdims=True)


def _softmax_stage(tab_ref, t, bufs, m_all):
    st = tab_ref[PAIR_STATE, t]
    m_prev = m_all[st]
    m_new = jnp.maximum(m_prev, bufs.mb[...])
    bufs.al[...] = jnp.exp2(m_prev - m_new)
    bufs.p[...] = jnp.exp2(bufs.s[...] - m_new).astype(BF16)
    m_all[st] = m_new


def _pv_stage(tab_ref, t, bufs, vt_scr, acc_all):
    st = tab_ref[PAIR_STATE, t]
    acc_all[st] = bufs.al[...] * acc_all[st] + jnp.dot(vt_scr[tab_ref[PAIR_K, t]], bufs.p[...],
                                                        preferred_element_type=F32)


def _run_pipeline(tab_ref, bufs, qk_stage, sm_stage, pv_stage):
    n_pairs = tab_ref.shape[1] - 2 * PIPE_DIST
    u, d = PAIRS_PER_TRIP, PIPE_DIST
    assert u > d

    def body(i, carry):
        for j in range(u):
            c = u * i + d + j
            pv_stage(c - d, bufs[j % u])
            sm_stage(c, bufs[(d + j) % u])
            qk_stage(c + d, bufs[(2 * d + j) % u])
        return carry

    for c in range(d, 2 * d):
        qk_stage(c, bufs[c % u])
    lax.fori_loop(0, n_pairs // u, body, 0)
    for c in range(n_pairs, n_pairs + d):
        pv_stage(c, bufs[c % u])


def _normalized(acc_all, qi):
    return acc_all[qi, :HEAD_DIM, :] * (1.0 / acc_all[qi, HEAD_DIM:HEAD_DIM + 1, :])


def _silu(g):
    return g / (1.0 + jnp.exp(-g))


def _head_spec(width, slab0, n_heads, seq):
    return pl.BlockSpec((None, None, seq, width), lambda g: (slab0 + g % n_heads, g // n_heads, 0, 0))


_SMEM_SPEC = pl.BlockSpec(memory_space=pltpu.SMEM)


def _fox_attn_kernel(tab_ref, q_ref, k_ref, kb_ref, v_ref, g_ref, o_ref, vt_scr, qext_scr, m_all, acc_all, *pair_scr):
    bufs = _pair_bufs(pair_scr)
    _load_vt(v_ref, vt_scr)
    _init_pipeline(m_all, acc_all, bufs[:PIPE_DIST])
    row = lax.broadcasted_iota(jnp.int32, (TQ, LANE), 0)
    lane = lax.broadcasted_iota(jnp.int32, (TQ, LANE), 1)
    ones3 = jnp.where(lane < FOX_MASK_LANE, 1.0, 0.0)
    qext_scr[0] = ones3.astype(BF16)
    qext_scr[1] = (ones3 + _chunk_maskq(row >> CHUNK_SHIFT, lane, FOX_MASK_LANE)).astype(BF16)
    sub_key = lax.broadcasted_iota(jnp.int32, (LANE, LANE), 0)
    sub_qry = lax.broadcasted_iota(jnp.int32, (LANE, LANE), 1)

    def qk_stage(t, bufs):
        q_cat = jnp.concatenate([_tile_rows(q_ref, tab_ref[PAIR_Q, t]), qext_scr[tab_ref[PAIR_DIAG, t]]], axis=1)
        kj = tab_ref[PAIR_K, t]
        k_cat = jnp.concatenate([_tile_rows(k_ref, kj), _tile_rows(kb_ref, kj)], axis=1)
        s = _nt_dot(k_cat, q_cat)
        masked = sub_key - sub_qry > jnp.where(tab_ref[PAIR_DIAG, t] > 0, 0, TK)
        blocks = []
        for r in range(TK // LANE):
            blk = s[r * LANE:(r + 1) * LANE, :]
            mid = jnp.where(masked, NEG_BIG, blk[:, r * LANE:(r + 1) * LANE])
            parts = ([blk[:, :r * LANE]] if r > 0 else []) + [mid]
            parts += [blk[:, (r + 1) * LANE:]] if (r + 1) * LANE < TQ else []
            blocks.append(jnp.concatenate(parts, axis=1))
        _qk_store(jnp.concatenate(blocks, axis=0), bufs)

    _run_pipeline(tab_ref, bufs, qk_stage,
                  lambda t, bufs: _softmax_stage(tab_ref, t, bufs, m_all),
                  lambda t, bufs: _pv_stage(tab_ref, t, bufs, vt_scr, acc_all))
    for qi in range(m_all.shape[0] - 1):
        rows = slice(qi * TQ, (qi + 1) * TQ)
        o_ref[rows, :] = (_normalized(acc_all, qi).T * _silu(g_ref[rows, :])).astype(BF16)


def _fox_attn(tab, ob4, kb4, of4, batch, seq):
    hs = functools.partial(_head_spec, n_heads=A_HEADS, seq=seq)
    n_tiles = seq // TQ
    return pl.pallas_call(
        _fox_attn_kernel,
        grid=(batch * A_HEADS,),
        in_specs=[_SMEM_SPEC, hs(LANE, SLAB_AQ), hs(LANE, SLAB_AK), hs(LANE, 0), hs(LANE, SLAB_AV),
                  hs(LANE, FSLAB_AGATE)],
        out_specs=hs(LANE, 0),
        out_shape=jax.ShapeDtypeStruct((A_HEADS, batch, seq, LANE), BF16),
        scratch_shapes=[pltpu.VMEM((n_tiles, ACC_ROWS, TK), BF16), pltpu.VMEM((2, TQ, LANE), BF16)]
        + _state_scratch(n_tiles) + PAIRS_PER_TRIP * _pair_scratch(),
        compiler_params=_cparams(1),
        name="fox_attn",
    )(tab, ob4, ob4, kb4, ob4, of4)


def _mla_attn_kernel(tab_ref, q_ref, k_ref, v_ref, g_ref, o_ref, vt_scr, m_all, acc_all, *pair_scr):
    bufs = _pair_bufs(pair_scr)
    _load_vt(v_ref, vt_scr)
    _init_pipeline(m_all, acc_all, bufs[:PIPE_DIST])
    lane = lax.broadcasted_iota(jnp.int32, (TQ, LANE), 1)

    def qk_stage(t, bufs):
        q = _tile_rows(q_ref, tab_ref[PAIR_Q, t])
        first_off = jnp.where(tab_ref[PAIR_DIAG, t] > 0, LANE, B_ROPE)
        q_hi = jnp.where(lane >= first_off, jnp.zeros((), BF16), q[:, LANE:])
        q_cat = jnp.concatenate([q[:, :LANE], q_hi], axis=1)
        _qk_store(_nt_dot(_tile_rows(k_ref, tab_ref[PAIR_K, t]), q_cat), bufs)

    _run_pipeline(tab_ref, bufs, qk_stage,
                  lambda t, bufs: _softmax_stage(tab_ref, t, bufs, m_all),
                  lambda t, bufs: _pv_stage(tab_ref, t, bufs, vt_scr, acc_all))
    for qi in range(m_all.shape[0] - 1):
        rows = slice(qi * TQ, (qi + 1) * TQ)
        o_ref[rows, :] = (_normalized(acc_all, qi).T * _silu(g_ref[rows, :])).astype(BF16)


def _mla_attn(tab, q4, k4, v4, of4, batch, seq):
    hs = functools.partial(_head_spec, n_heads=B_HEADS, seq=seq)
    n_tiles = seq // TQ
    return pl.pallas_call(
        _mla_attn_kernel,
        grid=(batch * B_HEADS,),
        in_specs=[_SMEM_SPEC, hs(2 * LANE, 0), hs(2 * LANE, 0), hs(LANE, 0), hs(LANE, FSLAB_BGATE)],
        out_specs=hs(LANE, 0),
        out_shape=jax.ShapeDtypeStruct((B_HEADS, batch, seq, LANE), BF16),
        scratch_shapes=[pltpu.VMEM((n_tiles, ACC_ROWS, TK), BF16)]
        + _state_scratch(n_tiles) + PAIRS_PER_TRIP * _pair_scratch(),
        compiler_params=_cparams(1),
        name="mla_attn",
    )(tab, q4, k4, v4, of4)


def _diff_attn_kernel(tab_ref, q_ref, k_ref, v_ref, g_ref, bvec_ref, lq1_ref, lk1_ref, lq2_ref, lk2_ref, sg_ref,
                      o_ref, vt_scr, kext_scr, qext_scr, bias_scr, m1_all, acc1_all, m2_all, acc2_all, *pair_scr,
                      lam_init):
    flat = _pair_bufs(pair_scr)
    bufs = [(flat[2 * i], flat[2 * i + 1]) for i in range(PAIRS_PER_TRIP)]
    states = ((m1_all, acc1_all), (m2_all, acc2_all))
    _load_vt(v_ref, vt_scr)
    for k, (m_all, acc_all) in enumerate(states):
        _init_pipeline(m_all, acc_all, [b[k] for b in bufs[:PIPE_DIST]])
    for i in range(2):
        gen = jnp.broadcast_to(bvec_ref[i], (TK, 2 * TQ))
        bias_scr[i] = pltpu.roll(gen, 0, 1, stride=1, stride_axis=0)[:, :TQ]
    bias_scr[2] = jnp.zeros((TK, TQ), F32)
    row = lax.broadcasted_iota(jnp.int32, (TQ, LANE), 0)
    lane = lax.broadcasted_iota(jnp.int32, (TQ, LANE), 1)
    kext_scr[...] = _chunk_onehot(row >> CHUNK_SHIFT, lane, 0).astype(BF16)
    qext_scr[0] = jnp.zeros((TQ, LANE), BF16)
    qext_scr[1] = _chunk_maskq(row >> CHUNK_SHIFT, lane, 0).astype(BF16)
    lam = (jnp.exp(jnp.sum(lq1_ref[...] * lk1_ref[...], axis=-1, keepdims=True))
           - jnp.exp(jnp.sum(lq2_ref[...] * lk2_ref[...], axis=-1, keepdims=True)) + lam_init)

    def qk_stage(t, bufs2):
        q = _tile_rows(q_ref, tab_ref[PAIR_Q, t])
        qext = qext_scr[tab_ref[PAIR_DIAG, t]]
        zero = jnp.zeros_like(q)
        k_cat = jnp.concatenate([_tile_rows(k_ref, tab_ref[PAIR_K, t]), kext_scr[...]], axis=1)
        bias = bias_scr[tab_ref[PAIR_BIAS, t]]
        for bufs, q_map in zip(bufs2, (jnp.where(lane < C_QK, q, zero), jnp.where(lane >= C_QK, q, zero))):
            _qk_store(_nt_dot(k_cat, jnp.concatenate([q_map, qext], axis=1)) + bias, bufs)

    def sm_stage(t, bufs2):
        for (m_all, _), bufs in zip(states, bufs2):
            _softmax_stage(tab_ref, t, bufs, m_all)

    def pv_stage(t, bufs2):
        for (_, acc_all), bufs in zip(states, bufs2):
            _pv_stage(tab_ref, t, bufs, vt_scr, acc_all)

    _run_pipeline(tab_ref, bufs, qk_stage, sm_stage, pv_stage)
    for qi in range(m1_all.shape[0] - 1):
        rows = slice(qi * TQ, (qi + 1) * TQ)
        o = _normalized(acc1_all, qi) - lam * _normalized(acc2_all, qi)
        o = o * lax.rsqrt(jnp.mean(o * o, axis=0, keepdims=True) + EPS)
        out = o.T * sg_ref[...] * (1.0 - lam_init)
        o_ref[rows, :] = (out * _silu(g_ref[rows, :])).astype(BF16)


def _diff_attn(tab, ob4, of4, bias, lq1, lk1, lq2, lk2, sg, lam_init, batch, seq):
    hs = functools.partial(_head_spec, n_heads=C_HEADS, seq=seq)
    vec = lambda n: pl.BlockSpec((1, n), lambda g: (0, 0))
    n_tiles = seq // TQ
    return pl.pallas_call(
        functools.partial(_diff_attn_kernel, lam_init=lam_init),
        grid=(batch * C_HEADS,),
        in_specs=[_SMEM_SPEC, hs(LANE, SLAB_CQ), hs(LANE, SLAB_CK), hs(LANE, SLAB_CV), hs(LANE, FSLAB_CGATE),
                  pl.BlockSpec((None, 2, 1, 2 * TQ), lambda g: (g % C_HEADS, 0, 0, 0)),
                  vec(C_QK), vec(C_QK), vec(C_QK), vec(C_QK), vec(C_V)],
        out_specs=hs(LANE, 0),
        out_shape=jax.ShapeDtypeStruct((C_HEADS, batch, seq, LANE), BF16),
        scratch_shapes=[pltpu.VMEM((n_tiles, ACC_ROWS, TK), BF16), pltpu.VMEM((TK, LANE), BF16),
                        pltpu.VMEM((2, TQ, LANE), BF16), pltpu.VMEM((3, TK, TQ), F32)]
        + 2 * _state_scratch(n_tiles) + 2 * PAIRS_PER_TRIP * _pair_scratch(),
        compiler_params=_cparams(1),
        name="diff_attn",
    )(tab, ob4, ob4, ob4, of4, bias, lq1, lk1, lq2, lk2, sg)


def _outproj_kernel(a_ref, b_ref, c_ref, w_ref, x_ref, g_ref, o_ref, *, final_norm):
    mixed = jnp.concatenate([a_ref[h] for h in range(A_HEADS)] + [b_ref[h] for h in range(B_HEADS)]
                            + [c_ref[h] for h in range(C_HEADS)], axis=1)
    y = x_ref[...] + jnp.dot(mixed, w_ref[...], preferred_element_type=F32)
    if final_norm:
        y = _rms(y, g_ref[...])
    o_ref[...] = y


def _outproj(mix_a, mix_b, mix_c, w, x2, g, final_norm):
    m = x2.shape[0]
    tm = TM_PROJ
    return pl.pallas_call(
        functools.partial(_outproj_kernel, final_norm=final_norm),
        grid=(m // tm,),
        in_specs=[
            pl.BlockSpec((A_HEADS, tm, LANE), lambda i: (0, i, 0)),
            pl.BlockSpec((B_HEADS, tm, LANE), lambda i: (0, i, 0)),
            pl.BlockSpec((C_HEADS, tm, LANE), lambda i: (0, i, 0)),
            pl.BlockSpec((D_MIX, D_MODEL), lambda i: (0, 0)),
            pl.BlockSpec((tm, D_MODEL), lambda i: (i, 0)),
            pl.BlockSpec((1, D_MODEL), lambda i: (0, 0)),
        ],
        out_specs=pl.BlockSpec((tm, D_MODEL), lambda i: (i, 0)),
        out_shape=jax.ShapeDtypeStruct((m, D_MODEL), F32),
        compiler_params=_cparams(1),
        name="outproj",
    )(mix_a, mix_b, mix_c, w, x2, g)


def kernel(x, norm_g, w_in, b_forget, mla_q_norm_g, w_uq, mla_kv_norm_g, w_ukv, lambda_q1, lambda_k1,
           lambda_q2, lambda_k2, diff_subln_g, rel_bias, w_out, final_norm_g):
    batch, seq, d = x.shape
    assert d == D_MODEL and seq % TQ == 0 and seq % TM_PROJ == 0 and (batch * seq) % TM_IN == 0
    m = batch * seq
    x2 = x.reshape(m, d)
    cs = _proj_col_scale()
    cos, sin = _rope_tables(seq)
    bias_vecs = _t5_bias_vectors(rel_bias)
    tab = jnp.asarray(_pair_table(seq // TQ))
    for l in range(N_LAYERS):
        ob, of = _inproj(x2, norm_g[l].reshape(1, d), _prep_w_in(w_in[l]), cs)
        ob4 = ob.reshape(N_BF_SLABS, batch, seq, LANE)
        of4 = of.reshape(N_F_SLABS, batch, seq, LANE)
        kb4 = _fox_bias(of, b_forget[l], batch, seq)
        mix_a = _fox_attn(tab, ob4, kb4, of4, batch, seq)
        qb, kbm, vb = _mla_prep(of, mla_q_norm_g[l].reshape(1, -1), mla_kv_norm_g[l].reshape(1, -1),
                                _prep_w_uq(w_uq[l]), _prep_w_ukv(w_ukv[l]), cos, sin, seq)
        r4 = lambda a: a.reshape(a.shape[0], batch, seq, a.shape[-1])
        mix_b = _mla_attn(tab, r4(qb), r4(kbm), r4(vb), of4, batch, seq)
        lam_init = 0.8 - 0.6 * math.exp(-0.3 * l)
        mix_c = _diff_attn(tab, ob4, of4, bias_vecs, lambda_q1[l].reshape(1, -1), lambda_k1[l].reshape(1, -1),
                           lambda_q2[l].reshape(1, -1), lambda_k2[l].reshape(1, -1),
                           diff_subln_g[l].reshape(1, -1), lam_init, batch, seq)
        x2 = _outproj(mix_a.reshape(A_HEADS, m, LANE), mix_b.reshape(B_HEADS, m, LANE),
                      mix_c.reshape(C_HEADS, m, LANE), w_out[l].astype(BF16), x2,
                      final_norm_g.reshape(1, d), final_norm=(l == N_LAYERS - 1))
    return x2.reshape(batch, seq, d)
```

```python
import functools
import math

import numpy as np
import jax
import jax.numpy as jnp
from jax import lax
from jax.experimental import pallas as pl
from jax.experimental.pallas import tpu as pltpu

F32 = jnp.float32
BF16 = jnp.bfloat16

D_MODEL = 2048
N_LAYERS = 2
CHUNK = 64
HEAD_DIM = 128
EPS = 1e-6
A_HEADS = 6
A_DIM = A_HEADS * HEAD_DIM
B_HEADS = 6
B_Q_LORA = 512
B_KV_LORA = 256
B_NOPE = 128
B_ROPE = 64
B_V = 128
B_DIM = B_HEADS * B_V
ROPE_THETA = 10000.0
C_HEADS = 4
C_QK = 64
C_V = 2 * C_QK
C_DIM = C_HEADS * C_V
REL_BUCKETS = 32
REL_MAX_DIST = 128
D_MIX = A_DIM + B_DIM + C_DIM

LANE = 128
V7X_VMEM_BYTES = 64 * 1024 * 1024
VMEM_LIMIT = 56 * 1024 * 1024

LOG2E = math.log2(math.e)
NEG_BIG = -1e30

TN = 768
SLABS_PER_TILE = TN // LANE
N_BF_SLABS = 30
N_F_SLABS = 24
N_BF_TILES = N_BF_SLABS // SLABS_PER_TILE
N_F_TILES = N_F_SLABS // SLABS_PER_TILE
N_PROJ = (N_BF_SLABS + N_F_SLABS) * LANE
SLAB_AQ, SLAB_AK, SLAB_AV, SLAB_CQ, SLAB_CK, SLAB_CV = 0, 6, 12, 18, 22, 26
FSLAB_AGATE, FSLAB_BGATE, FSLAB_CGATE, FSLAB_CQ, FSLAB_CKV, FSLAB_KROPE = 0, 6, 12, 16, 20, 22
AF_LANE = 64

TM_PROJ = 512
TM_IN = 1024
TQ = 512
TK = 512

CHUNK_SHIFT = CHUNK.bit_length() - 1
N_TILE_CHUNKS = TK // CHUNK
FOX_TERM_STRIDE = A_HEADS
FOX_TERMS = 3


def _chunk_onehot(row_chunk, lane, lane0):
    return jnp.where(lane - lane0 == row_chunk, 1.0, 0.0)


def _chunk_maskq(row_chunk, lane, lane0):
    c = lane - lane0
    return jnp.where((c >= 0) & (c < N_TILE_CHUNKS) & (row_chunk < c), NEG_BIG, 0.0)


def _cparams(n_grid):
    return pltpu.CompilerParams(dimension_semantics=("arbitrary",) * n_grid,
                                vmem_limit_bytes=VMEM_LIMIT)


def _w_in_slab_sources():
    o = 0
    seg = {}
    for name, size in (("a_q", A_DIM), ("a_k", A_DIM), ("a_v", A_DIM), ("a_f", A_HEADS), ("a_gate", A_DIM),
                       ("b_cq", B_Q_LORA), ("b_ckv", B_KV_LORA), ("b_krope", B_ROPE), ("b_gate", B_DIM),
                       ("c_q", C_DIM), ("c_k", C_DIM), ("c_v", C_DIM), ("c_gate", C_DIM)):
        seg[name] = (o, size)
        o += size
    slabs = []
    for name in ("a_q", "a_k", "a_v", "c_q", "c_k", "c_v", "a_gate", "b_gate", "c_gate", "b_cq", "b_ckv"):
        start, size = seg[name]
        slabs += [[(start + i, start + i + LANE, 1)] for i in range(0, size, LANE)]
    kr, half = seg["b_krope"][0], B_ROPE // 2
    af = seg["a_f"][0]
    slabs.append([(kr, kr + B_ROPE, 1), (None, LANE - B_ROPE, 0)])
    slabs.append([(kr + half, kr + B_ROPE, -1), (kr, kr + half, 1), (af, af + A_HEADS, 1),
                  (None, LANE - B_ROPE - A_HEADS, 0)])
    assert len(slabs) * LANE == N_PROJ
    return slabs


def _w_in_kernel(w_ref, o_ref):
    for j, pieces in enumerate(_w_in_slab_sources()):
        cols = []
        for start, stop, sign in pieces:
            if start is None:
                cols.append(jnp.zeros((w_ref.shape[0], stop), F32))
            else:
                cols.append(w_ref[:, start:stop] if sign > 0 else -w_ref[:, start:stop])
        slab = cols[0] if len(cols) == 1 else jnp.concatenate(cols, axis=1)
        o_ref[:, j * LANE:(j + 1) * LANE] = slab.astype(BF16)


def _prep_w_in(w):
    d, n = w.shape
    tk = 256
    return pl.pallas_call(
        _w_in_kernel,
        grid=(d // tk,),
        in_specs=[pl.BlockSpec((tk, n), lambda i: (i, 0))],
        out_specs=pl.BlockSpec((tk, N_PROJ), lambda i: (i, 0)),
        out_shape=jax.ShapeDtypeStruct((d, N_PROJ), BF16),
        compiler_params=_cparams(1),
        name="w_in_reorder",
    )(w)


def _proj_col_scale():
    s = jnp.ones((N_PROJ,), F32)
    s = s.at[SLAB_AQ * LANE:(SLAB_AQ + A_HEADS) * LANE].set(LOG2E / math.sqrt(HEAD_DIM))
    s = s.at[SLAB_CQ * LANE:(SLAB_CQ + C_HEADS) * LANE].set(LOG2E / math.sqrt(C_QK))
    return s.reshape(1, N_PROJ)


def _prep_w_uq(w):
    w = w.reshape(B_Q_LORA, B_HEADS, B_NOPE + B_ROPE)
    nope = w[:, :, :B_NOPE].reshape(B_Q_LORA, B_HEADS * B_NOPE)
    rope = w[:, :, B_NOPE:]
    half = B_ROPE // 2
    rot = jnp.concatenate([-rope[:, :, half:], rope[:, :, :half]], axis=2)
    pad = jnp.zeros((B_Q_LORA, B_HEADS, LANE - B_ROPE), w.dtype)
    rope_p = jnp.concatenate([rope, pad], axis=2).reshape(B_Q_LORA, B_HEADS * LANE)
    rot_p = jnp.concatenate([rot, pad], axis=2).reshape(B_Q_LORA, B_HEADS * LANE)
    return jnp.concatenate([nope, rope_p, rot_p], axis=1).astype(BF16)


def _prep_w_ukv(w):
    w = w.reshape(B_KV_LORA, B_HEADS, B_NOPE + B_V)
    k = w[:, :, :B_NOPE].reshape(B_KV_LORA, B_HEADS * B_NOPE)
    v = w[:, :, B_NOPE:].reshape(B_KV_LORA, B_HEADS * B_V)
    return jnp.concatenate([k, v], axis=1).astype(BF16)


def _rope_tables(seq):
    half = B_ROPE // 2
    inv = ROPE_THETA ** (-jnp.arange(half, dtype=F32) / half)
    ang = jnp.arange(seq).astype(F32)[:, None] * inv[None, :]
    pad = jnp.zeros((seq, LANE - B_ROPE), F32)
    cos = jnp.concatenate([jnp.cos(ang), jnp.cos(ang), pad], axis=1)
    sin = jnp.concatenate([jnp.sin(ang), jnp.sin(ang), pad], axis=1)
    return cos, sin


def _t5_bucket(rel):
    nb = REL_BUCKETS // 2
    max_exact = nb // 2
    ret = (rel > 0).astype(jnp.int32) * nb
    n = jnp.abs(rel)
    nf = jnp.maximum(n, 1).astype(F32)
    large = max_exact + (jnp.log(nf / max_exact) / math.log(REL_MAX_DIST / max_exact)
                         * (nb - max_exact)).astype(jnp.int32)
    large = jnp.minimum(large, nb - 1)
    return ret + jnp.where(n < max_exact, n, large)


def _t5_bias_vectors(rel_bias):
    assert TQ == TK and TQ > REL_MAX_DIST
    x = jnp.arange(2 * TQ)
    x = jnp.where(x < TQ, x, x - 2 * TQ)
    bucket = jnp.stack([_t5_bucket(-x - delta) for delta in (0, TQ)])
    rel = rel_bias - rel_bias[REL_BUCKETS // 2 - 1][None, :]
    t = jnp.zeros((C_HEADS,) + bucket.shape, F32)
    for b in range(REL_BUCKETS):
        t = jnp.where(bucket[None] == b, rel[b][:, None, None], t)
    return (t * LOG2E).reshape(C_HEADS, 2, 1, 2 * TQ)


def _inproj_kernel(x_ref, g_ref, w_ref, cs_ref, ob_ref, of_ref, h_scr):
    j = pl.program_id(1)

    @pl.when(j == 0)
    def _():
        x = x_ref[...]
        ms = jnp.mean(x * x, axis=-1, keepdims=True)
        h_scr[...] = (x * lax.rsqrt(ms + EPS) * g_ref[...]).astype(BF16)

    def project(out_ref):
        acc = jnp.dot(h_scr[...], w_ref[...], preferred_element_type=F32) * cs_ref[...]
        for s in range(SLABS_PER_TILE):
            out_ref[s] = acc[:, s * LANE:(s + 1) * LANE].astype(out_ref.dtype)

    @pl.when(j < N_BF_TILES)
    def _():
        project(ob_ref)

    @pl.when(j >= N_BF_TILES)
    def _():
        project(of_ref)


def _inproj(x2, g, w, cs):
    m = x2.shape[0]
    tm = TM_IN
    return pl.pallas_call(
        _inproj_kernel,
        grid=(m // tm, N_BF_TILES + N_F_TILES),
        in_specs=[
            pl.BlockSpec((tm, D_MODEL), lambda i, j: (i, 0)),
            pl.BlockSpec((1, D_MODEL), lambda i, j: (0, 0)),
            pl.BlockSpec((D_MODEL, TN), lambda i, j: (0, j)),
            pl.BlockSpec((1, TN), lambda i, j: (0, j)),
        ],
        out_specs=[
            pl.BlockSpec((SLABS_PER_TILE, tm, LANE), lambda i, j: (jnp.minimum(j, N_BF_TILES - 1), i, 0)),
            pl.BlockSpec((SLABS_PER_TILE, tm, LANE), lambda i, j: (jnp.maximum(j - N_BF_TILES, 0), i, 0)),
        ],
        out_shape=[jax.ShapeDtypeStruct((N_BF_SLABS, m, LANE), BF16),
                   jax.ShapeDtypeStruct((N_F_SLABS, m, LANE), F32)],
        scratch_shapes=[pltpu.VMEM((tm, D_MODEL), BF16)],
        compiler_params=_cparams(2),
        name="inproj",
    )(x2, g, w, cs)


def _rms(x, g):
    return x * lax.rsqrt(jnp.mean(x * x, axis=-1, keepdims=True) + EPS) * g


def _mla_prep_kernel(cq_ref, ckv_ref, kr_ref, gq_ref, gkv_ref, wq_ref, wkv_ref, cos_ref, sin_ref,
                     qo_ref, ko_ref, vo_ref, *, n_pos):
    scale = LOG2E / math.sqrt(B_NOPE + B_ROPE)
    cos = cos_ref[...]
    sin = sin_ref[...]
    cq = jnp.concatenate([cq_ref[s] for s in range(B_Q_LORA // LANE)], axis=1)
    q = jnp.dot(_rms(cq, gq_ref[...]).astype(BF16), wq_ref[...], preferred_element_type=F32)
    ckv = jnp.concatenate([ckv_ref[s] for s in range(B_KV_LORA // LANE)], axis=1)
    kv = jnp.dot(_rms(ckv, gkv_ref[...]).astype(BF16), wkv_ref[...], preferred_element_type=F32)
    tm = cos.shape[0]
    pos = (pl.program_id(0) % n_pos) * tm + lax.broadcasted_iota(jnp.int32, (tm, LANE), 0)
    chunk = (pos % TK) >> CHUNK_SHIFT
    lane = lax.broadcasted_iota(jnp.int32, (tm, LANE), 1)
    k_rope = (kr_ref[0] * cos + kr_ref[1] * sin + _chunk_onehot(chunk, lane, B_ROPE)).astype(BF16)
    mask_q = _chunk_maskq(chunk, lane, B_ROPE)
    n_h = B_HEADS * LANE
    for h in range(B_HEADS):
        sl = slice(h * LANE, (h + 1) * LANE)
        q_rope = q[:, n_h + h * LANE:n_h + (h + 1) * LANE] * cos + q[:, 2 * n_h + h * LANE:2 * n_h + (h + 1) * LANE] * sin
        qo_ref[h, :, :LANE] = (q[:, sl] * scale).astype(BF16)
        qo_ref[h, :, LANE:] = (q_rope * scale + mask_q).astype(BF16)
        ko_ref[h, :, :LANE] = kv[:, sl].astype(BF16)
        ko_ref[h, :, LANE:] = k_rope
        vo_ref[h] = kv[:, n_h + h * LANE:n_h + (h + 1) * LANE].astype(BF16)


def _mla_prep(of, gq, gkv, wq, wkv, cos, sin, seq):
    m = of.shape[1]
    tm = TM_PROJ
    n_pos = seq // tm
    nq = B_Q_LORA // LANE
    nkv = B_KV_LORA // LANE
    return pl.pallas_call(
        functools.partial(_mla_prep_kernel, n_pos=n_pos),
        grid=(m // tm,),
        in_specs=[
            pl.BlockSpec((nq, tm, LANE), lambda i: (FSLAB_CQ // nq, i, 0)),
            pl.BlockSpec((nkv, tm, LANE), lambda i: (FSLAB_CKV // nkv, i, 0)),
            pl.BlockSpec((2, tm, LANE), lambda i: (FSLAB_KROPE // 2, i, 0)),
            pl.BlockSpec((1, B_Q_LORA), lambda i: (0, 0)),
            pl.BlockSpec((1, B_KV_LORA), lambda i: (0, 0)),
            pl.BlockSpec(wq.shape, lambda i: (0, 0)),
            pl.BlockSpec(wkv.shape, lambda i: (0, 0)),
            pl.BlockSpec((tm, LANE), lambda i: (i % n_pos, 0)),
            pl.BlockSpec((tm, LANE), lambda i: (i % n_pos, 0)),
        ],
        out_specs=[
            pl.BlockSpec((B_HEADS, tm, 2 * LANE), lambda i: (0, i, 0)),
            pl.BlockSpec((B_HEADS, tm, 2 * LANE), lambda i: (0, i, 0)),
            pl.BlockSpec((B_HEADS, tm, LANE), lambda i: (0, i, 0)),
        ],
        out_shape=[jax.ShapeDtypeStruct((B_HEADS, m, 2 * LANE), BF16),
                   jax.ShapeDtypeStruct((B_HEADS, m, 2 * LANE), BF16),
                   jax.ShapeDtypeStruct((B_HEADS, m, LANE), BF16)],
        compiler_params=_cparams(1),
        name="mla_prep",
    )(of, of, of, gq, gkv, wq, wkv, cos, sin)


CUM_CHUNK = 128


def _fox_bias_kernel(af_ref, bf_ref, kb_ref, cum_scr):
    seq = af_ref.shape[1]
    r = lax.broadcasted_iota(jnp.int32, (CUM_CHUNK, CUM_CHUNK), 0)
    c = lax.broadcasted_iota(jnp.int32, (CUM_CHUNK, CUM_CHUNK), 1)
    tri = (r >= c).astype(F32)
    bvec = bf_ref[...]

    def body(t, carry):
        r0 = pl.multiple_of(t * CUM_CHUNK, CUM_CHUNK)
        x = af_ref[0, pl.ds(r0, CUM_CHUNK), :] + bvec
        logf = jnp.minimum(x, 0.0) - jnp.log1p(jnp.exp(-jnp.abs(x)))
        cs = jnp.dot(tri, logf, preferred_element_type=F32, precision=lax.Precision.HIGHEST) + carry
        cum_scr[pl.ds(r0, CUM_CHUNK), :] = cs
        return cs[CUM_CHUNK - 1:CUM_CHUNK, :]

    lax.fori_loop(0, seq // CUM_CHUNK, body, jnp.zeros((1, LANE), F32))
    lane = lax.broadcasted_iota(jnp.int32, (seq, LANE), 1)
    chunk = (lax.broadcasted_iota(jnp.int32, (seq, LANE), 0) % TK) >> CHUNK_SHIFT
    out = _chunk_onehot(chunk, lane, 0)
    rest = cum_scr[...] * (-LOG2E)
    for i in range(FOX_TERMS):
        term = rest.astype(BF16).astype(F32)
        rest = rest - term
        lo = AF_LANE + FOX_TERM_STRIDE * i
        moved = term if i == 0 else pltpu.roll(term, FOX_TERM_STRIDE * i, 1)
        out = jnp.where((lane >= lo) & (lane < lo + A_HEADS), moved, out)
    kb_ref[0] = out.astype(BF16)


def _fox_bias(of, b_forget, batch, seq):
    bvec = jnp.zeros((1, LANE), F32).at[0, AF_LANE:AF_LANE + A_HEADS].set(b_forget)
    of4 = of.reshape(N_F_SLABS, batch, seq, LANE)
    return pl.pallas_call(
        _fox_bias_kernel,
        grid=(batch,),
        in_specs=[
            pl.BlockSpec((None, 1, seq, LANE), lambda b: (N_F_SLABS - 1, b, 0, 0)),
            pl.BlockSpec((1, LANE), lambda b: (0, 0)),
        ],
        out_specs=pl.BlockSpec((1, seq, LANE), lambda b: (b, 0, 0)),
        out_shape=jax.ShapeDtypeStruct((batch, seq, LANE), BF16),
        scratch_shapes=[pltpu.VMEM((seq, LANE), F32)],
        compiler_params=_cparams(1),
        name="fox_bias",
    )(of4, bvec)


PAIR_Q, PAIR_K, PAIR_DIAG, PAIR_BIAS, PAIR_STATE = 0, 1, 2, 3, 4
ACC_ROWS = HEAD_DIM + 16
PAIRS_PER_TRIP = 4
PIPE_DIST = 2


def _pair_table(n_tiles):
    pad = (0, 0, 1, 0, n_tiles)
    rows = []
    for qi in range(n_tiles):
        for kj in range(qi + 1):
            rows.append((qi, kj, int(kj == qi), 0 if kj == qi else (1 if kj == qi - 1 else 2), qi))
    rows += [pad] * (-len(rows) % PAIRS_PER_TRIP)
    rows = [(0, 0, 0, 2, n_tiles)] * PIPE_DIST + rows + [pad] * PIPE_DIST
    return np.asarray(rows, np.int32).T.copy()


def _nt_dot(a, b):
    return lax.dot_general(a, b, (((1,), (1,)), ((), ())), preferred_element_type=F32)


def _tile_rows(ref, tile):
    return ref[pl.ds(pl.multiple_of(tile * TK, TK), TK), :]


def _load_vt(v_ref, vt_scr):
    extra = jnp.where(lax.broadcasted_iota(jnp.int32, (ACC_ROWS - HEAD_DIM, TK), 0) == 0, 1.0, 0.0).astype(BF16)
    for j in range(vt_scr.shape[0]):
        vt_scr[j, :HEAD_DIM, :] = v_ref[j * TK:(j + 1) * TK, :].T
        vt_scr[j, HEAD_DIM:, :] = extra


class _PairBufs:
    def __init__(self, s, mb, p, al):
        self.s, self.mb, self.p, self.al = s, mb, p, al


def _pair_scratch():
    return [pltpu.VMEM((TK, TQ), F32), pltpu.VMEM((1, TQ), F32), pltpu.VMEM((TK, TQ), BF16),
            pltpu.VMEM((1, TQ), F32)]


def _state_scratch(n_tiles):
    return [pltpu.VMEM((n_tiles + 1, 1, TQ), F32), pltpu.VMEM((n_tiles + 1, ACC_ROWS, TQ), F32)]


def _init_pipeline(m_all, acc_all, behind):
    m_all[...] = jnp.full(m_all.shape, NEG_BIG, F32)
    acc_all[...] = jnp.zeros(acc_all.shape, F32)
    for bufs in behind:
        bufs.p[...] = jnp.zeros(bufs.p.shape, BF16)
        bufs.al[...] = jnp.ones(bufs.al.shape, F32)


def _pair_bufs(pair_scr):
    return [_PairBufs(*pair_scr[4 * i:4 * i + 4]) for i in range(len(pair_scr) // 4)]


def _qk_store(s, bufs):
    bufs.s[...] = s
    bufs.mb[...] = jnp.max(s, axis=0, keepdims=True)


def _softmax_stage(tab_ref, t, bufs, m_all):
    st = tab_ref[PAIR_STATE, t]
    m_prev = m_all[st]
    m_new = jnp.maximum(m_prev, bufs.mb[...])
    bufs.al[...] = jnp.exp2(m_prev - m_new)
    bufs.p[...] = jnp.exp2(bufs.s[...] - m_new).astype(BF16)
    m_all[st] = m_new


def _pv_stage(tab_ref, t, bufs, vt_scr, acc_all):
    st = tab_ref[PAIR_STATE, t]
    acc_all[st] = bufs.al[...] * acc_all[st] + jnp.dot(vt_scr[tab_ref[PAIR_K, t]], bufs.p[...],
                                                        preferred_element_type=F32)


def _run_pipeline(tab_ref, bufs, qk_stage, sm_stage, pv_stage):
    n_pairs = tab_ref.shape[1] - 2 * PIPE_DIST
    u, d = PAIRS_PER_TRIP, PIPE_DIST
    assert u > d

    def body(i, carry):
        for j in range(u):
            c = u * i + d + j
            pv_stage(c - d, bufs[j % u])
            sm_stage(c, bufs[(d + j) % u])
            qk_stage(c + d, bufs[(2 * d + j) % u])
        return carry

    for c in range(d, 2 * d):
        qk_stage(c, bufs[c % u])
    lax.fori_loop(0, n_pairs // u, body, 0)
    for c in range(n_pairs, n_pairs + d):
        pv_stage(c, bufs[c % u])


def _normalized(acc_all, qi):
    return acc_all[qi, :HEAD_DIM, :] * (1.0 / acc_all[qi, HEAD_DIM:HEAD_DIM + 1, :])


def _silu(g):
    return g / (1.0 + jnp.exp(-g))


def _head_spec(width, slab0, n_heads, seq):
    return pl.BlockSpec((None, None, seq, width), lambda g: (slab0 + g % n_heads, g // n_heads, 0, 0))


_SMEM_SPEC = pl.BlockSpec(memory_space=pltpu.SMEM)


def _fox_attn_kernel(tab_ref, q_ref, k_ref, kb_ref, v_ref, g_ref, o_ref, vt_scr, qext_scr, m_all, acc_all, *pair_scr):
    bufs = _pair_bufs(pair_scr)
    _load_vt(v_ref, vt_scr)
    _init_pipeline(m_all, acc_all, bufs[:PIPE_DIST])
    row = lax.broadcasted_iota(jnp.int32, (TQ, LANE), 0)
    lane = lax.broadcasted_iota(jnp.int32, (TQ, LANE), 1)
    rel = lane - (AF_LANE + pl.program_id(0) % A_HEADS)
    ones3 = jnp.where((rel >= 0) & (rel < FOX_TERM_STRIDE * FOX_TERMS) & (rel % FOX_TERM_STRIDE == 0), 1.0, 0.0)
    qext_scr[0] = ones3.astype(BF16)
    qext_scr[1] = (ones3 + _chunk_maskq(row >> CHUNK_SHIFT, lane, 0)).astype(BF16)
    sub_key = lax.broadcasted_iota(jnp.int32, (LANE, LANE), 0)
    sub_qry = lax.broadcasted_iota(jnp.int32, (LANE, LANE), 1)

    def qk_stage(t, bufs):
        q_cat = jnp.concatenate([_tile_rows(q_ref, tab_ref[PAIR_Q, t]), qext_scr[tab_ref[PAIR_DIAG, t]]], axis=1)
        kj = tab_ref[PAIR_K, t]
        k_cat = jnp.concatenate([_tile_rows(k_ref, kj), _tile_rows(kb_ref, kj)], axis=1)
        s = _nt_dot(k_cat, q_cat)
        masked = sub_key - sub_qry > jnp.where(tab_ref[PAIR_DIAG, t] > 0, 0, TK)
        blocks = []
        for r in range(TK // LANE):
            blk = s[r * LANE:(r + 1) * LANE, :]
            mid = jnp.where(masked, NEG_BIG, blk[:, r * LANE:(r + 1) * LANE])
            parts = ([blk[:, :r * LANE]] if r > 0 else []) + [mid]
            parts += [blk[:, (r + 1) * LANE:]] if (r + 1) * LANE < TQ else []
            blocks.append(jnp.concatenate(parts, axis=1))
        _qk_store(jnp.concatenate(blocks, axis=0), bufs)

    _run_pipeline(tab_ref, bufs, qk_stage,
                  lambda t, bufs: _softmax_stage(tab_ref, t, bufs, m_all),
                  lambda t, bufs: _pv_stage(tab_ref, t, bufs, vt_scr, acc_all))
    for qi in range(m_all.shape[0] - 1):
        rows = slice(qi * TQ, (qi + 1) * TQ)
        o_ref[rows, :] = (_normalized(acc_all, qi).T * _silu(g_ref[rows, :])).astype(BF16)


def _fox_attn(tab, ob4, kb, of4, batch, seq):
    hs = functools.partial(_head_spec, n_heads=A_HEADS, seq=seq)
    n_tiles = seq // TQ
    return pl.pallas_call(
        _fox_attn_kernel,
        grid=(batch * A_HEADS,),
        in_specs=[_SMEM_SPEC, hs(LANE, SLAB_AQ), hs(LANE, SLAB_AK),
                  pl.BlockSpec((None, seq, LANE), lambda g: (g // A_HEADS, 0, 0)), hs(LANE, SLAB_AV),
                  hs(LANE, FSLAB_AGATE)],
        out_specs=hs(LANE, 0),
        out_shape=jax.ShapeDtypeStruct((A_HEADS, batch, seq, LANE), BF16),
        scratch_shapes=[pltpu.VMEM((n_tiles, ACC_ROWS, TK), BF16), pltpu.VMEM((2, TQ, LANE), BF16)]
        + _state_scratch(n_tiles) + PAIRS_PER_TRIP * _pair_scratch(),
        compiler_params=_cparams(1),
        name="fox_attn",
    )(tab, ob4, ob4, kb, ob4, of4)


def _mla_attn_kernel(tab_ref, q_ref, k_ref, v_ref, g_ref, o_ref, vt_scr, m_all, acc_all, *pair_scr):
    bufs = _pair_bufs(pair_scr)
    _load_vt(v_ref, vt_scr)
    _init_pipeline(m_all, acc_all, bufs[:PIPE_DIST])
    lane = lax.broadcasted_iota(jnp.int32, (TQ, LANE), 1)

    def qk_stage(t, bufs):
        q = _tile_rows(q_ref, tab_ref[PAIR_Q, t])
        first_off = jnp.where(tab_ref[PAIR_DIAG, t] > 0, LANE, B_ROPE)
        q_hi = jnp.where(lane >= first_off, jnp.zeros((), BF16), q[:, LANE:])
        q_cat = jnp.concatenate([q[:, :LANE], q_hi], axis=1)
        _qk_store(_nt_dot(_tile_rows(k_ref, tab_ref[PAIR_K, t]), q_cat), bufs)

    _run_pipeline(tab_ref, bufs, qk_stage,
                  lambda t, bufs: _softmax_stage(tab_ref, t, bufs, m_all),
                  lambda t, bufs: _pv_stage(tab_ref, t, bufs, vt_scr, acc_all))
    for qi in range(m_all.shape[0] - 1):
        rows = slice(qi * TQ, (qi + 1) * TQ)
        o_ref[rows, :] = (_normalized(acc_all, qi).T * _silu(g_ref[rows, :])).astype(BF16)


def _mla_attn(tab, q4, k4, v4, of4, batch, seq):
    hs = functools.partial(_head_spec, n_heads=B_HEADS, seq=seq)
    n_tiles = seq // TQ
    return pl.pallas_call(
        _mla_attn_kernel,
        grid=(batch * B_HEADS,),
        in_specs=[_SMEM_SPEC, hs(2 * LANE, 0), hs(2 * LANE, 0), hs(LANE, 0), hs(LANE, FSLAB_BGATE)],
        out_specs=hs(LANE, 0),
        out_shape=jax.ShapeDtypeStruct((B_HEADS, batch, seq, LANE), BF16),
        scratch_shapes=[pltpu.VMEM((n_tiles, ACC_ROWS, TK), BF16)]
        + _state_scratch(n_tiles) + PAIRS_PER_TRIP * _pair_scratch(),
        compiler_params=_cparams(1),
        name="mla_attn",
    )(tab, q4, k4, v4, of4)


def _diff_attn_kernel(tab_ref, q_ref, k_ref, v_ref, g_ref, bvec_ref, lq1_ref, lk1_ref, lq2_ref, lk2_ref, sg_ref,
                      o_ref, vt_scr, kext_scr, qext_scr, bias_scr, m1_all, acc1_all, m2_all, acc2_all, *pair_scr,
                      lam_init):
    flat = _pair_bufs(pair_scr)
    bufs = [(flat[2 * i], flat[2 * i + 1]) for i in range(PAIRS_PER_TRIP)]
    states = ((m1_all, acc1_all), (m2_all, acc2_all))
    _load_vt(v_ref, vt_scr)
    for k, (m_all, acc_all) in enumerate(states):
        _init_pipeline(m_all, acc_all, [b[k] for b in bufs[:PIPE_DIST]])
    for i in range(2):
        gen = jnp.broadcast_to(bvec_ref[i], (TK, 2 * TQ))
        bias_scr[i] = pltpu.roll(gen, 0, 1, stride=1, stride_axis=0)[:, :TQ]
    bias_scr[2] = jnp.zeros((TK, TQ), F32)
    row = lax.broadcasted_iota(jnp.int32, (TQ, LANE), 0)
    lane = lax.broadcasted_iota(jnp.int32, (TQ, LANE), 1)
    kext_scr[...] = _chunk_onehot(row >> CHUNK_SHIFT, lane, 0).astype(BF16)
    qext_scr[0] = jnp.zeros((TQ, LANE), BF16)
    qext_scr[1] = _chunk_maskq(row >> CHUNK_SHIFT, lane, 0).astype(BF16)
    lam = (jnp.exp(jnp.sum(lq1_ref[...] * lk1_ref[...], axis=-1, keepdims=True))
           - jnp.exp(jnp.sum(lq2_ref[...] * lk2_ref[...], axis=-1, keepdims=True)) + lam_init)

    def qk_stage(t, bufs2):
        q = _tile_rows(q_ref, tab_ref[PAIR_Q, t])
        qext = qext_scr[tab_ref[PAIR_DIAG, t]]
        zero = jnp.zeros_like(q)
        k_cat = jnp.concatenate([_tile_rows(k_ref, tab_ref[PAIR_K, t]), kext_scr[...]], axis=1)
        bias = bias_scr[tab_ref[PAIR_BIAS, t]]
        for bufs, q_map in zip(bufs2, (jnp.where(lane < C_QK, q, zero), jnp.where(lane >= C_QK, q, zero))):
            _qk_store(_nt_dot(k_cat, jnp.concatenate([q_map, qext], axis=1)) + bias, bufs)

    def sm_stage(t, bufs2):
        for (m_all, _), bufs in zip(states, bufs2):
            _softmax_stage(tab_ref, t, bufs, m_all)

    def pv_stage(t, bufs2):
        for (_, acc_all), bufs in zip(states, bufs2):
            _pv_stage(tab_ref, t, bufs, vt_scr, acc_all)

    _run_pipeline(tab_ref, bufs, qk_stage, sm_stage, pv_stage)
    for qi in range(m1_all.shape[0] - 1):
        rows = slice(qi * TQ, (qi + 1) * TQ)
        o = _normalized(acc1_all, qi) - lam * _normalized(acc2_all, qi)
        o = o * lax.rsqrt(jnp.mean(o * o, axis=0, keepdims=True) + EPS)
        out = o.T * sg_ref[...] * (1.0 - lam_init)
        o_ref[rows, :] = (out * _silu(g_ref[rows, :])).astype(BF16)


def _diff_attn(tab, ob4, of4, bias, lq1, lk1, lq2, lk2, sg, lam_init, batch, seq):
    hs = functools.partial(_head_spec, n_heads=C_HEADS, seq=seq)
    vec = lambda n: pl.BlockSpec((1, n), lambda g: (0, 0))
    n_tiles = seq // TQ
    return pl.pallas_call(
        functools.partial(_diff_attn_kernel, lam_init=lam_init),
        grid=(batch * C_HEADS,),
        in_specs=[_SMEM_SPEC, hs(LANE, SLAB_CQ), hs(LANE, SLAB_CK), hs(LANE, SLAB_CV), hs(LANE, FSLAB_CGATE),
                  pl.BlockSpec((None, 2, 1, 2 * TQ), lambda g: (g % C_HEADS, 0, 0, 0)),
                  vec(C_QK), vec(C_QK), vec(C_QK), vec(C_QK), vec(C_V)],
        out_specs=hs(LANE, 0),
        out_shape=jax.ShapeDtypeStruct((C_HEADS, batch, seq, LANE), BF16),
        scratch_shapes=[pltpu.VMEM((n_tiles, ACC_ROWS, TK), BF16), pltpu.VMEM((TK, LANE), BF16),
                        pltpu.VMEM((2, TQ, LANE), BF16), pltpu.VMEM((3, TK, TQ), F32)]
        + 2 * _state_scratch(n_tiles) + 2 * PAIRS_PER_TRIP * _pair_scratch(),
        compiler_params=_cparams(1),
        name="diff_attn",
    )(tab, ob4, ob4, ob4, of4, bias, lq1, lk1, lq2, lk2, sg)


def _outproj_kernel(a_ref, b_ref, c_ref, w_ref, x_ref, g_ref, o_ref, *, final_norm):
    mixed = jnp.concatenate([a_ref[h] for h in range(A_HEADS)] + [b_ref[h] for h in range(B_HEADS)]
                            + [c_ref[h] for h in range(C_HEADS)], axis=1)
    y = x_ref[...] + jnp.dot(mixed, w_ref[...], preferred_element_type=F32)
    if final_norm:
        y = _rms(y, g_ref[...])
    o_ref[...] = y


def _outproj(mix_a, mix_b, mix_c, w, x2, g, final_norm):
    m = x2.shape[0]
    tm = TM_PROJ
    return pl.pallas_call(
        functools.partial(_outproj_kernel, final_norm=final_norm),
        grid=(m // tm,),
        in_specs=[
            pl.BlockSpec((A_HEADS, tm, LANE), lambda i: (0, i, 0)),
            pl.BlockSpec((B_HEADS, tm, LANE), lambda i: (0, i, 0)),
            pl.BlockSpec((C_HEADS, tm, LANE), lambda i: (0, i, 0)),
            pl.BlockSpec((D_MIX, D_MODEL), lambda i: (0, 0)),
            pl.BlockSpec((tm, D_MODEL), lambda i: (i, 0)),
            pl.BlockSpec((1, D_MODEL), lambda i: (0, 0)),
        ],
        out_specs=pl.BlockSpec((tm, D_MODEL), lambda i: (i, 0)),
        out_shape=jax.ShapeDtypeStruct((m, D_MODEL), F32),
        compiler_params=_cparams(1),
        name="outproj",
    )(mix_a, mix_b, mix_c, w, x2, g)


def kernel(x, norm_g, w_in, b_forget, mla_q_norm_g, w_uq, mla_kv_norm_g, w_ukv, lambda_q1, lambda_k1,
           lambda_q2, lambda_k2, diff_subln_g, rel_bias, w_out, final_norm_g):
    batch, seq, d = x.shape
    assert d == D_MODEL and seq % TQ == 0 and seq % TM_PROJ == 0 and (batch * seq) % TM_IN == 0
    m = batch * seq
    x2 = x.reshape(m, d)
    cs = _proj_col_scale()
    cos, sin = _rope_tables(seq)
    bias_vecs = _t5_bias_vectors(rel_bias)
    tab = jnp.asarray(_pair_table(seq // TQ))
    for l in range(N_LAYERS):
        ob, of = _inproj(x2, norm_g[l].reshape(1, d), _prep_w_in(w_in[l]), cs)
        ob4 = ob.reshape(N_BF_SLABS, batch, seq, LANE)
        of4 = of.reshape(N_F_SLABS, batch, seq, LANE)
        kb = _fox_bias(of, b_forget[l], batch, seq)
        mix_a = _fox_attn(tab, ob4, kb, of4, batch, seq)
        qb, kbm, vb = _mla_prep(of, mla_q_norm_g[l].reshape(1, -1), mla_kv_norm_g[l].reshape(1, -1),
                                _prep_w_uq(w_uq[l]), _prep_w_ukv(w_ukv[l]), cos, sin, seq)
        r4 = lambda a: a.reshape(a.shape[0], batch, seq, a.shape[-1])
        mix_b = _mla_attn(tab, r4(qb), r4(kbm), r4(vb), of4, batch, seq)
        lam_init = 0.8 - 0.6 * math.exp(-0.3 * l)
        mix_c = _diff_attn(tab, ob4, of4, bias_vecs, lambda_q1[l].reshape(1, -1), lambda_k1[l].reshape(1, -1),
                           lambda_q2[l].reshape(1, -1), lambda_k2[l].reshape(1, -1),
                           diff_subln_g[l].reshape(1, -1), lam_init, batch, seq)
        x2 = _outproj(mix_a.reshape(A_HEADS, m, LANE), mix_b.reshape(B_HEADS, m, LANE),
                      mix_c.reshape(C_HEADS, m, LANE), w_out[l].astype(BF16), x2,
                      final_norm_g.reshape(1, d), final_norm=(l == N_LAYERS - 1))
    return x2.reshape(batch, seq, d)
```

```python
import functools
import math

import numpy as np
import jax
import jax.numpy as jnp
from jax import lax
from jax.experimental import pallas as pl
from jax.experimental.pallas import tpu as pltpu

F32 = jnp.float32
BF16 = jnp.bfloat16

D_MODEL = 2048
N_LAYERS = 2
CHUNK = 64
HEAD_DIM = 128
EPS = 1e-6
A_HEADS = 6
A_DIM = A_HEADS * HEAD_DIM
B_HEADS = 6
B_Q_LORA = 512
B_KV_LORA = 256
B_NOPE = 128
B_ROPE = 64
B_V = 128
B_DIM = B_HEADS * B_V
ROPE_THETA = 10000.0
C_HEADS = 4
C_QK = 64
C_V = 2 * C_QK
C_DIM = C_HEADS * C_V
REL_BUCKETS = 32
REL_MAX_DIST = 128
D_MIX = A_DIM + B_DIM + C_DIM

LANE = 128
V7X_VMEM_BYTES = 64 * 1024 * 1024
VMEM_LIMIT = 56 * 1024 * 1024

LOG2E = math.log2(math.e)
NEG_BIG = -1e30

TN = 768
SLABS_PER_TILE = TN // LANE
N_BF_SLABS = 30
N_F_SLABS = 24
N_BF_TILES = N_BF_SLABS // SLABS_PER_TILE
N_F_TILES = N_F_SLABS // SLABS_PER_TILE
N_PROJ = (N_BF_SLABS + N_F_SLABS) * LANE
SLAB_AQ, SLAB_AK, SLAB_AV, SLAB_CQ, SLAB_CK, SLAB_CV = 0, 6, 12, 18, 22, 26
FSLAB_AGATE, FSLAB_BGATE, FSLAB_CGATE, FSLAB_CQ, FSLAB_CKV, FSLAB_KROPE = 0, 6, 12, 16, 20, 22
AF_LANE = 64

TM_PROJ = 512
TM_IN = 1024
TQ = 512
TK = 512

CHUNK_SHIFT = CHUNK.bit_length() - 1
N_TILE_CHUNKS = TK // CHUNK
FOX_TERM_STRIDE = A_HEADS
FOX_TERMS = 3


def _chunk_onehot(row_chunk, lane, lane0):
    return jnp.where(lane - lane0 == row_chunk, 1.0, 0.0)


def _chunk_maskq(row_chunk, lane, lane0):
    c = lane - lane0
    return jnp.where((c >= 0) & (c < N_TILE_CHUNKS) & (row_chunk < c), NEG_BIG, 0.0)


def _cparams(n_grid):
    return pltpu.CompilerParams(dimension_semantics=("arbitrary",) * n_grid,
                                vmem_limit_bytes=VMEM_LIMIT)


def _w_in_slab_sources():
    o = 0
    seg = {}
    for name, size in (("a_q", A_DIM), ("a_k", A_DIM), ("a_v", A_DIM), ("a_f", A_HEADS), ("a_gate", A_DIM),
                       ("b_cq", B_Q_LORA), ("b_ckv", B_KV_LORA), ("b_krope", B_ROPE), ("b_gate", B_DIM),
                       ("c_q", C_DIM), ("c_k", C_DIM), ("c_v", C_DIM), ("c_gate", C_DIM)):
        seg[name] = (o, size)
        o += size
    slabs = []
    for name in ("a_q", "a_k", "a_v", "c_q", "c_k", "c_v", "a_gate", "b_gate", "c_gate", "b_cq", "b_ckv"):
        start, size = seg[name]
        slabs += [[(start + i, start + i + LANE, 1)] for i in range(0, size, LANE)]
    kr, half = seg["b_krope"][0], B_ROPE // 2
    af = seg["a_f"][0]
    slabs.append([(kr, kr + B_ROPE, 1), (None, LANE - B_ROPE, 0)])
    slabs.append([(kr + half, kr + B_ROPE, -1), (kr, kr + half, 1), (af, af + A_HEADS, 1),
                  (None, LANE - B_ROPE - A_HEADS, 0)])
    assert len(slabs) * LANE == N_PROJ
    return slabs


def _w_in_kernel(w_ref, o_ref):
    for j, pieces in enumerate(_w_in_slab_sources()):
        cols = []
        for start, stop, sign in pieces:
            if start is None:
                cols.append(jnp.zeros((w_ref.shape[0], stop), F32))
            else:
                cols.append(w_ref[:, start:stop] if sign > 0 else -w_ref[:, start:stop])
        slab = cols[0] if len(cols) == 1 else jnp.concatenate(cols, axis=1)
        o_ref[:, j * LANE:(j + 1) * LANE] = slab.astype(BF16)


def _prep_w_in(w):
    n_layers, d, n = w.shape
    tk = 256
    return pl.pallas_call(
        _w_in_kernel,
        grid=(n_layers, d // tk),
        in_specs=[pl.BlockSpec((None, tk, n), lambda l, i: (l, i, 0))],
        out_specs=pl.BlockSpec((None, tk, N_PROJ), lambda l, i: (l, i, 0)),
        out_shape=jax.ShapeDtypeStruct((n_layers, d, N_PROJ), BF16),
        compiler_params=_cparams(2),
        name="w_in_reorder",
    )(w)


def _proj_col_scale():
    s = jnp.ones((N_PROJ,), F32)
    s = s.at[SLAB_AQ * LANE:(SLAB_AQ + A_HEADS) * LANE].set(LOG2E / math.sqrt(HEAD_DIM))
    s = s.at[SLAB_CQ * LANE:(SLAB_CQ + C_HEADS) * LANE].set(LOG2E / math.sqrt(C_QK))
    return s.reshape(1, N_PROJ)


def _prep_w_uq(w):
    w = w.reshape(B_Q_LORA, B_HEADS, B_NOPE + B_ROPE)
    nope = w[:, :, :B_NOPE].reshape(B_Q_LORA, B_HEADS * B_NOPE)
    rope = w[:, :, B_NOPE:]
    half = B_ROPE // 2
    rot = jnp.concatenate([-rope[:, :, half:], rope[:, :, :half]], axis=2)
    pad = jnp.zeros((B_Q_LORA, B_HEADS, LANE - B_ROPE), w.dtype)
    rope_p = jnp.concatenate([rope, pad], axis=2).reshape(B_Q_LORA, B_HEADS * LANE)
    rot_p = jnp.concatenate([rot, pad], axis=2).reshape(B_Q_LORA, B_HEADS * LANE)
    return jnp.concatenate([nope, rope_p, rot_p], axis=1).astype(BF16)


def _prep_w_ukv(w):
    w = w.reshape(B_KV_LORA, B_HEADS, B_NOPE + B_V)
    k = w[:, :, :B_NOPE].reshape(B_KV_LORA, B_HEADS * B_NOPE)
    v = w[:, :, B_NOPE:].reshape(B_KV_LORA, B_HEADS * B_V)
    return jnp.concatenate([k, v], axis=1).astype(BF16)


def _rope_tables(seq):
    half = B_ROPE // 2
    inv = ROPE_THETA ** (-jnp.arange(half, dtype=F32) / half)
    ang = jnp.arange(seq).astype(F32)[:, None] * inv[None, :]
    pad = jnp.zeros((seq, LANE - B_ROPE), F32)
    cos = jnp.concatenate([jnp.cos(ang), jnp.cos(ang), pad], axis=1)
    sin = jnp.concatenate([jnp.sin(ang), jnp.sin(ang), pad], axis=1)
    return cos, sin


def _t5_bucket(rel):
    nb = REL_BUCKETS // 2
    max_exact = nb // 2
    ret = (rel > 0).astype(jnp.int32) * nb
    n = jnp.abs(rel)
    nf = jnp.maximum(n, 1).astype(F32)
    large = max_exact + (jnp.log(nf / max_exact) / math.log(REL_MAX_DIST / max_exact)
                         * (nb - max_exact)).astype(jnp.int32)
    large = jnp.minimum(large, nb - 1)
    return ret + jnp.where(n < max_exact, n, large)


def _t5_bias_vectors(rel_bias):
    assert TQ == TK and TQ > REL_MAX_DIST
    x = jnp.arange(2 * TQ)
    x = jnp.where(x < TQ, x, x - 2 * TQ)
    bucket = jnp.stack([_t5_bucket(-x - delta) for delta in (0, TQ)])
    rel = rel_bias - rel_bias[REL_BUCKETS // 2 - 1][None, :]
    t = jnp.zeros((C_HEADS,) + bucket.shape, F32)
    for b in range(REL_BUCKETS):
        t = jnp.where(bucket[None] == b, rel[b][:, None, None], t)
    return (t * LOG2E).reshape(C_HEADS, 2, 1, 2 * TQ)


def _inproj_kernel(x_ref, g_ref, w_ref, cs_ref, ob_ref, of_ref, h_scr):
    j = pl.program_id(1)

    @pl.when(j == 0)
    def _():
        x = x_ref[...]
        ms = jnp.mean(x * x, axis=-1, keepdims=True)
        h_scr[...] = (x * lax.rsqrt(ms + EPS) * g_ref[...]).astype(BF16)

    def project(out_ref):
        acc = jnp.dot(h_scr[...], w_ref[...], preferred_element_type=F32) * cs_ref[...]
        for s in range(SLABS_PER_TILE):
            out_ref[s] = acc[:, s * LANE:(s + 1) * LANE].astype(out_ref.dtype)

    @pl.when(j < N_BF_TILES)
    def _():
        project(ob_ref)

    @pl.when(j >= N_BF_TILES)
    def _():
        project(of_ref)


def _inproj(x2, g, w, cs, layer):
    m = x2.shape[0]
    tm = TM_IN
    return pl.pallas_call(
        _inproj_kernel,
        grid=(m // tm, N_BF_TILES + N_F_TILES),
        in_specs=[
            pl.BlockSpec((tm, D_MODEL), lambda i, j: (i, 0)),
            pl.BlockSpec((1, D_MODEL), lambda i, j: (0, 0)),
            pl.BlockSpec((None, D_MODEL, TN), lambda i, j: (layer, 0, j)),
            pl.BlockSpec((1, TN), lambda i, j: (0, j)),
        ],
        out_specs=[
            pl.BlockSpec((SLABS_PER_TILE, tm, LANE), lambda i, j: (jnp.minimum(j, N_BF_TILES - 1), i, 0)),
            pl.BlockSpec((SLABS_PER_TILE, tm, LANE), lambda i, j: (jnp.maximum(j - N_BF_TILES, 0), i, 0)),
        ],
        out_shape=[jax.ShapeDtypeStruct((N_BF_SLABS, m, LANE), BF16),
                   jax.ShapeDtypeStruct((N_F_SLABS, m, LANE), F32)],
        scratch_shapes=[pltpu.VMEM((tm, D_MODEL), BF16)],
        compiler_params=_cparams(2),
        name="inproj",
    )(x2, g, w, cs)


def _rms(x, g):
    return x * lax.rsqrt(jnp.mean(x * x, axis=-1, keepdims=True) + EPS) * g


def _mla_prep_kernel(cq_ref, ckv_ref, kr_ref, gq_ref, gkv_ref, wq_ref, wkv_ref, cos_ref, sin_ref,
                     qo_ref, ko_ref, vo_ref, *, n_pos):
    scale = LOG2E / math.sqrt(B_NOPE + B_ROPE)
    cos = cos_ref[...]
    sin = sin_ref[...]
    cq = jnp.concatenate([cq_ref[s] for s in range(B_Q_LORA // LANE)], axis=1)
    q = jnp.dot(_rms(cq, gq_ref[...]).astype(BF16), wq_ref[...], preferred_element_type=F32)
    ckv = jnp.concatenate([ckv_ref[s] for s in range(B_KV_LORA // LANE)], axis=1)
    kv = jnp.dot(_rms(ckv, gkv_ref[...]).astype(BF16), wkv_ref[...], preferred_element_type=F32)
    tm = cos.shape[0]
    pos = (pl.program_id(0) % n_pos) * tm + lax.broadcasted_iota(jnp.int32, (tm, LANE), 0)
    chunk = (pos % TK) >> CHUNK_SHIFT
    lane = lax.broadcasted_iota(jnp.int32, (tm, LANE), 1)
    k_rope = (kr_ref[0] * cos + kr_ref[1] * sin + _chunk_onehot(chunk, lane, B_ROPE)).astype(BF16)
    mask_q = _chunk_maskq(chunk, lane, B_ROPE)
    n_h = B_HEADS * LANE
    for h in range(B_HEADS):
        sl = slice(h * LANE, (h + 1) * LANE)
        q_rope = q[:, n_h + h * LANE:n_h + (h + 1) * LANE] * cos + q[:, 2 * n_h + h * LANE:2 * n_h + (h + 1) * LANE] * sin
        qo_ref[h, :, :LANE] = (q[:, sl] * scale).astype(BF16)
        qo_ref[h, :, LANE:] = (q_rope * scale + mask_q).astype(BF16)
        ko_ref[h, :, :LANE] = kv[:, sl].astype(BF16)
        ko_ref[h, :, LANE:] = k_rope
        vo_ref[h] = kv[:, n_h + h * LANE:n_h + (h + 1) * LANE].astype(BF16)


def _mla_prep(of, gq, gkv, wq, wkv, cos, sin, seq):
    m = of.shape[1]
    tm = TM_PROJ
    n_pos = seq // tm
    nq = B_Q_LORA // LANE
    nkv = B_KV_LORA // LANE
    return pl.pallas_call(
        functools.partial(_mla_prep_kernel, n_pos=n_pos),
        grid=(m // tm,),
        in_specs=[
            pl.BlockSpec((nq, tm, LANE), lambda i: (FSLAB_CQ // nq, i, 0)),
            pl.BlockSpec((nkv, tm, LANE), lambda i: (FSLAB_CKV // nkv, i, 0)),
            pl.BlockSpec((2, tm, LANE), lambda i: (FSLAB_KROPE // 2, i, 0)),
            pl.BlockSpec((1, B_Q_LORA), lambda i: (0, 0)),
            pl.BlockSpec((1, B_KV_LORA), lambda i: (0, 0)),
            pl.BlockSpec(wq.shape, lambda i: (0, 0)),
            pl.BlockSpec(wkv.shape, lambda i: (0, 0)),
            pl.BlockSpec((tm, LANE), lambda i: (i % n_pos, 0)),
            pl.BlockSpec((tm, LANE), lambda i: (i % n_pos, 0)),
        ],
        out_specs=[
            pl.BlockSpec((B_HEADS, tm, 2 * LANE), lambda i: (0, i, 0)),
            pl.BlockSpec((B_HEADS, tm, 2 * LANE), lambda i: (0, i, 0)),
            pl.BlockSpec((B_HEADS, tm, LANE), lambda i: (0, i, 0)),
        ],
        out_shape=[jax.ShapeDtypeStruct((B_HEADS, m, 2 * LANE), BF16),
                   jax.ShapeDtypeStruct((B_HEADS, m, 2 * LANE), BF16),
                   jax.ShapeDtypeStruct((B_HEADS, m, LANE), BF16)],
        compiler_params=_cparams(1),
        name="mla_prep",
    )(of, of, of, gq, gkv, wq, wkv, cos, sin)


CUM_CHUNK = 128


def _fox_bias_kernel(af_ref, bf_ref, kb_ref, cum_scr):
    seq = af_ref.shape[1]
    r = lax.broadcasted_iota(jnp.int32, (CUM_CHUNK, CUM_CHUNK), 0)
    c = lax.broadcasted_iota(jnp.int32, (CUM_CHUNK, CUM_CHUNK), 1)
    tri = (r >= c).astype(F32)
    bvec = bf_ref[...]

    def body(t, carry):
        r0 = pl.multiple_of(t * CUM_CHUNK, CUM_CHUNK)
        x = af_ref[0, pl.ds(r0, CUM_CHUNK), :] + bvec
        logf = jnp.minimum(x, 0.0) - jnp.log1p(jnp.exp(-jnp.abs(x)))
        cs = jnp.dot(tri, logf, preferred_element_type=F32, precision=lax.Precision.HIGHEST) + carry
        cum_scr[pl.ds(r0, CUM_CHUNK), :] = cs
        return cs[CUM_CHUNK - 1:CUM_CHUNK, :]

    lax.fori_loop(0, seq // CUM_CHUNK, body, jnp.zeros((1, LANE), F32))
    lane = lax.broadcasted_iota(jnp.int32, (seq, LANE), 1)
    chunk = (lax.broadcasted_iota(jnp.int32, (seq, LANE), 0) % TK) >> CHUNK_SHIFT
    out = _chunk_onehot(chunk, lane, 0)
    rest = cum_scr[...] * (-LOG2E)
    for i in range(FOX_TERMS):
        term = rest.astype(BF16).astype(F32)
        rest = rest - term
        lo = AF_LANE + FOX_TERM_STRIDE * i
        moved = term if i == 0 else pltpu.roll(term, FOX_TERM_STRIDE * i, 1)
        out = jnp.where((lane >= lo) & (lane < lo + A_HEADS), moved, out)
    kb_ref[0] = out.astype(BF16)


def _fox_bias(of, b_forget, batch, seq):
    bvec = jnp.zeros((1, LANE), F32).at[0, AF_LANE:AF_LANE + A_HEADS].set(b_forget)
    of4 = of.reshape(N_F_SLABS, batch, seq, LANE)
    return pl.pallas_call(
        _fox_bias_kernel,
        grid=(batch,),
        in_specs=[
            pl.BlockSpec((None, 1, seq, LANE), lambda b: (N_F_SLABS - 1, b, 0, 0)),
            pl.BlockSpec((1, LANE), lambda b: (0, 0)),
        ],
        out_specs=pl.BlockSpec((1, seq, LANE), lambda b: (b, 0, 0)),
        out_shape=jax.ShapeDtypeStruct((batch, seq, LANE), BF16),
        scratch_shapes=[pltpu.VMEM((seq, LANE), F32)],
        compiler_params=_cparams(1),
        name="fox_bias",
    )(of4, bvec)


PAIR_Q, PAIR_K, PAIR_DIAG, PAIR_BIAS, PAIR_STATE = 0, 1, 2, 3, 4
ACC_ROWS = HEAD_DIM + 16
PAIRS_PER_TRIP = 4
PIPE_DIST = 2


def _pair_table(n_tiles):
    pad = (0, 0, 1, 0, n_tiles)
    rows = []
    for qi in range(n_tiles):
        for kj in range(qi + 1):
            rows.append((qi, kj, int(kj == qi), 0 if kj == qi else (1 if kj == qi - 1 else 2), qi))
    rows += [pad] * (-len(rows) % PAIRS_PER_TRIP)
    rows = rows + [pad] * PIPE_DIST
    return np.asarray(rows, np.int32).T.copy()


def _nt_dot(a, b):
    return lax.dot_general(a, b, (((1,), (1,)), ((), ())), preferred_element_type=F32)


def _tile_rows(ref, tile):
    return ref[pl.ds(pl.multiple_of(tile * TK, TK), TK), :]


def _load_vt(v_ref, vt_scr):
    extra = jnp.where(lax.broadcasted_iota(jnp.int32, (ACC_ROWS - HEAD_DIM, TK), 0) == 0, 1.0, 0.0).astype(BF16)
    for j in range(vt_scr.shape[0]):
        vt_scr[j, :HEAD_DIM, :] = v_ref[j * TK:(j + 1) * TK, :].T
        vt_scr[j, HEAD_DIM:, :] = extra


class _PairBufs:
    def __init__(self, s, mb):
        self.s, self.mb = s, mb


def _pair_scratch():
    return [pltpu.VMEM((TK, TQ), F32), pltpu.VMEM((1, TQ), F32)]


def _state_scratch(n_tiles):
    return [pltpu.VMEM((n_tiles + 1, 1, TQ), F32), pltpu.VMEM((n_tiles + 1, ACC_ROWS, TQ), F32)]


def _init_pipeline(m_all, acc_all):
    m_all[...] = jnp.full(m_all.shape, NEG_BIG, F32)
    acc_all[...] = jnp.zeros(acc_all.shape, F32)


def _pair_bufs(pair_scr):
    return [_PairBufs(*pair_scr[2 * i:2 * i + 2]) for i in range(len(pair_scr) // 2)]


def _qk_store(s, bufs):
    bufs.s[...] = s
    bufs.mb[...] = jnp.max(s, axis=0, keepdims=True)


def _softmax_pv_stage(tab_ref, t, bufs, m_all, vt_scr, acc_all):
    st = tab_ref[PAIR_STATE, t]
    m_prev = m_all[st]
    m_new = jnp.maximum(m_prev, bufs.mb[...])
    alpha = jnp.exp2(m_prev - m_new)
    p = jnp.exp2(bufs.s[...] - m_new).astype(BF16)
    m_all[st] = m_new
    acc_all[st] = alpha * acc_all[st] + jnp.dot(vt_scr[tab_ref[PAIR_K, t]], p, preferred_element_type=F32)


def _run_pipeline(tab_ref, bufs, qk_stage, smpv_stage):
    n_pairs = tab_ref.shape[1] - PIPE_DIST
    u, d = PAIRS_PER_TRIP, PIPE_DIST
    assert u > d

    def body(i, carry):
        for j in range(u):
            c = u * i + j
            smpv_stage(c, bufs[j % u])
            qk_stage(c + d, bufs[(d + j) % u])
        return carry

    for c in range(d):
        qk_stage(c, bufs[c % u])
    lax.fori_loop(0, n_pairs // u, body, 0)


def _normalized(acc_all, qi):
    return acc_all[qi, :HEAD_DIM, :] * (1.0 / acc_all[qi, HEAD_DIM:HEAD_DIM + 1, :])


def _silu(g):
    return g / (1.0 + jnp.exp(-g))


def _head_spec(width, slab0, n_heads, seq):
    return pl.BlockSpec((None, None, seq, width), lambda g: (slab0 + g % n_heads, g // n_heads, 0, 0))


_SMEM_SPEC = pl.BlockSpec(memory_space=pltpu.SMEM)


def _fox_attn_kernel(tab_ref, q_ref, k_ref, kb_ref, v_ref, g_ref, o_ref, vt_scr, qext_scr, m_all, acc_all, *pair_scr):
    bufs = _pair_bufs(pair_scr)
    _load_vt(v_ref, vt_scr)
    _init_pipeline(m_all, acc_all)
    row = lax.broadcasted_iota(jnp.int32, (TQ, LANE), 0)
    lane = lax.broadcasted_iota(jnp.int32, (TQ, LANE), 1)
    rel = lane - (AF_LANE + pl.program_id(0) % A_HEADS)
    ones3 = jnp.where((rel >= 0) & (rel < FOX_TERM_STRIDE * FOX_TERMS) & (rel % FOX_TERM_STRIDE == 0), 1.0, 0.0)
    qext_scr[0] = ones3.astype(BF16)
    qext_scr[1] = (ones3 + _chunk_maskq(row >> CHUNK_SHIFT, lane, 0)).astype(BF16)
    sub_key = lax.broadcasted_iota(jnp.int32, (LANE, LANE), 0)
    sub_qry = lax.broadcasted_iota(jnp.int32, (LANE, LANE), 1)

    def qk_stage(t, bufs):
        q_cat = jnp.concatenate([_tile_rows(q_ref, tab_ref[PAIR_Q, t]), qext_scr[tab_ref[PAIR_DIAG, t]]], axis=1)
        kj = tab_ref[PAIR_K, t]
        k_cat = jnp.concatenate([_tile_rows(k_ref, kj), _tile_rows(kb_ref, kj)], axis=1)
        s = _nt_dot(k_cat, q_cat)
        masked = sub_key - sub_qry > jnp.where(tab_ref[PAIR_DIAG, t] > 0, 0, TK)
        blocks = []
        for r in range(TK // LANE):
            blk = s[r * LANE:(r + 1) * LANE, :]
            mid = jnp.where(masked, NEG_BIG, blk[:, r * LANE:(r + 1) * LANE])
            parts = ([blk[:, :r * LANE]] if r > 0 else []) + [mid]
            parts += [blk[:, (r + 1) * LANE:]] if (r + 1) * LANE < TQ else []
            blocks.append(jnp.concatenate(parts, axis=1))
        _qk_store(jnp.concatenate(blocks, axis=0), bufs)

    _run_pipeline(tab_ref, bufs, qk_stage,
                  lambda t, bufs: _softmax_pv_stage(tab_ref, t, bufs, m_all, vt_scr, acc_all))
    for qi in range(m_all.shape[0] - 1):
        rows = slice(qi * TQ, (qi + 1) * TQ)
        o_ref[rows, :] = (_normalized(acc_all, qi).T * _silu(g_ref[rows, :])).astype(BF16)


def _fox_attn(tab, ob4, kb, of4, batch, seq):
    hs = functools.partial(_head_spec, n_heads=A_HEADS, seq=seq)
    n_tiles = seq // TQ
    return pl.pallas_call(
        _fox_attn_kernel,
        grid=(batch * A_HEADS,),
        in_specs=[_SMEM_SPEC, hs(LANE, SLAB_AQ), hs(LANE, SLAB_AK),
                  pl.BlockSpec((None, seq, LANE), lambda g: (g // A_HEADS, 0, 0)), hs(LANE, SLAB_AV),
                  hs(LANE, FSLAB_AGATE)],
        out_specs=hs(LANE, 0),
        out_shape=jax.ShapeDtypeStruct((A_HEADS, batch, seq, LANE), BF16),
        scratch_shapes=[pltpu.VMEM((n_tiles, ACC_ROWS, TK), BF16), pltpu.VMEM((2, TQ, LANE), BF16)]
        + _state_scratch(n_tiles) + PAIRS_PER_TRIP * _pair_scratch(),
        compiler_params=_cparams(1),
        name="fox_attn",
    )(tab, ob4, ob4, kb, ob4, of4)


def _mla_attn_kernel(tab_ref, q_ref, k_ref, v_ref, g_ref, o_ref, vt_scr, m_all, acc_all, *pair_scr):
    bufs = _pair_bufs(pair_scr)
    _load_vt(v_ref, vt_scr)
    _init_pipeline(m_all, acc_all)
    lane = lax.broadcasted_iota(jnp.int32, (TQ, LANE), 1)

    def qk_stage(t, bufs):
        q = _tile_rows(q_ref, tab_ref[PAIR_Q, t])
        first_off = jnp.where(tab_ref[PAIR_DIAG, t] > 0, LANE, B_ROPE)
        q_hi = jnp.where(lane >= first_off, jnp.zeros((), BF16), q[:, LANE:])
        q_cat = jnp.concatenate([q[:, :LANE], q_hi], axis=1)
        _qk_store(_nt_dot(_tile_rows(k_ref, tab_ref[PAIR_K, t]), q_cat), bufs)

    _run_pipeline(tab_ref, bufs, qk_stage,
                  lambda t, bufs: _softmax_pv_stage(tab_ref, t, bufs, m_all, vt_scr, acc_all))
    for qi in range(m_all.shape[0] - 1):
        rows = slice(qi * TQ, (qi + 1) * TQ)
        o_ref[rows, :] = (_normalized(acc_all, qi).T * _silu(g_ref[rows, :])).astype(BF16)


def _mla_attn(tab, q4, k4, v4, of4, batch, seq):
    hs = functools.partial(_head_spec, n_heads=B_HEADS, seq=seq)
    n_tiles = seq // TQ
    return pl.pallas_call(
        _mla_attn_kernel,
        grid=(batch * B_HEADS,),
        in_specs=[_SMEM_SPEC, hs(2 * LANE, 0), hs(2 * LANE, 0), hs(LANE, 0), hs(LANE, FSLAB_BGATE)],
        out_specs=hs(LANE, 0),
        out_shape=jax.ShapeDtypeStruct((B_HEADS, batch, seq, LANE), BF16),
        scratch_shapes=[pltpu.VMEM((n_tiles, ACC_ROWS, TK), BF16)]
        + _state_scratch(n_tiles) + PAIRS_PER_TRIP * _pair_scratch(),
        compiler_params=_cparams(1),
        name="mla_attn",
    )(tab, q4, k4, v4, of4)


def _diff_attn_kernel(tab_ref, q_ref, k_ref, v_ref, g_ref, bvec_ref, lq1_ref, lk1_ref, lq2_ref, lk2_ref, sg_ref,
                      o_ref, vt_scr, kext_scr, qext_scr, bias_scr, m1_all, acc1_all, m2_all, acc2_all, *pair_scr,
                      lam_init):
    flat = _pair_bufs(pair_scr)
    bufs = [(flat[2 * i], flat[2 * i + 1]) for i in range(PAIRS_PER_TRIP)]
    states = ((m1_all, acc1_all), (m2_all, acc2_all))
    _load_vt(v_ref, vt_scr)
    for m_all, acc_all in states:
        _init_pipeline(m_all, acc_all)
    for i in range(2):
        gen = jnp.broadcast_to(bvec_ref[i], (TK, 2 * TQ))
        bias_scr[i] = pltpu.roll(gen, 0, 1, stride=1, stride_axis=0)[:, :TQ]
    bias_scr[2] = jnp.zeros((TK, TQ), F32)
    row = lax.broadcasted_iota(jnp.int32, (TQ, LANE), 0)
    lane = lax.broadcasted_iota(jnp.int32, (TQ, LANE), 1)
    kext_scr[...] = _chunk_onehot(row >> CHUNK_SHIFT, lane, 0).astype(BF16)
    qext_scr[0] = jnp.zeros((TQ, LANE), BF16)
    qext_scr[1] = _chunk_maskq(row >> CHUNK_SHIFT, lane, 0).astype(BF16)
    lam = (jnp.exp(jnp.sum(lq1_ref[...] * lk1_ref[...], axis=-1, keepdims=True))
           - jnp.exp(jnp.sum(lq2_ref[...] * lk2_ref[...], axis=-1, keepdims=True)) + lam_init)

    def qk_stage(t, bufs2):
        q = _tile_rows(q_ref, tab_ref[PAIR_Q, t])
        qext = qext_scr[tab_ref[PAIR_DIAG, t]]
        zero = jnp.zeros_like(q)
        k_cat = jnp.concatenate([_tile_rows(k_ref, tab_ref[PAIR_K, t]), kext_scr[...]], axis=1)
        bias = bias_scr[tab_ref[PAIR_BIAS, t]]
        for bufs, q_map in zip(bufs2, (jnp.where(lane < C_QK, q, zero), jnp.where(lane >= C_QK, q, zero))):
            _qk_store(_nt_dot(k_cat, jnp.concatenate([q_map, qext], axis=1)) + bias, bufs)

    def smpv_stage(t, bufs2):
        for (m_all, acc_all), bufs in zip(states, bufs2):
            _softmax_pv_stage(tab_ref, t, bufs, m_all, vt_scr, acc_all)

    _run_pipeline(tab_ref, bufs, qk_stage, smpv_stage)
    for qi in range(m1_all.shape[0] - 1):
        rows = slice(qi * TQ, (qi + 1) * TQ)
        o = _normalized(acc1_all, qi) - lam * _normalized(acc2_all, qi)
        o = o * lax.rsqrt(jnp.mean(o * o, axis=0, keepdims=True) + EPS)
        out = o.T * sg_ref[...] * (1.0 - lam_init)
        o_ref[rows, :] = (out * _silu(g_ref[rows, :])).astype(BF16)


def _diff_attn(tab, ob4, of4, bias, lq1, lk1, lq2, lk2, sg, lam_init, batch, seq):
    hs = functools.partial(_head_spec, n_heads=C_HEADS, seq=seq)
    vec = lambda n: pl.BlockSpec((1, n), lambda g: (0, 0))
    n_tiles = seq // TQ
    return pl.pallas_call(
        functools.partial(_diff_attn_kernel, lam_init=lam_init),
        grid=(batch * C_HEADS,),
        in_specs=[_SMEM_SPEC, hs(LANE, SLAB_CQ), hs(LANE, SLAB_CK), hs(LANE, SLAB_CV), hs(LANE, FSLAB_CGATE),
                  pl.BlockSpec((None, 2, 1, 2 * TQ), lambda g: (g % C_HEADS, 0, 0, 0)),
                  vec(C_QK), vec(C_QK), vec(C_QK), vec(C_QK), vec(C_V)],
        out_specs=hs(LANE, 0),
        out_shape=jax.ShapeDtypeStruct((C_HEADS, batch, seq, LANE), BF16),
        scratch_shapes=[pltpu.VMEM((n_tiles, ACC_ROWS, TK), BF16), pltpu.VMEM((TK, LANE), BF16),
                        pltpu.VMEM((2, TQ, LANE), BF16), pltpu.VMEM((3, TK, TQ), F32)]
        + 2 * _state_scratch(n_tiles) + 2 * PAIRS_PER_TRIP * _pair_scratch(),
        compiler_params=_cparams(1),
        name="diff_attn",
    )(tab, ob4, ob4, ob4, of4, bias, lq1, lk1, lq2, lk2, sg)


def _outproj_kernel(a_ref, b_ref, c_ref, w_ref, x_ref, g_ref, o_ref, *, final_norm):
    mixed = jnp.concatenate([a_ref[h] for h in range(A_HEADS)] + [b_ref[h] for h in range(B_HEADS)]
                            + [c_ref[h] for h in range(C_HEADS)], axis=1)
    y = x_ref[...] + jnp.dot(mixed, w_ref[...], preferred_element_type=F32)
    if final_norm:
        y = _rms(y, g_ref[...])
    o_ref[...] = y


def _outproj(mix_a, mix_b, mix_c, w, x2, g, layer, final_norm):
    m = x2.shape[0]
    tm = TM_PROJ
    return pl.pallas_call(
        functools.partial(_outproj_kernel, final_norm=final_norm),
        grid=(m // tm,),
        in_specs=[
            pl.BlockSpec((A_HEADS, tm, LANE), lambda i: (0, i, 0)),
            pl.BlockSpec((B_HEADS, tm, LANE), lambda i: (0, i, 0)),
            pl.BlockSpec((C_HEADS, tm, LANE), lambda i: (0, i, 0)),
            pl.BlockSpec((None, D_MIX, D_MODEL), lambda i: (layer, 0, 0)),
            pl.BlockSpec((tm, D_MODEL), lambda i: (i, 0)),
            pl.BlockSpec((1, D_MODEL), lambda i: (0, 0)),
        ],
        out_specs=pl.BlockSpec((tm, D_MODEL), lambda i: (i, 0)),
        out_shape=jax.ShapeDtypeStruct((m, D_MODEL), F32),
        compiler_params=_cparams(1),
        name="outproj",
    )(mix_a, mix_b, mix_c, w, x2, g)


def kernel(x, norm_g, w_in, b_forget, mla_q_norm_g, w_uq, mla_kv_norm_g, w_ukv, lambda_q1, lambda_k1,
           lambda_q2, lambda_k2, diff_subln_g, rel_bias, w_out, final_norm_g):
    batch, seq, d = x.shape
    assert d == D_MODEL and seq % TQ == 0 and seq % TM_PROJ == 0 and (batch * seq) % TM_IN == 0
    m = batch * seq
    x2 = x.reshape(m, d)
    cs = _proj_col_scale()
    cos, sin = _rope_tables(seq)
    bias_vecs = _t5_bias_vectors(rel_bias)
    tab = jnp.asarray(_pair_table(seq // TQ))
    w_in_r = _prep_w_in(w_in)
    w_out_b = w_out.astype(BF16)
    for l in range(N_LAYERS):
        ob, of = _inproj(x2, norm_g[l].reshape(1, d), w_in_r, cs, l)
        ob4 = ob.reshape(N_BF_SLABS, batch, seq, LANE)
        of4 = of.reshape(N_F_SLABS, batch, seq, LANE)
        kb = _fox_bias(of, b_forget[l], batch, seq)
        mix_a = _fox_attn(tab, ob4, kb, of4, batch, seq)
        qb, kbm, vb = _mla_prep(of, mla_q_norm_g[l].reshape(1, -1), mla_kv_norm_g[l].reshape(1, -1),
                                _prep_w_uq(w_uq[l]), _prep_w_ukv(w_ukv[l]), cos, sin, seq)
        r4 = lambda a: a.reshape(a.shape[0], batch, seq, a.shape[-1])
        mix_b = _mla_attn(tab, r4(qb), r4(kbm), r4(vb), of4, batch, seq)
        lam_init = 0.8 - 0.6 * math.exp(-0.3 * l)
        mix_c = _diff_attn(tab, ob4, of4, bias_vecs, lambda_q1[l].reshape(1, -1), lambda_k1[l].reshape(1, -1),
                           lambda_q2[l].reshape(1, -1), lambda_k2[l].reshape(1, -1),
                           diff_subln_g[l].reshape(1, -1), lam_init, batch, seq)
        x2 = _outproj(mix_a.reshape(A_HEADS, m, LANE), mix_b.reshape(B_HEADS, m, LANE),
                      mix_c.reshape(C_HEADS, m, LANE), w_out_b, x2,
                      final_norm_g.reshape(1, d), l, final_norm=(l == N_LAYERS - 1))
    return x2.reshape(batch, seq, d)
```

```python
import functools
import math

import numpy as np
import jax
import jax.numpy as jnp
from jax import lax
from jax.experimental import pallas as pl
from jax.experimental.pallas import tpu as pltpu

F32 = jnp.float32
BF16 = jnp.bfloat16

D_MODEL = 2048
N_LAYERS = 2
CHUNK = 64
HEAD_DIM = 128
EPS = 1e-6
A_HEADS = 6
A_DIM = A_HEADS * HEAD_DIM
B_HEADS = 6
B_Q_LORA = 512
B_KV_LORA = 256
B_NOPE = 128
B_ROPE = 64
B_V = 128
B_DIM = B_HEADS * B_V
ROPE_THETA = 10000.0
C_HEADS = 4
C_QK = 64
C_V = 2 * C_QK
C_DIM = C_HEADS * C_V
REL_BUCKETS = 32
REL_MAX_DIST = 128
D_MIX = A_DIM + B_DIM + C_DIM

LANE = 128
V7X_VMEM_BYTES = 64 * 1024 * 1024
VMEM_LIMIT = 56 * 1024 * 1024

LOG2E = math.log2(math.e)
NEG_BIG = -1e30

TN = 768
SLABS_PER_TILE = TN // LANE
N_BF_SLABS = 30
N_F_SLABS = 24
N_BF_TILES = N_BF_SLABS // SLABS_PER_TILE
N_F_TILES = N_F_SLABS // SLABS_PER_TILE
N_PROJ = (N_BF_SLABS + N_F_SLABS) * LANE
SLAB_AQ, SLAB_AK, SLAB_AV, SLAB_CQ, SLAB_CK, SLAB_CV = 0, 6, 12, 18, 22, 26
FSLAB_AGATE, FSLAB_BGATE, FSLAB_CGATE, FSLAB_CQ, FSLAB_CKV, FSLAB_KROPE = 0, 6, 12, 16, 20, 22
AF_LANE = 64

TM_PROJ = 512
TM_IN = 1024
TQ = 512
TK = 512

CHUNK_SHIFT = CHUNK.bit_length() - 1
N_TILE_CHUNKS = TK // CHUNK
FOX_TERM_STRIDE = A_HEADS
FOX_TERMS = 3


def _chunk_onehot(row_chunk, lane, lane0):
    return jnp.where(lane - lane0 == row_chunk, 1.0, 0.0)


def _chunk_maskq(row_chunk, lane, lane0):
    c = lane - lane0
    return jnp.where((c >= 0) & (c < N_TILE_CHUNKS) & (row_chunk < c), NEG_BIG, 0.0)


def _nt_dot(a, b):
    return lax.dot_general(a, b, (((1,), (1,)), ((), ())), preferred_element_type=F32)


def _cparams(n_grid):
    return pltpu.CompilerParams(dimension_semantics=("arbitrary",) * n_grid,
                                vmem_limit_bytes=VMEM_LIMIT)


def _w_in_slab_sources():
    o = 0
    seg = {}
    for name, size in (("a_q", A_DIM), ("a_k", A_DIM), ("a_v", A_DIM), ("a_f", A_HEADS), ("a_gate", A_DIM),
                       ("b_cq", B_Q_LORA), ("b_ckv", B_KV_LORA), ("b_krope", B_ROPE), ("b_gate", B_DIM),
                       ("c_q", C_DIM), ("c_k", C_DIM), ("c_v", C_DIM), ("c_gate", C_DIM)):
        seg[name] = (o, size)
        o += size
    slabs = []
    for name in ("a_q", "a_k", "a_v", "c_q", "c_k", "c_v", "a_gate", "b_gate", "c_gate", "b_cq", "b_ckv"):
        start, size = seg[name]
        slabs += [[(start + i, start + i + LANE, 1)] for i in range(0, size, LANE)]
    kr, half = seg["b_krope"][0], B_ROPE // 2
    af = seg["a_f"][0]
    slabs.append([(kr, kr + B_ROPE, 1), (None, LANE - B_ROPE, 0)])
    slabs.append([(kr + half, kr + B_ROPE, -1), (kr, kr + half, 1), (af, af + A_HEADS, 1),
                  (None, LANE - B_ROPE - A_HEADS, 0)])
    assert len(slabs) * LANE == N_PROJ
    return slabs


def _prep_w_in(w):
    wt = jnp.transpose(w.astype(BF16), (2, 0, 1))
    rows = []
    for pieces in _w_in_slab_sources():
        for start, stop, sign in pieces:
            if start is None:
                rows.append(jnp.zeros((stop,) + wt.shape[1:], wt.dtype))
            else:
                rows.append(wt[start:stop] if sign > 0 else -wt[start:stop])
    return jnp.transpose(jnp.concatenate(rows, axis=0), (1, 0, 2))


def _proj_col_scale():
    s = jnp.ones((N_PROJ,), F32)
    s = s.at[SLAB_AQ * LANE:(SLAB_AQ + A_HEADS) * LANE].set(LOG2E / math.sqrt(HEAD_DIM))
    s = s.at[SLAB_CQ * LANE:(SLAB_CQ + C_HEADS) * LANE].set(LOG2E / math.sqrt(C_QK))
    return s.reshape(1, N_PROJ)


def _prep_w_uq(w):
    w = w.reshape(B_Q_LORA, B_HEADS, B_NOPE + B_ROPE)
    nope = w[:, :, :B_NOPE].reshape(B_Q_LORA, B_HEADS * B_NOPE)
    rope = w[:, :, B_NOPE:]
    half = B_ROPE // 2
    rot = jnp.concatenate([-rope[:, :, half:], rope[:, :, :half]], axis=2)
    pad = jnp.zeros((B_Q_LORA, B_HEADS, LANE - B_ROPE), w.dtype)
    rope_p = jnp.concatenate([rope, pad], axis=2).reshape(B_Q_LORA, B_HEADS * LANE)
    rot_p = jnp.concatenate([rot, pad], axis=2).reshape(B_Q_LORA, B_HEADS * LANE)
    return jnp.concatenate([nope, rope_p, rot_p], axis=1).astype(BF16)


def _prep_w_ukv(w):
    w = w.reshape(B_KV_LORA, B_HEADS, B_NOPE + B_V)
    k = w[:, :, :B_NOPE].reshape(B_KV_LORA, B_HEADS * B_NOPE)
    v = w[:, :, B_NOPE:].reshape(B_KV_LORA, B_HEADS * B_V)
    return jnp.concatenate([k, v], axis=1).astype(BF16)


def _rope_tables(seq):
    half = B_ROPE // 2
    inv = ROPE_THETA ** (-jnp.arange(half, dtype=F32) / half)
    ang = jnp.arange(seq).astype(F32)[:, None] * inv[None, :]
    pad = jnp.zeros((seq, LANE - B_ROPE), F32)
    cos = jnp.concatenate([jnp.cos(ang), jnp.cos(ang), pad], axis=1)
    sin = jnp.concatenate([jnp.sin(ang), jnp.sin(ang), pad], axis=1)
    return cos, sin


def _t5_bucket(rel):
    nb = REL_BUCKETS // 2
    max_exact = nb // 2
    ret = (rel > 0).astype(jnp.int32) * nb
    n = jnp.abs(rel)
    nf = jnp.maximum(n, 1).astype(F32)
    large = max_exact + (jnp.log(nf / max_exact) / math.log(REL_MAX_DIST / max_exact)
                         * (nb - max_exact)).astype(jnp.int32)
    large = jnp.minimum(large, nb - 1)
    return ret + jnp.where(n < max_exact, n, large)


def _t5_bias_vectors(rel_bias):
    assert TQ == TK and TQ > REL_MAX_DIST
    x = jnp.arange(2 * TQ)
    x = jnp.where(x < TQ, x, x - 2 * TQ)
    bucket = jnp.stack([_t5_bucket(-x - delta) for delta in (0, TQ)])
    rel = rel_bias - rel_bias[REL_BUCKETS // 2 - 1][None, :]
    t = jnp.zeros((C_HEADS,) + bucket.shape, F32)
    for b in range(REL_BUCKETS):
        t = jnp.where(bucket[None] == b, rel[b][:, None, None], t)
    return (t * LOG2E).reshape(C_HEADS, 2, 1, 2 * TQ)


def _inproj_kernel(x_ref, g_ref, w_ref, cs_ref, ob_ref, of_ref, h_scr):
    j = pl.program_id(1)

    @pl.when(j == 0)
    def _():
        x = x_ref[...]
        ms = jnp.mean(x * x, axis=-1, keepdims=True)
        h_scr[...] = (x * lax.rsqrt(ms + EPS) * g_ref[...]).astype(BF16)

    def project(out_ref):
        acc = _nt_dot(h_scr[...], w_ref[...]) * cs_ref[...]
        for s in range(SLABS_PER_TILE):
            out_ref[s] = acc[:, s * LANE:(s + 1) * LANE].astype(out_ref.dtype)

    @pl.when(j < N_BF_TILES)
    def _():
        project(ob_ref)

    @pl.when(j >= N_BF_TILES)
    def _():
        project(of_ref)


def _inproj(x2, g, w, cs, layer):
    m = x2.shape[0]
    tm = TM_IN
    return pl.pallas_call(
        _inproj_kernel,
        grid=(m // tm, N_BF_TILES + N_F_TILES),
        in_specs=[
            pl.BlockSpec((tm, D_MODEL), lambda i, j: (i, 0)),
            pl.BlockSpec((1, D_MODEL), lambda i, j: (0, 0)),
            pl.BlockSpec((None, TN, D_MODEL), lambda i, j: (layer, j, 0)),
            pl.BlockSpec((1, TN), lambda i, j: (0, j)),
        ],
        out_specs=[
            pl.BlockSpec((SLABS_PER_TILE, tm, LANE), lambda i, j: (jnp.minimum(j, N_BF_TILES - 1), i, 0)),
            pl.BlockSpec((SLABS_PER_TILE, tm, LANE), lambda i, j: (jnp.maximum(j - N_BF_TILES, 0), i, 0)),
        ],
        out_shape=[jax.ShapeDtypeStruct((N_BF_SLABS, m, LANE), BF16),
                   jax.ShapeDtypeStruct((N_F_SLABS, m, LANE), F32)],
        scratch_shapes=[pltpu.VMEM((tm, D_MODEL), BF16)],
        compiler_params=_cparams(2),
        name="inproj",
    )(x2, g, w, cs)


def _rms(x, g):
    return x * lax.rsqrt(jnp.mean(x * x, axis=-1, keepdims=True) + EPS) * g


def _mla_prep_kernel(cq_ref, ckv_ref, kr_ref, gq_ref, gkv_ref, wq_ref, wkv_ref, cos_ref, sin_ref,
                     qo_ref, ko_ref, vo_ref, *, n_pos):
    scale = LOG2E / math.sqrt(B_NOPE + B_ROPE)
    cos = cos_ref[...]
    sin = sin_ref[...]
    cq = jnp.concatenate([cq_ref[s] for s in range(B_Q_LORA // LANE)], axis=1)
    q = jnp.dot(_rms(cq, gq_ref[...]).astype(BF16), wq_ref[...], preferred_element_type=F32)
    ckv = jnp.concatenate([ckv_ref[s] for s in range(B_KV_LORA // LANE)], axis=1)
    kv = jnp.dot(_rms(ckv, gkv_ref[...]).astype(BF16), wkv_ref[...], preferred_element_type=F32)
    tm = cos.shape[0]
    pos = (pl.program_id(0) % n_pos) * tm + lax.broadcasted_iota(jnp.int32, (tm, LANE), 0)
    chunk = (pos % TK) >> CHUNK_SHIFT
    lane = lax.broadcasted_iota(jnp.int32, (tm, LANE), 1)
    k_rope = (kr_ref[0] * cos + kr_ref[1] * sin + _chunk_onehot(chunk, lane, B_ROPE)).astype(BF16)
    mask_q = _chunk_maskq(chunk, lane, B_ROPE)
    n_h = B_HEADS * LANE
    for h in range(B_HEADS):
        sl = slice(h * LANE, (h + 1) * LANE)
        q_rope = q[:, n_h + h * LANE:n_h + (h + 1) * LANE] * cos + q[:, 2 * n_h + h * LANE:2 * n_h + (h + 1) * LANE] * sin
        qo_ref[h, :, :LANE] = (q[:, sl] * scale).astype(BF16)
        qo_ref[h, :, LANE:] = (q_rope * scale + mask_q).astype(BF16)
        ko_ref[h, :, :LANE] = kv[:, sl].astype(BF16)
        ko_ref[h, :, LANE:] = k_rope
        vo_ref[h] = kv[:, n_h + h * LANE:n_h + (h + 1) * LANE].astype(BF16)


def _mla_prep(of, gq, gkv, wq, wkv, cos, sin, seq):
    m = of.shape[1]
    tm = TM_PROJ
    n_pos = seq // tm
    nq = B_Q_LORA // LANE
    nkv = B_KV_LORA // LANE
    return pl.pallas_call(
        functools.partial(_mla_prep_kernel, n_pos=n_pos),
        grid=(m // tm,),
        in_specs=[
            pl.BlockSpec((nq, tm, LANE), lambda i: (FSLAB_CQ // nq, i, 0)),
            pl.BlockSpec((nkv, tm, LANE), lambda i: (FSLAB_CKV // nkv, i, 0)),
            pl.BlockSpec((2, tm, LANE), lambda i: (FSLAB_KROPE // 2, i, 0)),
            pl.BlockSpec((1, B_Q_LORA), lambda i: (0, 0)),
            pl.BlockSpec((1, B_KV_LORA), lambda i: (0, 0)),
            pl.BlockSpec(wq.shape, lambda i: (0, 0)),
            pl.BlockSpec(wkv.shape, lambda i: (0, 0)),
            pl.BlockSpec((tm, LANE), lambda i: (i % n_pos, 0)),
            pl.BlockSpec((tm, LANE), lambda i: (i % n_pos, 0)),
        ],
        out_specs=[
            pl.BlockSpec((B_HEADS, tm, 2 * LANE), lambda i: (0, i, 0)),
            pl.BlockSpec((B_HEADS, tm, 2 * LANE), lambda i: (0, i, 0)),
            pl.BlockSpec((B_HEADS, tm, LANE), lambda i: (0, i, 0)),
        ],
        out_shape=[jax.ShapeDtypeStruct((B_HEADS, m, 2 * LANE), BF16),
                   jax.ShapeDtypeStruct((B_HEADS, m, 2 * LANE), BF16),
                   jax.ShapeDtypeStruct((B_HEADS, m, LANE), BF16)],
        compiler_params=_cparams(1),
        name="mla_prep",
    )(of, of, of, gq, gkv, wq, wkv, cos, sin)


CUM_CHUNK = 128


def _fox_bias_kernel(af_ref, bf_ref, kb_ref, cum_scr):
    seq = af_ref.shape[1]
    r = lax.broadcasted_iota(jnp.int32, (CUM_CHUNK, CUM_CHUNK), 0)
    c = lax.broadcasted_iota(jnp.int32, (CUM_CHUNK, CUM_CHUNK), 1)
    tri = (r >= c).astype(F32)
    bvec = bf_ref[...]

    def body(t, carry):
        r0 = pl.multiple_of(t * CUM_CHUNK, CUM_CHUNK)
        x = af_ref[0, pl.ds(r0, CUM_CHUNK), :] + bvec
        logf = jnp.minimum(x, 0.0) - jnp.log1p(jnp.exp(-jnp.abs(x)))
        cs = jnp.dot(tri, logf, preferred_element_type=F32, precision=lax.Precision.HIGHEST) + carry
        cum_scr[pl.ds(r0, CUM_CHUNK), :] = cs
        return cs[CUM_CHUNK - 1:CUM_CHUNK, :]

    lax.fori_loop(0, seq // CUM_CHUNK, body, jnp.zeros((1, LANE), F32))
    lane = lax.broadcasted_iota(jnp.int32, (seq, LANE), 1)
    chunk = (lax.broadcasted_iota(jnp.int32, (seq, LANE), 0) % TK) >> CHUNK_SHIFT
    out = _chunk_onehot(chunk, lane, 0)
    rest = cum_scr[...] * (-LOG2E)
    for i in range(FOX_TERMS):
        term = rest.astype(BF16).astype(F32)
        rest = rest - term
        lo = AF_LANE + FOX_TERM_STRIDE * i
        moved = term if i == 0 else pltpu.roll(term, FOX_TERM_STRIDE * i, 1)
        out = jnp.where((lane >= lo) & (lane < lo + A_HEADS), moved, out)
    kb_ref[0] = out.astype(BF16)


def _fox_bias(of, b_forget, batch, seq):
    bvec = jnp.zeros((1, LANE), F32).at[0, AF_LANE:AF_LANE + A_HEADS].set(b_forget)
    of4 = of.reshape(N_F_SLABS, batch, seq, LANE)
    return pl.pallas_call(
        _fox_bias_kernel,
        grid=(batch,),
        in_specs=[
            pl.BlockSpec((None, 1, seq, LANE), lambda b: (N_F_SLABS - 1, b, 0, 0)),
            pl.BlockSpec((1, LANE), lambda b: (0, 0)),
        ],
        out_specs=pl.BlockSpec((1, seq, LANE), lambda b: (b, 0, 0)),
        out_shape=jax.ShapeDtypeStruct((batch, seq, LANE), BF16),
        scratch_shapes=[pltpu.VMEM((seq, LANE), F32)],
        compiler_params=_cparams(1),
        name="fox_bias",
    )(of4, bvec)


PAIR_Q, PAIR_K, PAIR_DIAG, PAIR_BIAS, PAIR_STATE = 0, 1, 2, 3, 4
ACC_ROWS = HEAD_DIM + 16
PAIRS_PER_TRIP = 4
PIPE_DIST = 2


def _pair_table(n_tiles):
    pad = (0, 0, 1, 0, n_tiles)
    rows = []
    for qi in range(n_tiles):
        for kj in range(qi + 1):
            rows.append((qi, kj, int(kj == qi), 0 if kj == qi else (1 if kj == qi - 1 else 2), qi))
    rows += [pad] * (-len(rows) % PAIRS_PER_TRIP)
    rows = rows + [pad] * PIPE_DIST
    return np.asarray(rows, np.int32).T.copy()


def _tile_rows(ref, tile):
    return ref[pl.ds(pl.multiple_of(tile * TK, TK), TK), :]


def _load_vt(v_ref, vt_scr):
    extra = jnp.where(lax.broadcasted_iota(jnp.int32, (ACC_ROWS - HEAD_DIM, TK), 0) == 0, 1.0, 0.0).astype(BF16)
    for j in range(vt_scr.shape[0]):
        vt_scr[j, :HEAD_DIM, :] = v_ref[j * TK:(j + 1) * TK, :].T
        vt_scr[j, HEAD_DIM:, :] = extra


class _PairBufs:
    def __init__(self, s, mb):
        self.s, self.mb = s, mb


def _pair_scratch():
    return [pltpu.VMEM((TK, TQ), F32), pltpu.VMEM((1, TQ), F32)]


def _state_scratch(n_tiles):
    return [pltpu.VMEM((n_tiles + 1, 1, TQ), F32), pltpu.VMEM((n_tiles + 1, ACC_ROWS, TQ), F32)]


def _init_pipeline(m_all, acc_all):
    m_all[...] = jnp.full(m_all.shape, NEG_BIG, F32)
    acc_all[...] = jnp.zeros(acc_all.shape, F32)


def _pair_bufs(pair_scr):
    return [_PairBufs(*pair_scr[2 * i:2 * i + 2]) for i in range(len(pair_scr) // 2)]


def _qk_store(s, bufs):
    bufs.s[...] = s
    bufs.mb[...] = jnp.max(s, axis=0, keepdims=True)


def _softmax_pv_stage(tab_ref, t, bufs, m_all, vt_scr, acc_all):
    st = tab_ref[PAIR_STATE, t]
    m_prev = m_all[st]
    m_new = jnp.maximum(m_prev, bufs.mb[...])
    alpha = jnp.exp2(m_prev - m_new)
    p = jnp.exp2(bufs.s[...] - m_new).astype(BF16)
    m_all[st] = m_new
    acc_all[st] = alpha * acc_all[st] + jnp.dot(vt_scr[tab_ref[PAIR_K, t]], p, preferred_element_type=F32)


def _run_pipeline(tab_ref, bufs, qk_stage, smpv_stage):
    n_pairs = tab_ref.shape[1] - PIPE_DIST
    u, d = PAIRS_PER_TRIP, PIPE_DIST
    assert u > d

    def body(i, carry):
        for j in range(u):
            c = u * i + j
            smpv_stage(c, bufs[j % u])
            qk_stage(c + d, bufs[(d + j) % u])
        return carry

    for c in range(d):
        qk_stage(c, bufs[c % u])
    lax.fori_loop(0, n_pairs // u, body, 0)


def _normalized(acc_all, qi):
    return acc_all[qi, :HEAD_DIM, :] * (1.0 / acc_all[qi, HEAD_DIM:HEAD_DIM + 1, :])


def _silu(g):
    return g / (1.0 + jnp.exp(-g))


def _head_spec(width, slab0, n_heads, seq):
    return pl.BlockSpec((None, None, seq, width), lambda g: (slab0 + g % n_heads, g // n_heads, 0, 0))


_SMEM_SPEC = pl.BlockSpec(memory_space=pltpu.SMEM)


def _fox_attn_kernel(tab_ref, q_ref, k_ref, kb_ref, v_ref, g_ref, o_ref, vt_scr, qext_scr, m_all, acc_all, *pair_scr):
    bufs = _pair_bufs(pair_scr)
    _load_vt(v_ref, vt_scr)
    _init_pipeline(m_all, acc_all)
    row = lax.broadcasted_iota(jnp.int32, (TQ, LANE), 0)
    lane = lax.broadcasted_iota(jnp.int32, (TQ, LANE), 1)
    rel = lane - (AF_LANE + pl.program_id(0) % A_HEADS)
    ones3 = jnp.where((rel >= 0) & (rel < FOX_TERM_STRIDE * FOX_TERMS) & (rel % FOX_TERM_STRIDE == 0), 1.0, 0.0)
    qext_scr[0] = ones3.astype(BF16)
    qext_scr[1] = (ones3 + _chunk_maskq(row >> CHUNK_SHIFT, lane, 0)).astype(BF16)
    sub_key = lax.broadcasted_iota(jnp.int32, (LANE, LANE), 0)
    sub_qry = lax.broadcasted_iota(jnp.int32, (LANE, LANE), 1)

    def qk_stage(t, bufs):
        q_cat = jnp.concatenate([_tile_rows(q_ref, tab_ref[PAIR_Q, t]), qext_scr[tab_ref[PAIR_DIAG, t]]], axis=1)
        kj = tab_ref[PAIR_K, t]
        k_cat = jnp.concatenate([_tile_rows(k_ref, kj), _tile_rows(kb_ref, kj)], axis=1)
        s = _nt_dot(k_cat, q_cat)
        masked = sub_key - sub_qry > jnp.where(tab_ref[PAIR_DIAG, t] > 0, 0, TK)
        blocks = []
        for r in range(TK // LANE):
            blk = s[r * LANE:(r + 1) * LANE, :]
            mid = jnp.where(masked, NEG_BIG, blk[:, r * LANE:(r + 1) * LANE])
            parts = ([blk[:, :r * LANE]] if r > 0 else []) + [mid]
            parts += [blk[:, (r + 1) * LANE:]] if (r + 1) * LANE < TQ else []
            blocks.append(jnp.concatenate(parts, axis=1))
        _qk_store(jnp.concatenate(blocks, axis=0), bufs)

    _run_pipeline(tab_ref, bufs, qk_stage,
                  lambda t, bufs: _softmax_pv_stage(tab_ref, t, bufs, m_all, vt_scr, acc_all))
    for qi in range(m_all.shape[0] - 1):
        rows = slice(qi * TQ, (qi + 1) * TQ)
        o_ref[rows, :] = (_normalized(acc_all, qi).T * _silu(g_ref[rows, :])).astype(BF16)


def _fox_attn(tab, ob4, kb, of4, batch, seq):
    hs = functools.partial(_head_spec, n_heads=A_HEADS, seq=seq)
    n_tiles = seq // TQ
    return pl.pallas_call(
        _fox_attn_kernel,
        grid=(batch * A_HEADS,),
        in_specs=[_SMEM_SPEC, hs(LANE, SLAB_AQ), hs(LANE, SLAB_AK),
                  pl.BlockSpec((None, seq, LANE), lambda g: (g // A_HEADS, 0, 0)), hs(LANE, SLAB_AV),
                  hs(LANE, FSLAB_AGATE)],
        out_specs=hs(LANE, 0),
        out_shape=jax.ShapeDtypeStruct((A_HEADS, batch, seq, LANE), BF16),
        scratch_shapes=[pltpu.VMEM((n_tiles, ACC_ROWS, TK), BF16), pltpu.VMEM((2, TQ, LANE), BF16)]
        + _state_scratch(n_tiles) + PAIRS_PER_TRIP * _pair_scratch(),
        compiler_params=_cparams(1),
        name="fox_attn",
    )(tab, ob4, ob4, kb, ob4, of4)


def _mla_attn_kernel(tab_ref, q_ref, k_ref, v_ref, g_ref, o_ref, vt_scr, m_all, acc_all, *pair_scr):
    bufs = _pair_bufs(pair_scr)
    _load_vt(v_ref, vt_scr)
    _init_pipeline(m_all, acc_all)
    lane = lax.broadcasted_iota(jnp.int32, (TQ, LANE), 1)

    def qk_stage(t, bufs):
        q = _tile_rows(q_ref, tab_ref[PAIR_Q, t])
        first_off = jnp.where(tab_ref[PAIR_DIAG, t] > 0, LANE, B_ROPE)
        q_hi = jnp.where(lane >= first_off, jnp.zeros((), BF16), q[:, LANE:])
        q_cat = jnp.concatenate([q[:, :LANE], q_hi], axis=1)
        _qk_store(_nt_dot(_tile_rows(k_ref, tab_ref[PAIR_K, t]), q_cat), bufs)

    _run_pipeline(tab_ref, bufs, qk_stage,
                  lambda t, bufs: _softmax_pv_stage(tab_ref, t, bufs, m_all, vt_scr, acc_all))
    for qi in range(m_all.shape[0] - 1):
        rows = slice(qi * TQ, (qi + 1) * TQ)
        o_ref[rows, :] = (_normalized(acc_all, qi).T * _silu(g_ref[rows, :])).astype(BF16)


def _mla_attn(tab, q4, k4, v4, of4, batch, seq):
    hs = functools.partial(_head_spec, n_heads=B_HEADS, seq=seq)
    n_tiles = seq // TQ
    return pl.pallas_call(
        _mla_attn_kernel,
        grid=(batch * B_HEADS,),
        in_specs=[_SMEM_SPEC, hs(2 * LANE, 0), hs(2 * LANE, 0), hs(LANE, 0), hs(LANE, FSLAB_BGATE)],
        out_specs=hs(LANE, 0),
        out_shape=jax.ShapeDtypeStruct((B_HEADS, batch, seq, LANE), BF16),
        scratch_shapes=[pltpu.VMEM((n_tiles, ACC_ROWS, TK), BF16)]
        + _state_scratch(n_tiles) + PAIRS_PER_TRIP * _pair_scratch(),
        compiler_params=_cparams(1),
        name="mla_attn",
    )(tab, q4, k4, v4, of4)


def _diff_attn_kernel(tab_ref, q_ref, k_ref, v_ref, g_ref, bvec_ref, lq1_ref, lk1_ref, lq2_ref, lk2_ref, sg_ref,
                      o_ref, vt_scr, kext_scr, qext_scr, bias_scr, m1_all, acc1_all, m2_all, acc2_all, *pair_scr,
                      lam_init):
    flat = _pair_bufs(pair_scr)
    bufs = [(flat[2 * i], flat[2 * i + 1]) for i in range(PAIRS_PER_TRIP)]
    states = ((m1_all, acc1_all), (m2_all, acc2_all))
    _load_vt(v_ref, vt_scr)
    for m_all, acc_all in states:
        _init_pipeline(m_all, acc_all)
    for i in range(2):
        gen = jnp.broadcast_to(bvec_ref[i], (TK, 2 * TQ))
        bias_scr[i] = pltpu.roll(gen, 0, 1, stride=1, stride_axis=0)[:, :TQ]
    bias_scr[2] = jnp.zeros((TK, TQ), F32)
    row = lax.broadcasted_iota(jnp.int32, (TQ, LANE), 0)
    lane = lax.broadcasted_iota(jnp.int32, (TQ, LANE), 1)
    kext_scr[...] = _chunk_onehot(row >> CHUNK_SHIFT, lane, 0).astype(BF16)
    qext_scr[0] = jnp.zeros((TQ, LANE), BF16)
    qext_scr[1] = _chunk_maskq(row >> CHUNK_SHIFT, lane, 0).astype(BF16)
    lam = (jnp.exp(jnp.sum(lq1_ref[...] * lk1_ref[...], axis=-1, keepdims=True))
           - jnp.exp(jnp.sum(lq2_ref[...] * lk2_ref[...], axis=-1, keepdims=True)) + lam_init)

    def qk_stage(t, bufs2):
        q = _tile_rows(q_ref, tab_ref[PAIR_Q, t])
        qext = qext_scr[tab_ref[PAIR_DIAG, t]]
        zero = jnp.zeros_like(q)
        k_cat = jnp.concatenate([_tile_rows(k_ref, tab_ref[PAIR_K, t]), kext_scr[...]], axis=1)
        bias = bias_scr[tab_ref[PAIR_BIAS, t]]
        for bufs, q_map in zip(bufs2, (jnp.where(lane < C_QK, q, zero), jnp.where(lane >= C_QK, q, zero))):
            _qk_store(_nt_dot(k_cat, jnp.concatenate([q_map, qext], axis=1)) + bias, bufs)

    def smpv_stage(t, bufs2):
        for (m_all, acc_all), bufs in zip(states, bufs2):
            _softmax_pv_stage(tab_ref, t, bufs, m_all, vt_scr, acc_all)

    _run_pipeline(tab_ref, bufs, qk_stage, smpv_stage)
    for qi in range(m1_all.shape[0] - 1):
        rows = slice(qi * TQ, (qi + 1) * TQ)
        o = _normalized(acc1_all, qi) - lam * _normalized(acc2_all, qi)
        o = o * lax.rsqrt(jnp.mean(o * o, axis=0, keepdims=True) + EPS)
        out = o.T * sg_ref[...] * (1.0 - lam_init)
        o_ref[rows, :] = (out * _silu(g_ref[rows, :])).astype(BF16)


def _diff_attn(tab, ob4, of4, bias, lq1, lk1, lq2, lk2, sg, lam_init, batch, seq):
    hs = functools.partial(_head_spec, n_heads=C_HEADS, seq=seq)
    vec = lambda n: pl.BlockSpec((1, n), lambda g: (0, 0))
    n_tiles = seq // TQ
    return pl.pallas_call(
        functools.partial(_diff_attn_kernel, lam_init=lam_init),
        grid=(batch * C_HEADS,),
        in_specs=[_SMEM_SPEC, hs(LANE, SLAB_CQ), hs(LANE, SLAB_CK), hs(LANE, SLAB_CV), hs(LANE, FSLAB_CGATE),
                  pl.BlockSpec((None, 2, 1, 2 * TQ), lambda g: (g % C_HEADS, 0, 0, 0)),
                  vec(C_QK), vec(C_QK), vec(C_QK), vec(C_QK), vec(C_V)],
        out_specs=hs(LANE, 0),
        out_shape=jax.ShapeDtypeStruct((C_HEADS, batch, seq, LANE), BF16),
        scratch_shapes=[pltpu.VMEM((n_tiles, ACC_ROWS, TK), BF16), pltpu.VMEM((TK, LANE), BF16),
                        pltpu.VMEM((2, TQ, LANE), BF16), pltpu.VMEM((3, TK, TQ), F32)]
        + 2 * _state_scratch(n_tiles) + 2 * PAIRS_PER_TRIP * _pair_scratch(),
        compiler_params=_cparams(1),
        name="diff_attn",
    )(tab, ob4, ob4, ob4, of4, bias, lq1, lk1, lq2, lk2, sg)


def _outproj_kernel(a_ref, b_ref, c_ref, w_ref, x_ref, g_ref, o_ref, *, final_norm):
    mixed = jnp.concatenate([a_ref[h] for h in range(A_HEADS)] + [b_ref[h] for h in range(B_HEADS)]
                            + [c_ref[h] for h in range(C_HEADS)], axis=1)
    y = x_ref[...] + jnp.dot(mixed, w_ref[...], preferred_element_type=F32)
    if final_norm:
        y = _rms(y, g_ref[...])
    o_ref[...] = y


def _outproj(mix_a, mix_b, mix_c, w, x2, g, layer, final_norm):
    m = x2.shape[0]
    tm = TM_PROJ
    return pl.pallas_call(
        functools.partial(_outproj_kernel, final_norm=final_norm),
        grid=(m // tm,),
        in_specs=[
            pl.BlockSpec((A_HEADS, tm, LANE), lambda i: (0, i, 0)),
            pl.BlockSpec((B_HEADS, tm, LANE), lambda i: (0, i, 0)),
            pl.BlockSpec((C_HEADS, tm, LANE), lambda i: (0, i, 0)),
            pl.BlockSpec((None, D_MIX, D_MODEL), lambda i: (layer, 0, 0)),
            pl.BlockSpec((tm, D_MODEL), lambda i: (i, 0)),
            pl.BlockSpec((1, D_MODEL), lambda i: (0, 0)),
        ],
        out_specs=pl.BlockSpec((tm, D_MODEL), lambda i: (i, 0)),
        out_shape=jax.ShapeDtypeStruct((m, D_MODEL), F32),
        compiler_params=_cparams(1),
        name="outproj",
    )(mix_a, mix_b, mix_c, w, x2, g)


def kernel(x, norm_g, w_in, b_forget, mla_q_norm_g, w_uq, mla_kv_norm_g, w_ukv, lambda_q1, lambda_k1,
           lambda_q2, lambda_k2, diff_subln_g, rel_bias, w_out, final_norm_g):
    batch, seq, d = x.shape
    assert d == D_MODEL and seq % TQ == 0 and seq % TM_PROJ == 0 and (batch * seq) % TM_IN == 0
    m = batch * seq
    x2 = x.reshape(m, d)
    cs = _proj_col_scale()
    cos, sin = _rope_tables(seq)
    bias_vecs = _t5_bias_vectors(rel_bias)
    tab = jnp.asarray(_pair_table(seq // TQ))
    w_in_r = _prep_w_in(w_in)
    w_out_b = w_out.astype(BF16)
    for l in range(N_LAYERS):
        ob, of = _inproj(x2, norm_g[l].reshape(1, d), w_in_r, cs, l)
        ob4 = ob.reshape(N_BF_SLABS, batch, seq, LANE)
        of4 = of.reshape(N_F_SLABS, batch, seq, LANE)
        kb = _fox_bias(of, b_forget[l], batch, seq)
        mix_a = _fox_attn(tab, ob4, kb, of4, batch, seq)
        qb, kbm, vb = _mla_prep(of, mla_q_norm_g[l].reshape(1, -1), mla_kv_norm_g[l].reshape(1, -1),
                                _prep_w_uq(w_uq[l]), _prep_w_ukv(w_ukv[l]), cos, sin, seq)
        r4 = lambda a: a.reshape(a.shape[0], batch, seq, a.shape[-1])
        mix_b = _mla_attn(tab, r4(qb), r4(kbm), r4(vb), of4, batch, seq)
        lam_init = 0.8 - 0.6 * math.exp(-0.3 * l)
        mix_c = _diff_attn(tab, ob4, of4, bias_vecs, lambda_q1[l].reshape(1, -1), lambda_k1[l].reshape(1, -1),
                           lambda_q2[l].reshape(1, -1), lambda_k2[l].reshape(1, -1),
                           diff_subln_g[l].reshape(1, -1), lam_init, batch, seq)
        x2 = _outproj(mix_a.reshape(A_HEADS, m, LANE), mix_b.reshape(B_HEADS, m, LANE),
                      mix_c.reshape(C_HEADS, m, LANE), w_out_b, x2,
                      final_norm_g.reshape(1, d), l, final_norm=(l == N_LAYERS - 1))
    return x2.reshape(batch, seq, d)
```

```python
import functools
import math

import numpy as np
import jax
import jax.numpy as jnp
from jax import lax
from jax.experimental import pallas as pl
from jax.experimental.pallas import tpu as pltpu

F32 = jnp.float32
BF16 = jnp.bfloat16

D_MODEL = 2048
N_LAYERS = 2
CHUNK = 64
HEAD_DIM = 128
EPS = 1e-6
A_HEADS = 6
A_DIM = A_HEADS * HEAD_DIM
B_HEADS = 6
B_Q_LORA = 512
B_KV_LORA = 256
B_NOPE = 128
B_ROPE = 64
B_V = 128
B_DIM = B_HEADS * B_V
ROPE_THETA = 10000.0
C_HEADS = 4
C_QK = 64
C_V = 2 * C_QK
C_DIM = C_HEADS * C_V
REL_BUCKETS = 32
REL_MAX_DIST = 128
D_MIX = A_DIM + B_DIM + C_DIM

LANE = 128
V7X_VMEM_BYTES = 64 * 1024 * 1024
VMEM_LIMIT = 56 * 1024 * 1024

LOG2E = math.log2(math.e)
NEG_BIG = -1e30

TN = 768
SLABS_PER_TILE = TN // LANE
N_BF_SLABS = 30
N_F_SLABS = 24
N_BF_TILES = N_BF_SLABS // SLABS_PER_TILE
N_F_TILES = N_F_SLABS // SLABS_PER_TILE
N_PROJ = (N_BF_SLABS + N_F_SLABS) * LANE
SLAB_AQ, SLAB_AK, SLAB_AV, SLAB_CQ, SLAB_CK, SLAB_CV = 0, 6, 12, 18, 22, 26
FSLAB_AGATE, FSLAB_BGATE, FSLAB_CGATE, FSLAB_CQ, FSLAB_CKV, FSLAB_KROPE = 0, 6, 12, 16, 20, 22
AF_LANE = 64

TM_PROJ = 512
TM_IN = 1024
TQ = 512
TK = 512

CHUNK_SHIFT = CHUNK.bit_length() - 1
N_TILE_CHUNKS = TK // CHUNK
FOX_TERM_STRIDE = A_HEADS
FOX_TERMS = 3


def _chunk_onehot(row_chunk, lane, lane0):
    return jnp.where(lane - lane0 == row_chunk, 1.0, 0.0)


def _chunk_maskq(row_chunk, lane, lane0):
    c = lane - lane0
    return jnp.where((c >= 0) & (c < N_TILE_CHUNKS) & (row_chunk < c), NEG_BIG, 0.0)


def _nt_dot(a, b):
    return lax.dot_general(a, b, (((1,), (1,)), ((), ())), preferred_element_type=F32)


def _cparams(n_grid):
    return pltpu.CompilerParams(dimension_semantics=("arbitrary",) * n_grid,
                                vmem_limit_bytes=VMEM_LIMIT)


def _w_in_slab_sources():
    o = 0
    seg = {}
    for name, size in (("a_q", A_DIM), ("a_k", A_DIM), ("a_v", A_DIM), ("a_f", A_HEADS), ("a_gate", A_DIM),
                       ("b_cq", B_Q_LORA), ("b_ckv", B_KV_LORA), ("b_krope", B_ROPE), ("b_gate", B_DIM),
                       ("c_q", C_DIM), ("c_k", C_DIM), ("c_v", C_DIM), ("c_gate", C_DIM)):
        seg[name] = (o, size)
        o += size
    slabs = []
    for name in ("a_q", "a_k", "a_v", "c_q", "c_k", "c_v", "a_gate", "b_gate", "c_gate", "b_cq", "b_ckv"):
        start, size = seg[name]
        slabs += [[(start + i, start + i + LANE, 1)] for i in range(0, size, LANE)]
    kr, half = seg["b_krope"][0], B_ROPE // 2
    af = seg["a_f"][0]
    slabs.append([(kr, kr + B_ROPE, 1), (None, LANE - B_ROPE, 0)])
    slabs.append([(kr + half, kr + B_ROPE, -1), (kr, kr + half, 1), (af, af + A_HEADS, 1),
                  (None, LANE - B_ROPE - A_HEADS, 0)])
    assert len(slabs) * LANE == N_PROJ
    return slabs


N_SPECIAL_SLABS = 2


def _w_in_kernel(tab_ref, w_ref, sp_ref, o_ref, *, n_layers):
    j = pl.program_id(0)
    n_regular = pl.num_programs(0) - N_SPECIAL_SLABS
    k_chunks = o_ref.shape[2] // LANE
    rows_per_col = n_layers * k_chunks

    @pl.when(j < n_regular)
    def _():
        for kc in range(k_chunks):
            for l in range(n_layers):
                rows = w_ref[pl.ds(kc * n_layers + l, LANE, stride=rows_per_col), :]
                o_ref[l, :, kc * LANE:(kc + 1) * LANE] = rows.astype(BF16)

    @pl.when(j >= n_regular)
    def _():
        o_ref[...] = sp_ref[:, pl.ds(pl.multiple_of((j - n_regular) * LANE, LANE), LANE), :].astype(BF16)


def _prep_w_in(w):
    n_layers, d, n = w.shape
    k_chunks = d // LANE
    rows_per_col = n_layers * k_chunks
    view = jnp.transpose(w.reshape(n_layers, k_chunks, LANE, n), (3, 1, 0, 2))
    flat = view.reshape(n * rows_per_col, LANE)
    slabs = _w_in_slab_sources()
    n_regular = len(slabs) - N_SPECIAL_SLABS
    assert all(len(p) == 1 and p[0][2] == 1 for p in slabs[:n_regular])
    tab = jnp.asarray([p[0][0] for p in slabs[:n_regular]] + [0] * N_SPECIAL_SLABS, jnp.int32)

    def cols(a, b):
        part = flat[a * rows_per_col:b * rows_per_col].reshape(b - a, k_chunks, n_layers, LANE)
        return jnp.transpose(part, (2, 0, 1, 3)).reshape(n_layers, b - a, d)

    special = []
    for pieces in slabs[n_regular:]:
        for start, stop, sign in pieces:
            if start is None:
                special.append(jnp.zeros((n_layers, stop, d), w.dtype))
            else:
                special.append(cols(start, stop) if sign > 0 else -cols(start, stop))
    special = jnp.concatenate(special, axis=1)
    return pl.pallas_call(
        functools.partial(_w_in_kernel, n_layers=n_layers),
        grid_spec=pltpu.PrefetchScalarGridSpec(
            num_scalar_prefetch=1,
            grid=(len(slabs),),
            in_specs=[pl.BlockSpec((pl.Element(LANE * rows_per_col), pl.Element(LANE)),
                                   lambda j, tab: (tab[j] * rows_per_col, 0)),
                      pl.BlockSpec(special.shape, lambda j, tab: (0, 0, 0))],
            out_specs=pl.BlockSpec((n_layers, LANE, d), lambda j, tab: (0, j, 0)),
        ),
        out_shape=jax.ShapeDtypeStruct((n_layers, N_PROJ, d), BF16),
        compiler_params=_cparams(1),
        name="w_in_reorder",
    )(tab, flat, special)


def _proj_col_scale():
    s = jnp.ones((N_PROJ,), F32)
    s = s.at[SLAB_AQ * LANE:(SLAB_AQ + A_HEADS) * LANE].set(LOG2E / math.sqrt(HEAD_DIM))
    s = s.at[SLAB_CQ * LANE:(SLAB_CQ + C_HEADS) * LANE].set(LOG2E / math.sqrt(C_QK))
    return s.reshape(1, N_PROJ)


def _prep_w_uq(w):
    w = w.reshape(B_Q_LORA, B_HEADS, B_NOPE + B_ROPE)
    nope = w[:, :, :B_NOPE].reshape(B_Q_LORA, B_HEADS * B_NOPE)
    rope = w[:, :, B_NOPE:]
    half = B_ROPE // 2
    rot = jnp.concatenate([-rope[:, :, half:], rope[:, :, :half]], axis=2)
    pad = jnp.zeros((B_Q_LORA, B_HEADS, LANE - B_ROPE), w.dtype)
    rope_p = jnp.concatenate([rope, pad], axis=2).reshape(B_Q_LORA, B_HEADS * LANE)
    rot_p = jnp.concatenate([rot, pad], axis=2).reshape(B_Q_LORA, B_HEADS * LANE)
    return jnp.concatenate([nope, rope_p, rot_p], axis=1).astype(BF16)


def _prep_w_ukv(w):
    w = w.reshape(B_KV_LORA, B_HEADS, B_NOPE + B_V)
    k = w[:, :, :B_NOPE].reshape(B_KV_LORA, B_HEADS * B_NOPE)
    v = w[:, :, B_NOPE:].reshape(B_KV_LORA, B_HEADS * B_V)
    return jnp.concatenate([k, v], axis=1).astype(BF16)


def _rope_tables(seq):
    half = B_ROPE // 2
    inv = ROPE_THETA ** (-jnp.arange(half, dtype=F32) / half)
    ang = jnp.arange(seq).astype(F32)[:, None] * inv[None, :]
    pad = jnp.zeros((seq, LANE - B_ROPE), F32)
    cos = jnp.concatenate([jnp.cos(ang), jnp.cos(ang), pad], axis=1)
    sin = jnp.concatenate([jnp.sin(ang), jnp.sin(ang), pad], axis=1)
    return cos, sin


def _t5_bucket(rel):
    nb = REL_BUCKETS // 2
    max_exact = nb // 2
    ret = (rel > 0).astype(jnp.int32) * nb
    n = jnp.abs(rel)
    nf = jnp.maximum(n, 1).astype(F32)
    large = max_exact + (jnp.log(nf / max_exact) / math.log(REL_MAX_DIST / max_exact)
                         * (nb - max_exact)).astype(jnp.int32)
    large = jnp.minimum(large, nb - 1)
    return ret + jnp.where(n < max_exact, n, large)


def _t5_bias_vectors(rel_bias):
    assert TQ == TK and TQ > REL_MAX_DIST
    x = jnp.arange(2 * TQ)
    x = jnp.where(x < TQ, x, x - 2 * TQ)
    bucket = jnp.stack([_t5_bucket(-x - delta) for delta in (0, TQ)])
    rel = rel_bias - rel_bias[REL_BUCKETS // 2 - 1][None, :]
    t = jnp.zeros((C_HEADS,) + bucket.shape, F32)
    for b in range(REL_BUCKETS):
        t = jnp.where(bucket[None] == b, rel[b][:, None, None], t)
    return (t * LOG2E).reshape(C_HEADS, 2, 1, 2 * TQ)


def _inproj_kernel(x_ref, g_ref, w_ref, cs_ref, ob_ref, of_ref, h_scr):
    j = pl.program_id(1)

    @pl.when(j == 0)
    def _():
        x = x_ref[...]
        ms = jnp.mean(x * x, axis=-1, keepdims=True)
        h_scr[...] = (x * lax.rsqrt(ms + EPS) * g_ref[...]).astype(BF16)

    def project(out_ref):
        acc = _nt_dot(h_scr[...], w_ref[...]) * cs_ref[...]
        for s in range(SLABS_PER_TILE):
            out_ref[s] = acc[:, s * LANE:(s + 1) * LANE].astype(out_ref.dtype)

    @pl.when(j < N_BF_TILES)
    def _():
        project(ob_ref)

    @pl.when(j >= N_BF_TILES)
    def _():
        project(of_ref)


def _inproj(x2, g, w, cs, layer):
    m = x2.shape[0]
    tm = TM_IN
    return pl.pallas_call(
        _inproj_kernel,
        grid=(m // tm, N_BF_TILES + N_F_TILES),
        in_specs=[
            pl.BlockSpec((tm, D_MODEL), lambda i, j: (i, 0)),
            pl.BlockSpec((1, D_MODEL), lambda i, j: (0, 0)),
            pl.BlockSpec((None, TN, D_MODEL), lambda i, j: (layer, j, 0)),
            pl.BlockSpec((1, TN), lambda i, j: (0, j)),
        ],
        out_specs=[
            pl.BlockSpec((SLABS_PER_TILE, tm, LANE), lambda i, j: (jnp.minimum(j, N_BF_TILES - 1), i, 0)),
            pl.BlockSpec((SLABS_PER_TILE, tm, LANE), lambda i, j: (jnp.maximum(j - N_BF_TILES, 0), i, 0)),
        ],
        out_shape=[jax.ShapeDtypeStruct((N_BF_SLABS, m, LANE), BF16),
                   jax.ShapeDtypeStruct((N_F_SLABS, m, LANE), F32)],
        scratch_shapes=[pltpu.VMEM((tm, D_MODEL), BF16)],
        compiler_params=_cparams(2),
        name="inproj",
    )(x2, g, w, cs)


def _rms(x, g):
    return x * lax.rsqrt(jnp.mean(x * x, axis=-1, keepdims=True) + EPS) * g


def _mla_prep_kernel(cq_ref, ckv_ref, kr_ref, gq_ref, gkv_ref, wq_ref, wkv_ref, cos_ref, sin_ref,
                     qo_ref, ko_ref, vo_ref, *, n_pos):
    scale = LOG2E / math.sqrt(B_NOPE + B_ROPE)
    cos = cos_ref[...]
    sin = sin_ref[...]
    cq = jnp.concatenate([cq_ref[s] for s in range(B_Q_LORA // LANE)], axis=1)
    q = jnp.dot(_rms(cq, gq_ref[...]).astype(BF16), wq_ref[...], preferred_element_type=F32)
    ckv = jnp.concatenate([ckv_ref[s] for s in range(B_KV_LORA // LANE)], axis=1)
    kv = jnp.dot(_rms(ckv, gkv_ref[...]).astype(BF16), wkv_ref[...], preferred_element_type=F32)
    tm = cos.shape[0]
    pos = (pl.program_id(0) % n_pos) * tm + lax.broadcasted_iota(jnp.int32, (tm, LANE), 0)
    chunk = (pos % TK) >> CHUNK_SHIFT
    lane = lax.broadcasted_iota(jnp.int32, (tm, LANE), 1)
    k_rope = (kr_ref[0] * cos + kr_ref[1] * sin + _chunk_onehot(chunk, lane, B_ROPE)).astype(BF16)
    mask_q = _chunk_maskq(chunk, lane, B_ROPE)
    n_h = B_HEADS * LANE
    for h in range(B_HEADS):
        sl = slice(h * LANE, (h + 1) * LANE)
        q_rope = q[:, n_h + h * LANE:n_h + (h + 1) * LANE] * cos + q[:, 2 * n_h + h * LANE:2 * n_h + (h + 1) * LANE] * sin
        qo_ref[h, :, :LANE] = (q[:, sl] * scale).astype(BF16)
        qo_ref[h, :, LANE:] = (q_rope * scale + mask_q).astype(BF16)
        ko_ref[h, :, :LANE] = kv[:, sl].astype(BF16)
        ko_ref[h, :, LANE:] = k_rope
        vo_ref[h] = kv[:, n_h + h * LANE:n_h + (h + 1) * LANE].astype(BF16)


def _mla_prep(of, gq, gkv, wq, wkv, cos, sin, seq):
    m = of.shape[1]
    tm = TM_PROJ
    n_pos = seq // tm
    nq = B_Q_LORA // LANE
    nkv = B_KV_LORA // LANE
    return pl.pallas_call(
        functools.partial(_mla_prep_kernel, n_pos=n_pos),
        grid=(m // tm,),
        in_specs=[
            pl.BlockSpec((nq, tm, LANE), lambda i: (FSLAB_CQ // nq, i, 0)),
            pl.BlockSpec((nkv, tm, LANE), lambda i: (FSLAB_CKV // nkv, i, 0)),
            pl.BlockSpec((2, tm, LANE), lambda i: (FSLAB_KROPE // 2, i, 0)),
            pl.BlockSpec((1, B_Q_LORA), lambda i: (0, 0)),
            pl.BlockSpec((1, B_KV_LORA), lambda i: (0, 0)),
            pl.BlockSpec(wq.shape, lambda i: (0, 0)),
            pl.BlockSpec(wkv.shape, lambda i: (0, 0)),
            pl.BlockSpec((tm, LANE), lambda i: (i % n_pos, 0)),
            pl.BlockSpec((tm, LANE), lambda i: (i % n_pos, 0)),
        ],
        out_specs=[
            pl.BlockSpec((B_HEADS, tm, 2 * LANE), lambda i: (0, i, 0)),
            pl.BlockSpec((B_HEADS, tm, 2 * LANE), lambda i: (0, i, 0)),
            pl.BlockSpec((B_HEADS, tm, LANE), lambda i: (0, i, 0)),
        ],
        out_shape=[jax.ShapeDtypeStruct((B_HEADS, m, 2 * LANE), BF16),
                   jax.ShapeDtypeStruct((B_HEADS, m, 2 * LANE), BF16),
                   jax.ShapeDtypeStruct((B_HEADS, m, LANE), BF16)],
        compiler_params=_cparams(1),
        name="mla_prep",
    )(of, of, of, gq, gkv, wq, wkv, cos, sin)


CUM_CHUNK = 128


def _fox_bias_kernel(af_ref, bf_ref, kb_ref, cum_scr):
    seq = af_ref.shape[1]
    r = lax.broadcasted_iota(jnp.int32, (CUM_CHUNK, CUM_CHUNK), 0)
    c = lax.broadcasted_iota(jnp.int32, (CUM_CHUNK, CUM_CHUNK), 1)
    tri = (r >= c).astype(F32)
    bvec = bf_ref[...]

    def body(t, carry):
        r0 = pl.multiple_of(t * CUM_CHUNK, CUM_CHUNK)
        x = af_ref[0, pl.ds(r0, CUM_CHUNK), :] + bvec
        logf = jnp.minimum(x, 0.0) - jnp.log1p(jnp.exp(-jnp.abs(x)))
        cs = jnp.dot(tri, logf, preferred_element_type=F32, precision=lax.Precision.HIGHEST) + carry
        cum_scr[pl.ds(r0, CUM_CHUNK), :] = cs
        return cs[CUM_CHUNK - 1:CUM_CHUNK, :]

    lax.fori_loop(0, seq // CUM_CHUNK, body, jnp.zeros((1, LANE), F32))
    lane = lax.broadcasted_iota(jnp.int32, (seq, LANE), 1)
    chunk = (lax.broadcasted_iota(jnp.int32, (seq, LANE), 0) % TK) >> CHUNK_SHIFT
    out = _chunk_onehot(chunk, lane, 0)
    rest = cum_scr[...] * (-LOG2E)
    for i in range(FOX_TERMS):
        term = rest.astype(BF16).astype(F32)
        rest = rest - term
        lo = AF_LANE + FOX_TERM_STRIDE * i
        moved = term if i == 0 else pltpu.roll(term, FOX_TERM_STRIDE * i, 1)
        out = jnp.where((lane >= lo) & (lane < lo + A_HEADS), moved, out)
    kb_ref[0] = out.astype(BF16)


def _fox_bias(of, b_forget, batch, seq):
    bvec = jnp.zeros((1, LANE), F32).at[0, AF_LANE:AF_LANE + A_HEADS].set(b_forget)
    of4 = of.reshape(N_F_SLABS, batch, seq, LANE)
    return pl.pallas_call(
        _fox_bias_kernel,
        grid=(batch,),
        in_specs=[
            pl.BlockSpec((None, 1, seq, LANE), lambda b: (N_F_SLABS - 1, b, 0, 0)),
            pl.BlockSpec((1, LANE), lambda b: (0, 0)),
        ],
        out_specs=pl.BlockSpec((1, seq, LANE), lambda b: (b, 0, 0)),
        out_shape=jax.ShapeDtypeStruct((batch, seq, LANE), BF16),
        scratch_shapes=[pltpu.VMEM((seq, LANE), F32)],
        compiler_params=_cparams(1),
        name="fox_bias",
    )(of4, bvec)


PAIR_Q, PAIR_K, PAIR_DIAG, PAIR_BIAS, PAIR_STATE = 0, 1, 2, 3, 4
ACC_ROWS = HEAD_DIM + 16
PAIRS_PER_TRIP = 4
PIPE_DIST = 2


def _pair_table(n_tiles):
    pad = (0, 0, 1, 0, n_tiles)
    rows = []
    for qi in range(n_tiles):
        for kj in range(qi + 1):
            rows.append((qi, kj, int(kj == qi), 0 if kj == qi else (1 if kj == qi - 1 else 2), qi))
    rows += [pad] * (-len(rows) % PAIRS_PER_TRIP)
    rows = rows + [pad] * PIPE_DIST
    return np.asarray(rows, np.int32).T.copy()


def _tile_rows(ref, tile):
    return ref[pl.ds(pl.multiple_of(tile * TK, TK), TK), :]


def _load_vt(v_ref, vt_scr):
    extra = jnp.where(lax.broadcasted_iota(jnp.int32, (ACC_ROWS - HEAD_DIM, TK), 0) == 0, 1.0, 0.0).astype(BF16)
    for j in range(vt_scr.shape[0]):
        vt_scr[j, :HEAD_DIM, :] = v_ref[j * TK:(j + 1) * TK, :].T
        vt_scr[j, HEAD_DIM:, :] = extra


class _PairBufs:
    def __init__(self, s, mb):
        self.s, self.mb = s, mb


def _pair_scratch():
    return [pltpu.VMEM((TK, TQ), F32), pltpu.VMEM((1, TQ), F32)]


def _state_scratch(n_tiles):
    return [pltpu.VMEM((n_tiles + 1, 1, TQ), F32), pltpu.VMEM((n_tiles + 1, ACC_ROWS, TQ), F32)]


def _init_pipeline(m_all, acc_all):
    m_all[...] = jnp.full(m_all.shape, NEG_BIG, F32)
    acc_all[...] = jnp.zeros(acc_all.shape, F32)


def _pair_bufs(pair_scr):
    return [_PairBufs(*pair_scr[2 * i:2 * i + 2]) for i in range(len(pair_scr) // 2)]


def _qk_store(s, bufs):
    bufs.s[...] = s
    bufs.mb[...] = jnp.max(s, axis=0, keepdims=True)


def _softmax_pv_stage(tab_ref, t, bufs, m_all, vt_scr, acc_all):
    st = tab_ref[PAIR_STATE, t]
    m_prev = m_all[st]
    m_new = jnp.maximum(m_prev, bufs.mb[...])
    alpha = jnp.exp2(m_prev - m_new)
    p = jnp.exp2(bufs.s[...] - m_new).astype(BF16)
    m_all[st] = m_new
    acc_all[st] = alpha * acc_all[st] + jnp.dot(vt_scr[tab_ref[PAIR_K, t]], p, preferred_element_type=F32)


def _run_pipeline(tab_ref, bufs, qk_stage, smpv_stage):
    n_pairs = tab_ref.shape[1] - PIPE_DIST
    u, d = PAIRS_PER_TRIP, PIPE_DIST
    assert u > d

    def body(i, carry):
        for j in range(u):
            c = u * i + j
            smpv_stage(c, bufs[j % u])
            qk_stage(c + d, bufs[(d + j) % u])
        return carry

    for c in range(d):
        qk_stage(c, bufs[c % u])
    lax.fori_loop(0, n_pairs // u, body, 0)


def _normalized(acc_all, qi):
    return acc_all[qi, :HEAD_DIM, :] * (1.0 / acc_all[qi, HEAD_DIM:HEAD_DIM + 1, :])


def _silu(g):
    return g / (1.0 + jnp.exp(-g))


def _head_spec(width, slab0, n_heads, seq):
    return pl.BlockSpec((None, None, seq, width), lambda g: (slab0 + g % n_heads, g // n_heads, 0, 0))


_SMEM_SPEC = pl.BlockSpec(memory_space=pltpu.SMEM)


def _fox_attn_kernel(tab_ref, q_ref, k_ref, kb_ref, v_ref, g_ref, o_ref, vt_scr, qext_scr, m_all, acc_all, *pair_scr):
    bufs = _pair_bufs(pair_scr)
    _load_vt(v_ref, vt_scr)
    _init_pipeline(m_all, acc_all)
    row = lax.broadcasted_iota(jnp.int32, (TQ, LANE), 0)
    lane = lax.broadcasted_iota(jnp.int32, (TQ, LANE), 1)
    rel = lane - (AF_LANE + pl.program_id(0) % A_HEADS)
    ones3 = jnp.where((rel >= 0) & (rel < FOX_TERM_STRIDE * FOX_TERMS) & (rel % FOX_TERM_STRIDE == 0), 1.0, 0.0)
    qext_scr[0] = ones3.astype(BF16)
    qext_scr[1] = (ones3 + _chunk_maskq(row >> CHUNK_SHIFT, lane, 0)).astype(BF16)
    sub_key = lax.broadcasted_iota(jnp.int32, (LANE, LANE), 0)
    sub_qry = lax.broadcasted_iota(jnp.int32, (LANE, LANE), 1)

    def qk_stage(t, bufs):
        q_cat = jnp.concatenate([_tile_rows(q_ref, tab_ref[PAIR_Q, t]), qext_scr[tab_ref[PAIR_DIAG, t]]], axis=1)
        kj = tab_ref[PAIR_K, t]
        k_cat = jnp.concatenate([_tile_rows(k_ref, kj), _tile_rows(kb_ref, kj)], axis=1)
        s = _nt_dot(k_cat, q_cat)
        masked = sub_key - sub_qry > jnp.where(tab_ref[PAIR_DIAG, t] > 0, 0, TK)
        blocks = []
        for r in range(TK // LANE):
            blk = s[r * LANE:(r + 1) * LANE, :]
            mid = jnp.where(masked, NEG_BIG, blk[:, r * LANE:(r + 1) * LANE])
            parts = ([blk[:, :r * LANE]] if r > 0 else []) + [mid]
            parts += [blk[:, (r + 1) * LANE:]] if (r + 1) * LANE < TQ else []
            blocks.append(jnp.concatenate(parts, axis=1))
        _qk_store(jnp.concatenate(blocks, axis=0), bufs)

    _run_pipeline(tab_ref, bufs, qk_stage,
                  lambda t, bufs: _softmax_pv_stage(tab_ref, t, bufs, m_all, vt_scr, acc_all))
    for qi in range(m_all.shape[0] - 1):
        rows = slice(qi * TQ, (qi + 1) * TQ)
        o_ref[rows, :] = (_normalized(acc_all, qi).T * _silu(g_ref[rows, :])).astype(BF16)


def _fox_attn(tab, ob4, kb, of4, batch, seq):
    hs = functools.partial(_head_spec, n_heads=A_HEADS, seq=seq)
    n_tiles = seq // TQ
    return pl.pallas_call(
        _fox_attn_kernel,
        grid=(batch * A_HEADS,),
        in_specs=[_SMEM_SPEC, hs(LANE, SLAB_AQ), hs(LANE, SLAB_AK),
                  pl.BlockSpec((None, seq, LANE), lambda g: (g // A_HEADS, 0, 0)), hs(LANE, SLAB_AV),
                  hs(LANE, FSLAB_AGATE)],
        out_specs=hs(LANE, 0),
        out_shape=jax.ShapeDtypeStruct((A_HEADS, batch, seq, LANE), BF16),
        scratch_shapes=[pltpu.VMEM((n_tiles, ACC_ROWS, TK), BF16), pltpu.VMEM((2, TQ, LANE), BF16)]
        + _state_scratch(n_tiles) + PAIRS_PER_TRIP * _pair_scratch(),
        compiler_params=_cparams(1),
        name="fox_attn",
    )(tab, ob4, ob4, kb, ob4, of4)


def _mla_attn_kernel(tab_ref, q_ref, k_ref, v_ref, g_ref, o_ref, vt_scr, m_all, acc_all, *pair_scr):
    bufs = _pair_bufs(pair_scr)
    _load_vt(v_ref, vt_scr)
    _init_pipeline(m_all, acc_all)
    lane = lax.broadcasted_iota(jnp.int32, (TQ, LANE), 1)

    def qk_stage(t, bufs):
        q = _tile_rows(q_ref, tab_ref[PAIR_Q, t])
        first_off = jnp.where(tab_ref[PAIR_DIAG, t] > 0, LANE, B_ROPE)
        q_hi = jnp.where(lane >= first_off, jnp.zeros((), BF16), q[:, LANE:])
        q_cat = jnp.concatenate([q[:, :LANE], q_hi], axis=1)
        _qk_store(_nt_dot(_tile_rows(k_ref, tab_ref[PAIR_K, t]), q_cat), bufs)

    _run_pipeline(tab_ref, bufs, qk_stage,
                  lambda t, bufs: _softmax_pv_stage(tab_ref, t, bufs, m_all, vt_scr, acc_all))
    for qi in range(m_all.shape[0] - 1):
        rows = slice(qi * TQ, (qi + 1) * TQ)
        o_ref[rows, :] = (_normalized(acc_all, qi).T * _silu(g_ref[rows, :])).astype(BF16)


def _mla_attn(tab, q4, k4, v4, of4, batch, seq):
    hs = functools.partial(_head_spec, n_heads=B_HEADS, seq=seq)
    n_tiles = seq // TQ
    return pl.pallas_call(
        _mla_attn_kernel,
        grid=(batch * B_HEADS,),
        in_specs=[_SMEM_SPEC, hs(2 * LANE, 0), hs(2 * LANE, 0), hs(LANE, 0), hs(LANE, FSLAB_BGATE)],
        out_specs=hs(LANE, 0),
        out_shape=jax.ShapeDtypeStruct((B_HEADS, batch, seq, LANE), BF16),
        scratch_shapes=[pltpu.VMEM((n_tiles, ACC_ROWS, TK), BF16)]
        + _state_scratch(n_tiles) + PAIRS_PER_TRIP * _pair_scratch(),
        compiler_params=_cparams(1),
        name="mla_attn",
    )(tab, q4, k4, v4, of4)


def _diff_attn_kernel(tab_ref, q_ref, k_ref, v_ref, g_ref, bvec_ref, lq1_ref, lk1_ref, lq2_ref, lk2_ref, sg_ref,
                      o_ref, vt_scr, kext_scr, qext_scr, bias_scr, m1_all, acc1_all, m2_all, acc2_all, *pair_scr,
                      lam_init):
    flat = _pair_bufs(pair_scr)
    bufs = [(flat[2 * i], flat[2 * i + 1]) for i in range(PAIRS_PER_TRIP)]
    states = ((m1_all, acc1_all), (m2_all, acc2_all))
    _load_vt(v_ref, vt_scr)
    for m_all, acc_all in states:
        _init_pipeline(m_all, acc_all)
    for i in range(2):
        gen = jnp.broadcast_to(bvec_ref[i], (TK, 2 * TQ))
        bias_scr[i] = pltpu.roll(gen, 0, 1, stride=1, stride_axis=0)[:, :TQ]
    bias_scr[2] = jnp.zeros((TK, TQ), F32)
    row = lax.broadcasted_iota(jnp.int32, (TQ, LANE), 0)
    lane = lax.broadcasted_iota(jnp.int32, (TQ, LANE), 1)
    kext_scr[...] = _chunk_onehot(row >> CHUNK_SHIFT, lane, 0).astype(BF16)
    qext_scr[0] = jnp.zeros((TQ, LANE), BF16)
    qext_scr[1] = _chunk_maskq(row >> CHUNK_SHIFT, lane, 0).astype(BF16)
    lam = (jnp.exp(jnp.sum(lq1_ref[...] * lk1_ref[...], axis=-1, keepdims=True))
           - jnp.exp(jnp.sum(lq2_ref[...] * lk2_ref[...], axis=-1, keepdims=True)) + lam_init)

    def qk_stage(t, bufs2):
        q = _tile_rows(q_ref, tab_ref[PAIR_Q, t])
        qext = qext_scr[tab_ref[PAIR_DIAG, t]]
        zero = jnp.zeros_like(q)
        k_cat = jnp.concatenate([_tile_rows(k_ref, tab_ref[PAIR_K, t]), kext_scr[...]], axis=1)
        bias = bias_scr[tab_ref[PAIR_BIAS, t]]
        for bufs, q_map in zip(bufs2, (jnp.where(lane < C_QK, q, zero), jnp.where(lane >= C_QK, q, zero))):
            _qk_store(_nt_dot(k_cat, jnp.concatenate([q_map, qext], axis=1)) + bias, bufs)

    def smpv_stage(t, bufs2):
        for (m_all, acc_all), bufs in zip(states, bufs2):
            _softmax_pv_stage(tab_ref, t, bufs, m_all, vt_scr, acc_all)

    _run_pipeline(tab_ref, bufs, qk_stage, smpv_stage)
    for qi in range(m1_all.shape[0] - 1):
        rows = slice(qi * TQ, (qi + 1) * TQ)
        o = _normalized(acc1_all, qi) - lam * _normalized(acc2_all, qi)
        o = o * lax.rsqrt(jnp.mean(o * o, axis=0, keepdims=True) + EPS)
        out = o.T * sg_ref[...] * (1.0 - lam_init)
        o_ref[rows, :] = (out * _silu(g_ref[rows, :])).astype(BF16)


def _diff_attn(tab, ob4, of4, bias, lq1, lk1, lq2, lk2, sg, lam_init, batch, seq):
    hs = functools.partial(_head_spec, n_heads=C_HEADS, seq=seq)
    vec = lambda n: pl.BlockSpec((1, n), lambda g: (0, 0))
    n_tiles = seq // TQ
    return pl.pallas_call(
        functools.partial(_diff_attn_kernel, lam_init=lam_init),
        grid=(batch * C_HEADS,),
        in_specs=[_SMEM_SPEC, hs(LANE, SLAB_CQ), hs(LANE, SLAB_CK), hs(LANE, SLAB_CV), hs(LANE, FSLAB_CGATE),
                  pl.BlockSpec((None, 2, 1, 2 * TQ), lambda g: (g % C_HEADS, 0, 0, 0)),
                  vec(C_QK), vec(C_QK), vec(C_QK), vec(C_QK), vec(C_V)],
        out_specs=hs(LANE, 0),
        out_shape=jax.ShapeDtypeStruct((C_HEADS, batch, seq, LANE), BF16),
        scratch_shapes=[pltpu.VMEM((n_tiles, ACC_ROWS, TK), BF16), pltpu.VMEM((TK, LANE), BF16),
                        pltpu.VMEM((2, TQ, LANE), BF16), pltpu.VMEM((3, TK, TQ), F32)]
        + 2 * _state_scratch(n_tiles) + 2 * PAIRS_PER_TRIP * _pair_scratch(),
        compiler_params=_cparams(1),
        name="diff_attn",
    )(tab, ob4, ob4, ob4, of4, bias, lq1, lk1, lq2, lk2, sg)


def _outproj_kernel(a_ref, b_ref, c_ref, w_ref, x_ref, g_ref, o_ref, *, final_norm):
    mixed = jnp.concatenate([a_ref[h] for h in range(A_HEADS)] + [b_ref[h] for h in range(B_HEADS)]
                            + [c_ref[h] for h in range(C_HEADS)], axis=1)
    y = x_ref[...] + jnp.dot(mixed, w_ref[...], preferred_element_type=F32)
    if final_norm:
        y = _rms(y, g_ref[...])
    o_ref[...] = y


def _outproj(mix_a, mix_b, mix_c, w, x2, g, layer, final_norm):
    m = x2.shape[0]
    tm = TM_PROJ
    return pl.pallas_call(
        functools.partial(_outproj_kernel, final_norm=final_norm),
        grid=(m // tm,),
        in_specs=[
            pl.BlockSpec((A_HEADS, tm, LANE), lambda i: (0, i, 0)),
            pl.BlockSpec((B_HEADS, tm, LANE), lambda i: (0, i, 0)),
            pl.BlockSpec((C_HEADS, tm, LANE), lambda i: (0, i, 0)),
            pl.BlockSpec((None, D_MIX, D_MODEL), lambda i: (layer, 0, 0)),
            pl.BlockSpec((tm, D_MODEL), lambda i: (i, 0)),
            pl.BlockSpec((1, D_MODEL), lambda i: (0, 0)),
        ],
        out_specs=pl.BlockSpec((tm, D_MODEL), lambda i: (i, 0)),
        out_shape=jax.ShapeDtypeStruct((m, D_MODEL), F32),
        compiler_params=_cparams(1),
        name="outproj",
    )(mix_a, mix_b, mix_c, w, x2, g)


def kernel(x, norm_g, w_in, b_forget, mla_q_norm_g, w_uq, mla_kv_norm_g, w_ukv, lambda_q1, lambda_k1,
           lambda_q2, lambda_k2, diff_subln_g, rel_bias, w_out, final_norm_g):
    batch, seq, d = x.shape
    assert d == D_MODEL and seq % TQ == 0 and seq % TM_PROJ == 0 and (batch * seq) % TM_IN == 0
    m = batch * seq
    x2 = x.reshape(m, d)
    cs = _proj_col_scale()
    cos, sin = _rope_tables(seq)
    bias_vecs = _t5_bias_vectors(rel_bias)
    tab = jnp.asarray(_pair_table(seq // TQ))
    w_in_r = _prep_w_in(w_in)
    w_out_b = w_out.astype(BF16)
    for l in range(N_LAYERS):
        ob, of = _inproj(x2, norm_g[l].reshape(1, d), w_in_r, cs, l)
        ob4 = ob.reshape(N_BF_SLABS, batch, seq, LANE)
        of4 = of.reshape(N_F_SLABS, batch, seq, LANE)
        kb = _fox_bias(of, b_forget[l], batch, seq)
        mix_a = _fox_attn(tab, ob4, kb, of4, batch, seq)
        qb, kbm, vb = _mla_prep(of, mla_q_norm_g[l].reshape(1, -1), mla_kv_norm_g[l].reshape(1, -1),
                                _prep_w_uq(w_uq[l]), _prep_w_ukv(w_ukv[l]), cos, sin, seq)
        r4 = lambda a: a.reshape(a.shape[0], batch, seq, a.shape[-1])
        mix_b = _mla_attn(tab, r4(qb), r4(kbm), r4(vb), of4, batch, seq)
        lam_init = 0.8 - 0.6 * math.exp(-0.3 * l)
        mix_c = _diff_attn(tab, ob4, of4, bias_vecs, lambda_q1[l].reshape(1, -1), lambda_k1[l].reshape(1, -1),
                           lambda_q2[l].reshape(1, -1), lambda_k2[l].reshape(1, -1),
                           diff_subln_g[l].reshape(1, -1), lam_init, batch, seq)
        x2 = _outproj(mix_a.reshape(A_HEADS, m, LANE), mix_b.reshape(B_HEADS, m, LANE),
                      mix_c.reshape(C_HEADS, m, LANE), w_out_b, x2,
                      final_norm_g.reshape(1, d), l, final_norm=(l == N_LAYERS - 1))
    return x2.reshape(batch, seq, d)
```

```python
import functools
import math

import numpy as np
import jax
import jax.numpy as jnp
from jax import lax
from jax.experimental import pallas as pl
from jax.experimental.pallas import tpu as pltpu

F32 = jnp.float32
BF16 = jnp.bfloat16

D_MODEL = 2048
N_LAYERS = 2
CHUNK = 64
HEAD_DIM = 128
EPS = 1e-6
A_HEADS = 6
A_DIM = A_HEADS * HEAD_DIM
B_HEADS = 6
B_Q_LORA = 512
B_KV_LORA = 256
B_NOPE = 128
B_ROPE = 64
B_V = 128
B_DIM = B_HEADS * B_V
ROPE_THETA = 10000.0
C_HEADS = 4
C_QK = 64
C_V = 2 * C_QK
C_DIM = C_HEADS * C_V
REL_BUCKETS = 32
REL_MAX_DIST = 128
D_MIX = A_DIM + B_DIM + C_DIM

LANE = 128
V7X_VMEM_BYTES = 64 * 1024 * 1024
VMEM_LIMIT = 56 * 1024 * 1024

LOG2E = math.log2(math.e)
NEG_BIG = -1e30

TN = 768
SLABS_PER_TILE = TN // LANE
N_BF_SLABS = 30
N_F_SLABS = 24
N_BF_TILES = N_BF_SLABS // SLABS_PER_TILE
N_F_TILES = N_F_SLABS // SLABS_PER_TILE
N_PROJ = (N_BF_SLABS + N_F_SLABS) * LANE
SLAB_AQ, SLAB_AK, SLAB_AV, SLAB_CQ, SLAB_CK, SLAB_CV = 0, 6, 12, 18, 22, 26
FSLAB_AGATE, FSLAB_BGATE, FSLAB_CGATE, FSLAB_CQ, FSLAB_CKV, FSLAB_KROPE = 0, 6, 12, 16, 20, 22
AF_LANE = 64

TM_PROJ = 512
TM_IN = 1024
TQ = 512
TK = 512

CHUNK_SHIFT = CHUNK.bit_length() - 1
N_TILE_CHUNKS = TK // CHUNK
FOX_TERM_STRIDE = A_HEADS
FOX_TERMS = 3


def _chunk_onehot(row_chunk, lane, lane0):
    return jnp.where(lane - lane0 == row_chunk, 1.0, 0.0)


def _chunk_maskq(row_chunk, lane, lane0):
    c = lane - lane0
    return jnp.where((c >= 0) & (c < N_TILE_CHUNKS) & (row_chunk < c), NEG_BIG, 0.0)


def _nt_dot(a, b):
    return lax.dot_general(a, b, (((1,), (1,)), ((), ())), preferred_element_type=F32)


def _cparams(n_grid):
    return pltpu.CompilerParams(dimension_semantics=("arbitrary",) * n_grid,
                                vmem_limit_bytes=VMEM_LIMIT)


def _w_in_slab_sources():
    o = 0
    seg = {}
    for name, size in (("a_q", A_DIM), ("a_k", A_DIM), ("a_v", A_DIM), ("a_f", A_HEADS), ("a_gate", A_DIM),
                       ("b_cq", B_Q_LORA), ("b_ckv", B_KV_LORA), ("b_krope", B_ROPE), ("b_gate", B_DIM),
                       ("c_q", C_DIM), ("c_k", C_DIM), ("c_v", C_DIM), ("c_gate", C_DIM)):
        seg[name] = (o, size)
        o += size
    slabs = []
    for name in ("a_q", "a_k", "a_v", "c_q", "c_k", "c_v", "a_gate", "b_gate", "c_gate", "b_cq", "b_ckv"):
        start, size = seg[name]
        slabs += [[(start + i, start + i + LANE, 1)] for i in range(0, size, LANE)]
    kr, half = seg["b_krope"][0], B_ROPE // 2
    af = seg["a_f"][0]
    slabs.append([(kr, kr + B_ROPE, 1), (None, LANE - B_ROPE, 0)])
    slabs.append([(kr + half, kr + B_ROPE, -1), (kr, kr + half, 1), (af, af + A_HEADS, 1),
                  (None, LANE - B_ROPE - A_HEADS, 0)])
    assert len(slabs) * LANE == N_PROJ
    return slabs


N_SPECIAL_SLABS = 2


def _w_in_kernel(tab_ref, w_ref, sp_ref, o_ref, *, n_layers):
    j = pl.program_id(0)
    n_regular = pl.num_programs(0) - N_SPECIAL_SLABS
    k_chunks = o_ref.shape[2] // LANE
    rows_per_col = n_layers * k_chunks

    @pl.when(j < n_regular)
    def _():
        for kc in range(k_chunks):
            for l in range(n_layers):
                rows = w_ref[pl.ds(kc * n_layers + l, LANE, stride=rows_per_col), :]
                o_ref[l, :, kc * LANE:(kc + 1) * LANE] = rows.astype(BF16)

    @pl.when(j >= n_regular)
    def _():
        o_ref[...] = sp_ref[:, pl.ds(pl.multiple_of((j - n_regular) * LANE, LANE), LANE), :].astype(BF16)


def _prep_w_in(w):
    n_layers, d, n = w.shape
    k_chunks = d // LANE
    rows_per_col = n_layers * k_chunks
    view = jnp.transpose(w.reshape(n_layers, k_chunks, LANE, n), (3, 1, 0, 2))
    flat = view.reshape(n * rows_per_col, LANE)
    slabs = _w_in_slab_sources()
    n_regular = len(slabs) - N_SPECIAL_SLABS
    assert all(len(p) == 1 and p[0][2] == 1 for p in slabs[:n_regular])
    tab = jnp.asarray([p[0][0] for p in slabs[:n_regular]] + [0] * N_SPECIAL_SLABS, jnp.int32)

    def cols(a, b):
        part = flat[a * rows_per_col:b * rows_per_col].reshape(b - a, k_chunks, n_layers, LANE)
        return jnp.transpose(part, (2, 0, 1, 3)).reshape(n_layers, b - a, d)

    special = []
    for pieces in slabs[n_regular:]:
        for start, stop, sign in pieces:
            if start is None:
                special.append(jnp.zeros((n_layers, stop, d), w.dtype))
            else:
                special.append(cols(start, stop) if sign > 0 else -cols(start, stop))
    special = jnp.concatenate(special, axis=1)
    return pl.pallas_call(
        functools.partial(_w_in_kernel, n_layers=n_layers),
        grid_spec=pltpu.PrefetchScalarGridSpec(
            num_scalar_prefetch=1,
            grid=(len(slabs),),
            in_specs=[pl.BlockSpec((pl.Element(LANE * rows_per_col), pl.Element(LANE)),
                                   lambda j, tab: (tab[j] * rows_per_col, 0)),
                      pl.BlockSpec(special.shape, lambda j, tab: (0, 0, 0))],
            out_specs=pl.BlockSpec((n_layers, LANE, d), lambda j, tab: (0, j, 0)),
        ),
        out_shape=jax.ShapeDtypeStruct((n_layers, N_PROJ, d), BF16),
        compiler_params=_cparams(1),
        name="w_in_reorder",
    )(tab, flat, special)


def _proj_col_scale():
    s = jnp.ones((N_PROJ,), F32)
    s = s.at[SLAB_AQ * LANE:(SLAB_AQ + A_HEADS) * LANE].set(LOG2E / math.sqrt(HEAD_DIM))
    s = s.at[SLAB_CQ * LANE:(SLAB_CQ + C_HEADS) * LANE].set(LOG2E / math.sqrt(C_QK))
    return s.reshape(1, N_PROJ)


def _prep_w_uq(w):
    w = w.reshape(B_Q_LORA, B_HEADS, B_NOPE + B_ROPE)
    nope = w[:, :, :B_NOPE].reshape(B_Q_LORA, B_HEADS * B_NOPE)
    rope = w[:, :, B_NOPE:]
    half = B_ROPE // 2
    rot = jnp.concatenate([-rope[:, :, half:], rope[:, :, :half]], axis=2)
    both = jnp.concatenate([rope, rot], axis=2).reshape(B_Q_LORA, B_HEADS * LANE)
    return jnp.concatenate([nope, both], axis=1).astype(BF16)


def _prep_w_ukv(w):
    w = w.reshape(B_KV_LORA, B_HEADS, B_NOPE + B_V)
    k = w[:, :, :B_NOPE].reshape(B_KV_LORA, B_HEADS * B_NOPE)
    v = w[:, :, B_NOPE:].reshape(B_KV_LORA, B_HEADS * B_V)
    return jnp.concatenate([k, v], axis=1).astype(BF16)


def _rope_tables(seq):
    half = B_ROPE // 2
    inv = ROPE_THETA ** (-jnp.arange(half, dtype=F32) / half)
    ang = jnp.arange(seq).astype(F32)[:, None] * inv[None, :]
    pad = jnp.zeros((seq, LANE - B_ROPE), F32)
    cos = jnp.concatenate([jnp.cos(ang), jnp.cos(ang), pad], axis=1)
    sin = jnp.concatenate([jnp.sin(ang), jnp.sin(ang), pad], axis=1)
    return cos, sin


def _t5_bucket(rel):
    nb = REL_BUCKETS // 2
    max_exact = nb // 2
    ret = (rel > 0).astype(jnp.int32) * nb
    n = jnp.abs(rel)
    nf = jnp.maximum(n, 1).astype(F32)
    large = max_exact + (jnp.log(nf / max_exact) / math.log(REL_MAX_DIST / max_exact)
                         * (nb - max_exact)).astype(jnp.int32)
    large = jnp.minimum(large, nb - 1)
    return ret + jnp.where(n < max_exact, n, large)


def _t5_bias_vectors(rel_bias):
    assert TQ == TK and TQ > REL_MAX_DIST
    x = jnp.arange(2 * TQ)
    x = jnp.where(x < TQ, x, x - 2 * TQ)
    bucket = jnp.stack([_t5_bucket(-x - delta) for delta in (0, TQ)])
    rel = rel_bias - rel_bias[REL_BUCKETS // 2 - 1][None, :]
    t = jnp.zeros((C_HEADS,) + bucket.shape, F32)
    for b in range(REL_BUCKETS):
        t = jnp.where(bucket[None] == b, rel[b][:, None, None], t)
    return (t * LOG2E).reshape(C_HEADS, 2, 1, 2 * TQ)


def _inproj_kernel(x_ref, g_ref, w_ref, cs_ref, ob_ref, of_ref, h_scr):
    j = pl.program_id(1)

    @pl.when(j == 0)
    def _():
        x = x_ref[...]
        ms = jnp.mean(x * x, axis=-1, keepdims=True)
        h_scr[...] = (x * lax.rsqrt(ms + EPS) * g_ref[...]).astype(BF16)

    def project(out_ref):
        acc = _nt_dot(h_scr[...], w_ref[...]) * cs_ref[...]
        for s in range(SLABS_PER_TILE):
            out_ref[s] = acc[:, s * LANE:(s + 1) * LANE].astype(out_ref.dtype)

    @pl.when(j < N_BF_TILES)
    def _():
        project(ob_ref)

    @pl.when(j >= N_BF_TILES)
    def _():
        project(of_ref)


def _inproj(x2, g, w, cs, layer):
    m = x2.shape[0]
    tm = TM_IN
    return pl.pallas_call(
        _inproj_kernel,
        grid=(m // tm, N_BF_TILES + N_F_TILES),
        in_specs=[
            pl.BlockSpec((tm, D_MODEL), lambda i, j: (i, 0)),
            pl.BlockSpec((1, D_MODEL), lambda i, j: (0, 0)),
            pl.BlockSpec((None, TN, D_MODEL), lambda i, j: (layer, j, 0)),
            pl.BlockSpec((1, TN), lambda i, j: (0, j)),
        ],
        out_specs=[
            pl.BlockSpec((SLABS_PER_TILE, tm, LANE), lambda i, j: (jnp.minimum(j, N_BF_TILES - 1), i, 0)),
            pl.BlockSpec((SLABS_PER_TILE, tm, LANE), lambda i, j: (jnp.maximum(j - N_BF_TILES, 0), i, 0)),
        ],
        out_shape=[jax.ShapeDtypeStruct((N_BF_SLABS, m, LANE), BF16),
                   jax.ShapeDtypeStruct((N_F_SLABS, m, LANE), F32)],
        scratch_shapes=[pltpu.VMEM((tm, D_MODEL), BF16)],
        compiler_params=_cparams(2),
        name="inproj",
    )(x2, g, w, cs)


def _rms(x, g):
    return x * lax.rsqrt(jnp.mean(x * x, axis=-1, keepdims=True) + EPS) * g


def _mla_prep_kernel(cq_ref, ckv_ref, kr_ref, gq_ref, gkv_ref, wq_ref, wkv_ref, cos_ref, sin_ref,
                     qo_ref, ko_ref, vo_ref, *, n_pos):
    scale = LOG2E / math.sqrt(B_NOPE + B_ROPE)
    cos = cos_ref[...]
    sin = sin_ref[...]
    cq = jnp.concatenate([cq_ref[s] for s in range(B_Q_LORA // LANE)], axis=1)
    q = jnp.dot(_rms(cq, gq_ref[...]).astype(BF16), wq_ref[...], preferred_element_type=F32)
    ckv = jnp.concatenate([ckv_ref[s] for s in range(B_KV_LORA // LANE)], axis=1)
    kv = jnp.dot(_rms(ckv, gkv_ref[...]).astype(BF16), wkv_ref[...], preferred_element_type=F32)
    tm = cos.shape[0]
    pos = (pl.program_id(0) % n_pos) * tm + lax.broadcasted_iota(jnp.int32, (tm, LANE), 0)
    chunk = (pos % TK) >> CHUNK_SHIFT
    lane = lax.broadcasted_iota(jnp.int32, (tm, LANE), 1)
    k_rope = (kr_ref[0] * cos + kr_ref[1] * sin + _chunk_onehot(chunk, lane, B_ROPE)).astype(BF16)
    mask_q = _chunk_maskq(chunk, lane, B_ROPE)
    cos_sin = cos + pltpu.roll(sin, B_ROPE, 1)
    n_h = B_HEADS * LANE
    for h in range(B_HEADS):
        sl = slice(h * LANE, (h + 1) * LANE)
        t = q[:, n_h + h * LANE:n_h + (h + 1) * LANE] * cos_sin
        q_rope = t + pltpu.roll(t, B_ROPE, 1)
        qo_ref[h, :, :LANE] = (q[:, sl] * scale).astype(BF16)
        qo_ref[h, :, LANE:] = jnp.where(lane < B_ROPE, q_rope * scale, mask_q).astype(BF16)
        ko_ref[h, :, :LANE] = kv[:, sl].astype(BF16)
        ko_ref[h, :, LANE:] = k_rope
        vo_ref[h] = kv[:, n_h + h * LANE:n_h + (h + 1) * LANE].astype(BF16)


def _mla_prep(of, gq, gkv, wq, wkv, cos, sin, seq):
    m = of.shape[1]
    tm = TM_PROJ
    n_pos = seq // tm
    nq = B_Q_LORA // LANE
    nkv = B_KV_LORA // LANE
    return pl.pallas_call(
        functools.partial(_mla_prep_kernel, n_pos=n_pos),
        grid=(m // tm,),
        in_specs=[
            pl.BlockSpec((nq, tm, LANE), lambda i: (FSLAB_CQ // nq, i, 0)),
            pl.BlockSpec((nkv, tm, LANE), lambda i: (FSLAB_CKV // nkv, i, 0)),
            pl.BlockSpec((2, tm, LANE), lambda i: (FSLAB_KROPE // 2, i, 0)),
            pl.BlockSpec((1, B_Q_LORA), lambda i: (0, 0)),
            pl.BlockSpec((1, B_KV_LORA), lambda i: (0, 0)),
            pl.BlockSpec(wq.shape, lambda i: (0, 0)),
            pl.BlockSpec(wkv.shape, lambda i: (0, 0)),
            pl.BlockSpec((tm, LANE), lambda i: (i % n_pos, 0)),
            pl.BlockSpec((tm, LANE), lambda i: (i % n_pos, 0)),
        ],
        out_specs=[
            pl.BlockSpec((B_HEADS, tm, 2 * LANE), lambda i: (0, i, 0)),
            pl.BlockSpec((B_HEADS, tm, 2 * LANE), lambda i: (0, i, 0)),
            pl.BlockSpec((B_HEADS, tm, LANE), lambda i: (0, i, 0)),
        ],
        out_shape=[jax.ShapeDtypeStruct((B_HEADS, m, 2 * LANE), BF16),
                   jax.ShapeDtypeStruct((B_HEADS, m, 2 * LANE), BF16),
                   jax.ShapeDtypeStruct((B_HEADS, m, LANE), BF16)],
        compiler_params=_cparams(1),
        name="mla_prep",
    )(of, of, of, gq, gkv, wq, wkv, cos, sin)


CUM_CHUNK = 128


def _fox_bias_kernel(af_ref, bf_ref, kb_ref, cum_scr):
    seq = af_ref.shape[1]
    r = lax.broadcasted_iota(jnp.int32, (CUM_CHUNK, CUM_CHUNK), 0)
    c = lax.broadcasted_iota(jnp.int32, (CUM_CHUNK, CUM_CHUNK), 1)
    tri = jnp.where(r >= c, 1.0, 0.0).astype(BF16)
    bvec = bf_ref[...]

    def body(t, carry):
        r0 = pl.multiple_of(t * CUM_CHUNK, CUM_CHUNK)
        x = af_ref[0, pl.ds(r0, CUM_CHUNK), :] + bvec
        logf = jnp.minimum(x, 0.0) - jnp.log1p(jnp.exp(-jnp.abs(x)))
        cs = carry
        rest = logf
        for _ in range(FOX_TERMS):
            term = rest.astype(BF16)
            rest = rest - term.astype(F32)
            cs = cs + jnp.dot(tri, term, preferred_element_type=F32)
        cum_scr[pl.ds(r0, CUM_CHUNK), :] = cs
        return cs[CUM_CHUNK - 1:CUM_CHUNK, :]

    lax.fori_loop(0, seq // CUM_CHUNK, body, jnp.zeros((1, LANE), F32))
    lane = lax.broadcasted_iota(jnp.int32, (seq, LANE), 1)
    chunk = (lax.broadcasted_iota(jnp.int32, (seq, LANE), 0) % TK) >> CHUNK_SHIFT
    out = _chunk_onehot(chunk, lane, 0)
    rest = cum_scr[...] * (-LOG2E)
    for i in range(FOX_TERMS):
        term = rest.astype(BF16).astype(F32)
        rest = rest - term
        lo = AF_LANE + FOX_TERM_STRIDE * i
        moved = term if i == 0 else pltpu.roll(term, FOX_TERM_STRIDE * i, 1)
        out = jnp.where((lane >= lo) & (lane < lo + A_HEADS), moved, out)
    kb_ref[0] = out.astype(BF16)


def _fox_bias(of, b_forget, batch, seq):
    bvec = jnp.zeros((1, LANE), F32).at[0, AF_LANE:AF_LANE + A_HEADS].set(b_forget)
    of4 = of.reshape(N_F_SLABS, batch, seq, LANE)
    return pl.pallas_call(
        _fox_bias_kernel,
        grid=(batch,),
        in_specs=[
            pl.BlockSpec((None, 1, seq, LANE), lambda b: (N_F_SLABS - 1, b, 0, 0)),
            pl.BlockSpec((1, LANE), lambda b: (0, 0)),
        ],
        out_specs=pl.BlockSpec((1, seq, LANE), lambda b: (b, 0, 0)),
        out_shape=jax.ShapeDtypeStruct((batch, seq, LANE), BF16),
        scratch_shapes=[pltpu.VMEM((seq, LANE), F32)],
        compiler_params=_cparams(1),
        name="fox_bias",
    )(of4, bvec)


PAIR_Q, PAIR_K, PAIR_DIAG, PAIR_BIAS, PAIR_STATE = 0, 1, 2, 3, 4
ACC_ROWS = HEAD_DIM + 16
PAIRS_PER_TRIP = 6
PIPE_DIST = 4


def _pair_table(n_tiles):
    pad = (0, 0, 1, 0, n_tiles)
    rows = []
    for qi in range(n_tiles):
        for kj in range(qi + 1):
            rows.append((qi, kj, int(kj == qi), 0 if kj == qi else (1 if kj == qi - 1 else 2), qi))
    rows += [pad] * (-len(rows) % PAIRS_PER_TRIP)
    rows = rows + [pad] * PIPE_DIST
    return np.asarray(rows, np.int32).T.copy()


def _tile_rows(ref, tile):
    return ref[pl.ds(pl.multiple_of(tile * TK, TK), TK), :]


def _load_vt(v_ref, vt_scr):
    extra = jnp.where(lax.broadcasted_iota(jnp.int32, (ACC_ROWS - HEAD_DIM, TK), 0) == 0, 1.0, 0.0).astype(BF16)
    for j in range(vt_scr.shape[0]):
        vt_scr[j, :HEAD_DIM, :] = v_ref[j * TK:(j + 1) * TK, :].T
        vt_scr[j, HEAD_DIM:, :] = extra


class _PairBufs:
    def __init__(self, s, mb):
        self.s, self.mb = s, mb


def _pair_scratch():
    return [pltpu.VMEM((TK, TQ), F32), pltpu.VMEM((1, TQ), F32)]


def _state_scratch(n_tiles):
    return [pltpu.VMEM((n_tiles + 1, 1, TQ), F32), pltpu.VMEM((n_tiles + 1, ACC_ROWS, TQ), F32)]


def _init_pipeline(m_all, acc_all):
    m_all[...] = jnp.full(m_all.shape, NEG_BIG, F32)
    acc_all[...] = jnp.zeros(acc_all.shape, F32)


def _pair_bufs(pair_scr):
    return [_PairBufs(*pair_scr[2 * i:2 * i + 2]) for i in range(len(pair_scr) // 2)]


def _qk_store(s, bufs):
    bufs.s[...] = s
    bufs.mb[...] = jnp.max(s, axis=0, keepdims=True)


def _softmax_pv_stage(tab_ref, t, bufs, m_all, vt_scr, acc_all):
    st = tab_ref[PAIR_STATE, t]
    m_prev = m_all[st]
    m_new = jnp.maximum(m_prev, bufs.mb[...])
    alpha = jnp.exp2(m_prev - m_new)
    p = jnp.exp2(bufs.s[...] - m_new).astype(BF16)
    m_all[st] = m_new
    acc_all[st] = alpha * acc_all[st] + jnp.dot(vt_scr[tab_ref[PAIR_K, t]], p, preferred_element_type=F32)


def _run_pipeline(tab_ref, bufs, qk_stage, smpv_stage):
    n_pairs = tab_ref.shape[1] - PIPE_DIST
    u, d = PAIRS_PER_TRIP, PIPE_DIST
    assert u > d

    def body(i, carry):
        for j in range(u):
            c = u * i + j
            smpv_stage(c, bufs[j % u])
            qk_stage(c + d, bufs[(d + j) % u])
        return carry

    for c in range(d):
        qk_stage(c, bufs[c % u])
    lax.fori_loop(0, n_pairs // u, body, 0)


def _normalized(acc_all, qi):
    return acc_all[qi, :HEAD_DIM, :] * (1.0 / acc_all[qi, HEAD_DIM:HEAD_DIM + 1, :])


def _silu(g):
    return g / (1.0 + jnp.exp(-g))


def _head_spec(width, slab0, n_heads, seq):
    return pl.BlockSpec((None, None, seq, width), lambda g: (slab0 + g % n_heads, g // n_heads, 0, 0))


_SMEM_SPEC = pl.BlockSpec(memory_space=pltpu.SMEM)


def _fox_attn_kernel(tab_ref, q_ref, k_ref, kb_ref, v_ref, g_ref, o_ref, vt_scr, qext_scr, m_all, acc_all, *pair_scr):
    bufs = _pair_bufs(pair_scr)
    _load_vt(v_ref, vt_scr)
    _init_pipeline(m_all, acc_all)
    row = lax.broadcasted_iota(jnp.int32, (TQ, LANE), 0)
    lane = lax.broadcasted_iota(jnp.int32, (TQ, LANE), 1)
    rel = lane - (AF_LANE + pl.program_id(0) % A_HEADS)
    ones3 = jnp.where((rel >= 0) & (rel < FOX_TERM_STRIDE * FOX_TERMS) & (rel % FOX_TERM_STRIDE == 0), 1.0, 0.0)
    qext_scr[0] = ones3.astype(BF16)
    qext_scr[1] = (ones3 + _chunk_maskq(row >> CHUNK_SHIFT, lane, 0)).astype(BF16)
    sub_key = lax.broadcasted_iota(jnp.int32, (LANE, LANE), 0)
    sub_qry = lax.broadcasted_iota(jnp.int32, (LANE, LANE), 1)

    def qk_stage(t, bufs):
        q_cat = jnp.concatenate([_tile_rows(q_ref, tab_ref[PAIR_Q, t]), qext_scr[tab_ref[PAIR_DIAG, t]]], axis=1)
        kj = tab_ref[PAIR_K, t]
        k_cat = jnp.concatenate([_tile_rows(k_ref, kj), _tile_rows(kb_ref, kj)], axis=1)
        s = _nt_dot(k_cat, q_cat)
        masked = sub_key - sub_qry > jnp.where(tab_ref[PAIR_DIAG, t] > 0, 0, TK)
        blocks = []
        for r in range(TK // LANE):
            blk = s[r * LANE:(r + 1) * LANE, :]
            mid = jnp.where(masked, NEG_BIG, blk[:, r * LANE:(r + 1) * LANE])
            parts = ([blk[:, :r * LANE]] if r > 0 else []) + [mid]
            parts += [blk[:, (r + 1) * LANE:]] if (r + 1) * LANE < TQ else []
            blocks.append(jnp.concatenate(parts, axis=1))
        _qk_store(jnp.concatenate(blocks, axis=0), bufs)

    _run_pipeline(tab_ref, bufs, qk_stage,
                  lambda t, bufs: _softmax_pv_stage(tab_ref, t, bufs, m_all, vt_scr, acc_all))
    for qi in range(m_all.shape[0] - 1):
        rows = slice(qi * TQ, (qi + 1) * TQ)
        o_ref[rows, :] = (_normalized(acc_all, qi).T * _silu(g_ref[rows, :])).astype(BF16)


def _fox_attn(tab, ob4, kb, of4, batch, seq):
    hs = functools.partial(_head_spec, n_heads=A_HEADS, seq=seq)
    n_tiles = seq // TQ
    return pl.pallas_call(
        _fox_attn_kernel,
        grid=(batch * A_HEADS,),
        in_specs=[_SMEM_SPEC, hs(LANE, SLAB_AQ), hs(LANE, SLAB_AK),
                  pl.BlockSpec((None, seq, LANE), lambda g: (g // A_HEADS, 0, 0)), hs(LANE, SLAB_AV),
                  hs(LANE, FSLAB_AGATE)],
        out_specs=hs(LANE, 0),
        out_shape=jax.ShapeDtypeStruct((A_HEADS, batch, seq, LANE), BF16),
        scratch_shapes=[pltpu.VMEM((n_tiles, ACC_ROWS, TK), BF16), pltpu.VMEM((2, TQ, LANE), BF16)]
        + _state_scratch(n_tiles) + PAIRS_PER_TRIP * _pair_scratch(),
        compiler_params=_cparams(1),
        name="fox_attn",
    )(tab, ob4, ob4, kb, ob4, of4)


def _mla_attn_kernel(tab_ref, q_ref, k_ref, v_ref, g_ref, o_ref, vt_scr, m_all, acc_all, *pair_scr):
    bufs = _pair_bufs(pair_scr)
    _load_vt(v_ref, vt_scr)
    _init_pipeline(m_all, acc_all)
    lane = lax.broadcasted_iota(jnp.int32, (TQ, LANE), 1)

    def qk_stage(t, bufs):
        q = _tile_rows(q_ref, tab_ref[PAIR_Q, t])
        first_off = jnp.where(tab_ref[PAIR_DIAG, t] > 0, LANE, B_ROPE)
        q_hi = jnp.where(lane >= first_off, jnp.zeros((), BF16), q[:, LANE:])
        q_cat = jnp.concatenate([q[:, :LANE], q_hi], axis=1)
        _qk_store(_nt_dot(_tile_rows(k_ref, tab_ref[PAIR_K, t]), q_cat), bufs)

    _run_pipeline(tab_ref, bufs, qk_stage,
                  lambda t, bufs: _softmax_pv_stage(tab_ref, t, bufs, m_all, vt_scr, acc_all))
    for qi in range(m_all.shape[0] - 1):
        rows = slice(qi * TQ, (qi + 1) * TQ)
        o_ref[rows, :] = (_normalized(acc_all, qi).T * _silu(g_ref[rows, :])).astype(BF16)


def _mla_attn(tab, q4, k4, v4, of4, batch, seq):
    hs = functools.partial(_head_spec, n_heads=B_HEADS, seq=seq)
    n_tiles = seq // TQ
    return pl.pallas_call(
        _mla_attn_kernel,
        grid=(batch * B_HEADS,),
        in_specs=[_SMEM_SPEC, hs(2 * LANE, 0), hs(2 * LANE, 0), hs(LANE, 0), hs(LANE, FSLAB_BGATE)],
        out_specs=hs(LANE, 0),
        out_shape=jax.ShapeDtypeStruct((B_HEADS, batch, seq, LANE), BF16),
        scratch_shapes=[pltpu.VMEM((n_tiles, ACC_ROWS, TK), BF16)]
        + _state_scratch(n_tiles) + PAIRS_PER_TRIP * _pair_scratch(),
        compiler_params=_cparams(1),
        name="mla_attn",
    )(tab, q4, k4, v4, of4)


def _diff_attn_kernel(tab_ref, q_ref, k_ref, v_ref, g_ref, bvec_ref, lq1_ref, lk1_ref, lq2_ref, lk2_ref, sg_ref,
                      o_ref, vt_scr, kext_scr, qext_scr, bias_scr, m1_all, acc1_all, m2_all, acc2_all, *pair_scr,
                      lam_init):
    flat = _pair_bufs(pair_scr)
    bufs = [(flat[2 * i], flat[2 * i + 1]) for i in range(PAIRS_PER_TRIP)]
    states = ((m1_all, acc1_all), (m2_all, acc2_all))
    _load_vt(v_ref, vt_scr)
    for m_all, acc_all in states:
        _init_pipeline(m_all, acc_all)
    for i in range(2):
        gen = jnp.broadcast_to(bvec_ref[i], (TK, 2 * TQ))
        bias_scr[i] = pltpu.roll(gen, 0, 1, stride=1, stride_axis=0)[:, :TQ]
    bias_scr[2] = jnp.zeros((TK, TQ), F32)
    row = lax.broadcasted_iota(jnp.int32, (TQ, LANE), 0)
    lane = lax.broadcasted_iota(jnp.int32, (TQ, LANE), 1)
    kext_scr[...] = _chunk_onehot(row >> CHUNK_SHIFT, lane, 0).astype(BF16)
    qext_scr[0] = jnp.zeros((TQ, LANE), BF16)
    qext_scr[1] = _chunk_maskq(row >> CHUNK_SHIFT, lane, 0).astype(BF16)
    lam = (jnp.exp(jnp.sum(lq1_ref[...] * lk1_ref[...], axis=-1, keepdims=True))
           - jnp.exp(jnp.sum(lq2_ref[...] * lk2_ref[...], axis=-1, keepdims=True)) + lam_init)

    def qk_stage(t, bufs2):
        q = _tile_rows(q_ref, tab_ref[PAIR_Q, t])
        qext = qext_scr[tab_ref[PAIR_DIAG, t]]
        zero = jnp.zeros_like(q)
        k_cat = jnp.concatenate([_tile_rows(k_ref, tab_ref[PAIR_K, t]), kext_scr[...]], axis=1)
        bias = bias_scr[tab_ref[PAIR_BIAS, t]]
        for bufs, q_map in zip(bufs2, (jnp.where(lane < C_QK, q, zero), jnp.where(lane >= C_QK, q, zero))):
            _qk_store(_nt_dot(k_cat, jnp.concatenate([q_map, qext], axis=1)) + bias, bufs)

    def smpv_stage(t, bufs2):
        for (m_all, acc_all), bufs in zip(states, bufs2):
            _softmax_pv_stage(tab_ref, t, bufs, m_all, vt_scr, acc_all)

    _run_pipeline(tab_ref, bufs, qk_stage, smpv_stage)
    for qi in range(m1_all.shape[0] - 1):
        rows = slice(qi * TQ, (qi + 1) * TQ)
        o = _normalized(acc1_all, qi) - lam * _normalized(acc2_all, qi)
        o = o * lax.rsqrt(jnp.mean(o * o, axis=0, keepdims=True) + EPS)
        out = o.T * sg_ref[...] * (1.0 - lam_init)
        o_ref[rows, :] = (out * _silu(g_ref[rows, :])).astype(BF16)


def _diff_attn(tab, ob4, of4, bias, lq1, lk1, lq2, lk2, sg, lam_init, batch, seq):
    hs = functools.partial(_head_spec, n_heads=C_HEADS, seq=seq)
    vec = lambda n: pl.BlockSpec((1, n), lambda g: (0, 0))
    n_tiles = seq // TQ
    return pl.pallas_call(
        functools.partial(_diff_attn_kernel, lam_init=lam_init),
        grid=(batch * C_HEADS,),
        in_specs=[_SMEM_SPEC, hs(LANE, SLAB_CQ), hs(LANE, SLAB_CK), hs(LANE, SLAB_CV), hs(LANE, FSLAB_CGATE),
                  pl.BlockSpec((None, 2, 1, 2 * TQ), lambda g: (g % C_HEADS, 0, 0, 0)),
                  vec(C_QK), vec(C_QK), vec(C_QK), vec(C_QK), vec(C_V)],
        out_specs=hs(LANE, 0),
        out_shape=jax.ShapeDtypeStruct((C_HEADS, batch, seq, LANE), BF16),
        scratch_shapes=[pltpu.VMEM((n_tiles, ACC_ROWS, TK), BF16), pltpu.VMEM((TK, LANE), BF16),
                        pltpu.VMEM((2, TQ, LANE), BF16), pltpu.VMEM((3, TK, TQ), F32)]
        + 2 * _state_scratch(n_tiles) + 2 * PAIRS_PER_TRIP * _pair_scratch(),
        compiler_params=_cparams(1),
        name="diff_attn",
    )(tab, ob4, ob4, ob4, of4, bias, lq1, lk1, lq2, lk2, sg)


def _outproj_kernel(a_ref, b_ref, c_ref, w_ref, x_ref, g_ref, o_ref, *, final_norm):
    mixed = jnp.concatenate([a_ref[h] for h in range(A_HEADS)] + [b_ref[h] for h in range(B_HEADS)]
                            + [c_ref[h] for h in range(C_HEADS)], axis=1)
    y = x_ref[...] + jnp.dot(mixed, w_ref[...], preferred_element_type=F32)
    if final_norm:
        y = _rms(y, g_ref[...])
    o_ref[...] = y


def _outproj(mix_a, mix_b, mix_c, w, x2, g, layer, final_norm):
    m = x2.shape[0]
    tm = TM_PROJ
    return pl.pallas_call(
        functools.partial(_outproj_kernel, final_norm=final_norm),
        grid=(m // tm,),
        in_specs=[
            pl.BlockSpec((A_HEADS, tm, LANE), lambda i: (0, i, 0)),
            pl.BlockSpec((B_HEADS, tm, LANE), lambda i: (0, i, 0)),
            pl.BlockSpec((C_HEADS, tm, LANE), lambda i: (0, i, 0)),
            pl.BlockSpec((None, D_MIX, D_MODEL), lambda i: (layer, 0, 0)),
            pl.BlockSpec((tm, D_MODEL), lambda i: (i, 0)),
            pl.BlockSpec((1, D_MODEL), lambda i: (0, 0)),
        ],
        out_specs=pl.BlockSpec((tm, D_MODEL), lambda i: (i, 0)),
        out_shape=jax.ShapeDtypeStruct((m, D_MODEL), F32),
        compiler_params=_cparams(1),
        name="outproj",
    )(mix_a, mix_b, mix_c, w, x2, g)


def kernel(x, norm_g, w_in, b_forget, mla_q_norm_g, w_uq, mla_kv_norm_g, w_ukv, lambda_q1, lambda_k1,
           lambda_q2, lambda_k2, diff_subln_g, rel_bias, w_out, final_norm_g):
    batch, seq, d = x.shape
    assert d == D_MODEL and seq % TQ == 0 and seq % TM_PROJ == 0 and (batch * seq) % TM_IN == 0
    m = batch * seq
    x2 = x.reshape(m, d)
    cs = _proj_col_scale()
    cos, sin = _rope_tables(seq)
    bias_vecs = _t5_bias_vectors(rel_bias)
    tab = jnp.asarray(_pair_table(seq // TQ))
    w_in_r = _prep_w_in(w_in)
    w_out_b = w_out.astype(BF16)
    for l in range(N_LAYERS):
        ob, of = _inproj(x2, norm_g[l].reshape(1, d), w_in_r, cs, l)
        ob4 = ob.reshape(N_BF_SLABS, batch, seq, LANE)
        of4 = of.reshape(N_F_SLABS, batch, seq, LANE)
        kb = _fox_bias(of, b_forget[l], batch, seq)
        mix_a = _fox_attn(tab, ob4, kb, of4, batch, seq)
        qb, kbm, vb = _mla_prep(of, mla_q_norm_g[l].reshape(1, -1), mla_kv_norm_g[l].reshape(1, -1),
                                _prep_w_uq(w_uq[l]), _prep_w_ukv(w_ukv[l]), cos, sin, seq)
        r4 = lambda a: a.reshape(a.shape[0], batch, seq, a.shape[-1])
        mix_b = _mla_attn(tab, r4(qb), r4(kbm), r4(vb), of4, batch, seq)
        lam_init = 0.8 - 0.6 * math.exp(-0.3 * l)
        mix_c = _diff_attn(tab, ob4, of4, bias_vecs, lambda_q1[l].reshape(1, -1), lambda_k1[l].reshape(1, -1),
                           lambda_q2[l].reshape(1, -1), lambda_k2[l].reshape(1, -1),
                           diff_subln_g[l].reshape(1, -1), lam_init, batch, seq)
        x2 = _outproj(mix_a.reshape(A_HEADS, m, LANE), mix_b.reshape(B_HEADS, m, LANE),
                      mix_c.reshape(C_HEADS, m, LANE), w_out_b, x2,
                      final_norm_g.reshape(1, d), l, final_norm=(l == N_LAYERS - 1))
    return x2.reshape(batch, seq, d)
```

```python
import functools
import math

import numpy as np
import jax
import jax.numpy as jnp
from jax import lax
from jax.experimental import pallas as pl
from jax.experimental.pallas import tpu as pltpu

F32 = jnp.float32
BF16 = jnp.bfloat16

D_MODEL = 2048
N_LAYERS = 2
CHUNK = 64
HEAD_DIM = 128
EPS = 1e-6
A_HEADS = 6
A_DIM = A_HEADS * HEAD_DIM
B_HEADS = 6
B_Q_LORA = 512
B_KV_LORA = 256
B_NOPE = 128
B_ROPE = 64
B_V = 128
B_DIM = B_HEADS * B_V
ROPE_THETA = 10000.0
C_HEADS = 4
C_QK = 64
C_V = 2 * C_QK
C_DIM = C_HEADS * C_V
REL_BUCKETS = 32
REL_MAX_DIST = 128
D_MIX = A_DIM + B_DIM + C_DIM

LANE = 128
V7X_VMEM_BYTES = 64 * 1024 * 1024
VMEM_LIMIT = 56 * 1024 * 1024

LOG2E = math.log2(math.e)
NEG_BIG = -1e30

TN = 768
SLABS_PER_TILE = TN // LANE
N_BF_SLABS = 30
N_F_SLABS = 24
N_BF_TILES = N_BF_SLABS // SLABS_PER_TILE
N_F_TILES = N_F_SLABS // SLABS_PER_TILE
N_PROJ = (N_BF_SLABS + N_F_SLABS) * LANE
SLAB_AQ, SLAB_AK, SLAB_AV, SLAB_CQ, SLAB_CK, SLAB_CV = 0, 6, 12, 18, 22, 26
FSLAB_AGATE, FSLAB_BGATE, FSLAB_CGATE, FSLAB_CQ, FSLAB_CKV, FSLAB_KROPE = 0, 6, 12, 16, 20, 22
AF_LANE = 64

TM_PROJ = 512
TM_IN = 1024
TQ = 512
TK = 512

CHUNK_SHIFT = CHUNK.bit_length() - 1
N_TILE_CHUNKS = TK // CHUNK
FOX_TERM_STRIDE = A_HEADS
FOX_TERMS = 3


def _chunk_onehot(row_chunk, lane, lane0):
    return jnp.where(lane - lane0 == row_chunk, 1.0, 0.0)


def _chunk_maskq(row_chunk, lane, lane0):
    c = lane - lane0
    return jnp.where((c >= 0) & (c < N_TILE_CHUNKS) & (row_chunk < c), NEG_BIG, 0.0)


def _nt_dot(a, b):
    return lax.dot_general(a, b, (((1,), (1,)), ((), ())), preferred_element_type=F32)


def _cparams(n_grid):
    return pltpu.CompilerParams(dimension_semantics=("arbitrary",) * n_grid,
                                vmem_limit_bytes=VMEM_LIMIT)


def _w_in_slab_sources():
    o = 0
    seg = {}
    for name, size in (("a_q", A_DIM), ("a_k", A_DIM), ("a_v", A_DIM), ("a_f", A_HEADS), ("a_gate", A_DIM),
                       ("b_cq", B_Q_LORA), ("b_ckv", B_KV_LORA), ("b_krope", B_ROPE), ("b_gate", B_DIM),
                       ("c_q", C_DIM), ("c_k", C_DIM), ("c_v", C_DIM), ("c_gate", C_DIM)):
        seg[name] = (o, size)
        o += size
    slabs = []
    for name in ("a_q", "a_k", "a_v", "c_q", "c_k", "c_v", "a_gate", "b_gate", "c_gate", "b_cq", "b_ckv"):
        start, size = seg[name]
        slabs += [[(start + i, start + i + LANE, 1)] for i in range(0, size, LANE)]
    kr, half = seg["b_krope"][0], B_ROPE // 2
    af = seg["a_f"][0]
    slabs.append([(kr, kr + B_ROPE, 1), (None, LANE - B_ROPE, 0)])
    slabs.append([(kr + half, kr + B_ROPE, -1), (kr, kr + half, 1), (af, af + A_HEADS, 1),
                  (None, LANE - B_ROPE - A_HEADS, 0)])
    assert len(slabs) * LANE == N_PROJ
    return slabs


N_SPECIAL_SLABS = 2


def _w_in_kernel(tab_ref, w_ref, sp_ref, o_ref, *, n_layers):
    j = pl.program_id(0)
    n_regular = pl.num_programs(0) - N_SPECIAL_SLABS
    k_chunks = o_ref.shape[2] // LANE
    rows_per_col = n_layers * k_chunks

    @pl.when(j < n_regular)
    def _():
        for kc in range(k_chunks):
            for l in range(n_layers):
                rows = w_ref[pl.ds(kc * n_layers + l, LANE, stride=rows_per_col), :]
                o_ref[l, :, kc * LANE:(kc + 1) * LANE] = rows.astype(BF16)

    @pl.when(j >= n_regular)
    def _():
        o_ref[...] = sp_ref[:, pl.ds(pl.multiple_of((j - n_regular) * LANE, LANE), LANE), :].astype(BF16)


def _prep_w_in(w):
    n_layers, d, n = w.shape
    k_chunks = d // LANE
    rows_per_col = n_layers * k_chunks
    view = jnp.transpose(w.reshape(n_layers, k_chunks, LANE, n), (3, 1, 0, 2))
    flat = view.reshape(n * rows_per_col, LANE)
    slabs = _w_in_slab_sources()
    n_regular = len(slabs) - N_SPECIAL_SLABS
    assert all(len(p) == 1 and p[0][2] == 1 for p in slabs[:n_regular])
    tab = jnp.asarray([p[0][0] for p in slabs[:n_regular]] + [0] * N_SPECIAL_SLABS, jnp.int32)

    def cols(a, b):
        part = flat[a * rows_per_col:b * rows_per_col].reshape(b - a, k_chunks, n_layers, LANE)
        return jnp.transpose(part, (2, 0, 1, 3)).reshape(n_layers, b - a, d)

    special = []
    for pieces in slabs[n_regular:]:
        for start, stop, sign in pieces:
            if start is None:
                special.append(jnp.zeros((n_layers, stop, d), w.dtype))
            else:
                special.append(cols(start, stop) if sign > 0 else -cols(start, stop))
    special = jnp.concatenate(special, axis=1)
    return pl.pallas_call(
        functools.partial(_w_in_kernel, n_layers=n_layers),
        grid_spec=pltpu.PrefetchScalarGridSpec(
            num_scalar_prefetch=1,
            grid=(len(slabs),),
            in_specs=[pl.BlockSpec((pl.Element(LANE * rows_per_col), pl.Element(LANE)),
                                   lambda j, tab: (tab[j] * rows_per_col, 0)),
                      pl.BlockSpec(special.shape, lambda j, tab: (0, 0, 0))],
            out_specs=pl.BlockSpec((n_layers, LANE, d), lambda j, tab: (0, j, 0)),
        ),
        out_shape=jax.ShapeDtypeStruct((n_layers, N_PROJ, d), BF16),
        compiler_params=_cparams(1),
        name="w_in_reorder",
    )(tab, flat, special)


def _proj_col_scale():
    s = jnp.ones((N_PROJ,), F32)
    s = s.at[SLAB_AQ * LANE:(SLAB_AQ + A_HEADS) * LANE].set(LOG2E / math.sqrt(HEAD_DIM))
    s = s.at[SLAB_CQ * LANE:(SLAB_CQ + C_HEADS) * LANE].set(LOG2E / math.sqrt(C_QK))
    return s.reshape(1, N_PROJ)


def _prep_w_uq(w):
    w = w.reshape(B_Q_LORA, B_HEADS, B_NOPE + B_ROPE)
    nope = w[:, :, :B_NOPE].reshape(B_Q_LORA, B_HEADS * B_NOPE)
    rope = w[:, :, B_NOPE:]
    half = B_ROPE // 2
    rot = jnp.concatenate([-rope[:, :, half:], rope[:, :, :half]], axis=2)
    both = jnp.concatenate([rope, rot], axis=2).reshape(B_Q_LORA, B_HEADS * LANE)
    return jnp.concatenate([nope, both], axis=1).astype(BF16)


def _prep_w_ukv(w):
    w = w.reshape(B_KV_LORA, B_HEADS, B_NOPE + B_V)
    k = w[:, :, :B_NOPE].reshape(B_KV_LORA, B_HEADS * B_NOPE)
    v = w[:, :, B_NOPE:].reshape(B_KV_LORA, B_HEADS * B_V)
    return jnp.concatenate([k, v], axis=1).astype(BF16)


def _rope_tables(seq):
    half = B_ROPE // 2
    inv = ROPE_THETA ** (-jnp.arange(half, dtype=F32) / half)
    ang = jnp.arange(seq).astype(F32)[:, None] * inv[None, :]
    pad = jnp.zeros((seq, LANE - B_ROPE), F32)
    cos = jnp.concatenate([jnp.cos(ang), jnp.cos(ang), pad], axis=1)
    sin = jnp.concatenate([jnp.sin(ang), jnp.sin(ang), pad], axis=1)
    return cos, sin


def _t5_bucket(rel):
    nb = REL_BUCKETS // 2
    max_exact = nb // 2
    ret = (rel > 0).astype(jnp.int32) * nb
    n = jnp.abs(rel)
    nf = jnp.maximum(n, 1).astype(F32)
    large = max_exact + (jnp.log(nf / max_exact) / math.log(REL_MAX_DIST / max_exact)
                         * (nb - max_exact)).astype(jnp.int32)
    large = jnp.minimum(large, nb - 1)
    return ret + jnp.where(n < max_exact, n, large)


def _t5_bias_vectors(rel_bias):
    assert TQ == TK and TQ > REL_MAX_DIST
    x = jnp.arange(2 * TQ)
    x = jnp.where(x < TQ, x, x - 2 * TQ)
    bucket = jnp.stack([_t5_bucket(-x - delta) for delta in (0, TQ)])
    rel = rel_bias - rel_bias[REL_BUCKETS // 2 - 1][None, :]
    t = jnp.zeros((C_HEADS,) + bucket.shape, F32)
    for b in range(REL_BUCKETS):
        t = jnp.where(bucket[None] == b, rel[b][:, None, None], t)
    return (t * LOG2E).reshape(C_HEADS, 2, 1, 2 * TQ)


def _inproj_kernel(x_ref, g_ref, w_ref, cs_ref, ob_ref, of_ref, h_scr):
    j = pl.program_id(1)

    def project(h, out_ref):
        acc = _nt_dot(h, w_ref[...]) * cs_ref[...]
        for s in range(SLABS_PER_TILE):
            out_ref[s] = acc[:, s * LANE:(s + 1) * LANE].astype(out_ref.dtype)

    @pl.when(j == 0)
    def _():
        x = x_ref[...]
        ms = jnp.mean(x * x, axis=-1, keepdims=True)
        h = (x * lax.rsqrt(ms + EPS) * g_ref[...]).astype(BF16)
        h_scr[...] = h
        project(h, ob_ref)

    @pl.when((j > 0) & (j < N_BF_TILES))
    def _():
        project(h_scr[...], ob_ref)

    @pl.when(j >= N_BF_TILES)
    def _():
        project(h_scr[...], of_ref)


def _inproj(x2, g, w, cs, layer):
    m = x2.shape[0]
    tm = TM_IN
    return pl.pallas_call(
        _inproj_kernel,
        grid=(m // tm, N_BF_TILES + N_F_TILES),
        in_specs=[
            pl.BlockSpec((tm, D_MODEL), lambda i, j: (i, 0)),
            pl.BlockSpec((1, D_MODEL), lambda i, j: (0, 0)),
            pl.BlockSpec((None, TN, D_MODEL), lambda i, j: (layer, j, 0)),
            pl.BlockSpec((1, TN), lambda i, j: (0, j)),
        ],
        out_specs=[
            pl.BlockSpec((SLABS_PER_TILE, tm, LANE), lambda i, j: (jnp.minimum(j, N_BF_TILES - 1), i, 0)),
            pl.BlockSpec((SLABS_PER_TILE, tm, LANE), lambda i, j: (jnp.maximum(j - N_BF_TILES, 0), i, 0)),
        ],
        out_shape=[jax.ShapeDtypeStruct((N_BF_SLABS, m, LANE), BF16),
                   jax.ShapeDtypeStruct((N_F_SLABS, m, LANE), F32)],
        scratch_shapes=[pltpu.VMEM((tm, D_MODEL), BF16)],
        compiler_params=_cparams(2),
        name="inproj",
    )(x2, g, w, cs)


def _rms(x, g):
    return x * lax.rsqrt(jnp.mean(x * x, axis=-1, keepdims=True) + EPS) * g


def _mla_prep_kernel(cq_ref, ckv_ref, kr_ref, gq_ref, gkv_ref, wq_ref, wkv_ref, cos_ref, sin_ref,
                     qo_ref, ko_ref, vo_ref, *, n_pos):
    scale = LOG2E / math.sqrt(B_NOPE + B_ROPE)
    cos = cos_ref[...]
    sin = sin_ref[...]
    cq = jnp.concatenate([cq_ref[s] for s in range(B_Q_LORA // LANE)], axis=1)
    q = jnp.dot(_rms(cq, gq_ref[...]).astype(BF16), wq_ref[...], preferred_element_type=F32)
    ckv = jnp.concatenate([ckv_ref[s] for s in range(B_KV_LORA // LANE)], axis=1)
    kv = jnp.dot(_rms(ckv, gkv_ref[...]).astype(BF16), wkv_ref[...], preferred_element_type=F32)
    tm = cos.shape[0]
    pos = (pl.program_id(0) % n_pos) * tm + lax.broadcasted_iota(jnp.int32, (tm, LANE), 0)
    chunk = (pos % TK) >> CHUNK_SHIFT
    lane = lax.broadcasted_iota(jnp.int32, (tm, LANE), 1)
    k_rope = (kr_ref[0] * cos + kr_ref[1] * sin + _chunk_onehot(chunk, lane, B_ROPE)).astype(BF16)
    mask_q = _chunk_maskq(chunk, lane, B_ROPE)
    cos_sin = cos + pltpu.roll(sin, B_ROPE, 1)
    n_h = B_HEADS * LANE
    for h in range(B_HEADS):
        sl = slice(h * LANE, (h + 1) * LANE)
        t = q[:, n_h + h * LANE:n_h + (h + 1) * LANE] * cos_sin
        q_rope = t + pltpu.roll(t, B_ROPE, 1)
        qo_ref[h, :, :LANE] = (q[:, sl] * scale).astype(BF16)
        qo_ref[h, :, LANE:] = jnp.where(lane < B_ROPE, q_rope * scale, mask_q).astype(BF16)
        ko_ref[h, :, :LANE] = kv[:, sl].astype(BF16)
        ko_ref[h, :, LANE:] = k_rope
        vo_ref[h] = kv[:, n_h + h * LANE:n_h + (h + 1) * LANE].astype(BF16)


def _mla_prep(of, gq, gkv, wq, wkv, cos, sin, seq):
    m = of.shape[1]
    tm = TM_PROJ
    n_pos = seq // tm
    nq = B_Q_LORA // LANE
    nkv = B_KV_LORA // LANE
    return pl.pallas_call(
        functools.partial(_mla_prep_kernel, n_pos=n_pos),
        grid=(m // tm,),
        in_specs=[
            pl.BlockSpec((nq, tm, LANE), lambda i: (FSLAB_CQ // nq, i, 0)),
            pl.BlockSpec((nkv, tm, LANE), lambda i: (FSLAB_CKV // nkv, i, 0)),
            pl.BlockSpec((2, tm, LANE), lambda i: (FSLAB_KROPE // 2, i, 0)),
            pl.BlockSpec((1, B_Q_LORA), lambda i: (0, 0)),
            pl.BlockSpec((1, B_KV_LORA), lambda i: (0, 0)),
            pl.BlockSpec(wq.shape, lambda i: (0, 0)),
            pl.BlockSpec(wkv.shape, lambda i: (0, 0)),
            pl.BlockSpec((tm, LANE), lambda i: (i % n_pos, 0)),
            pl.BlockSpec((tm, LANE), lambda i: (i % n_pos, 0)),
        ],
        out_specs=[
            pl.BlockSpec((B_HEADS, tm, 2 * LANE), lambda i: (0, i, 0)),
            pl.BlockSpec((B_HEADS, tm, 2 * LANE), lambda i: (0, i, 0)),
            pl.BlockSpec((B_HEADS, tm, LANE), lambda i: (0, i, 0)),
        ],
        out_shape=[jax.ShapeDtypeStruct((B_HEADS, m, 2 * LANE), BF16),
                   jax.ShapeDtypeStruct((B_HEADS, m, 2 * LANE), BF16),
                   jax.ShapeDtypeStruct((B_HEADS, m, LANE), BF16)],
        compiler_params=_cparams(1),
        name="mla_prep",
    )(of, of, of, gq, gkv, wq, wkv, cos, sin)


CUM_CHUNK = 128


def _fox_bias_kernel(af_ref, bf_ref, kb_ref, cum_scr):
    seq = af_ref.shape[1]
    r = lax.broadcasted_iota(jnp.int32, (CUM_CHUNK, CUM_CHUNK), 0)
    c = lax.broadcasted_iota(jnp.int32, (CUM_CHUNK, CUM_CHUNK), 1)
    tri = jnp.where(r >= c, 1.0, 0.0).astype(BF16)
    bvec = bf_ref[...]

    def body(t, carry):
        r0 = pl.multiple_of(t * CUM_CHUNK, CUM_CHUNK)
        x = af_ref[0, pl.ds(r0, CUM_CHUNK), :] + bvec
        logf = jnp.minimum(x, 0.0) - jnp.log1p(jnp.exp(-jnp.abs(x)))
        cs = carry
        rest = logf
        for _ in range(FOX_TERMS):
            term = rest.astype(BF16)
            rest = rest - term.astype(F32)
            cs = cs + jnp.dot(tri, term, preferred_element_type=F32)
        cum_scr[pl.ds(r0, CUM_CHUNK), :] = cs
        return cs[CUM_CHUNK - 1:CUM_CHUNK, :]

    lax.fori_loop(0, seq // CUM_CHUNK, body, jnp.zeros((1, LANE), F32))
    lane = lax.broadcasted_iota(jnp.int32, (seq, LANE), 1)
    chunk = (lax.broadcasted_iota(jnp.int32, (seq, LANE), 0) % TK) >> CHUNK_SHIFT
    out = _chunk_onehot(chunk, lane, 0)
    rest = cum_scr[...] * (-LOG2E)
    for i in range(FOX_TERMS):
        term = rest.astype(BF16).astype(F32)
        rest = rest - term
        lo = AF_LANE + FOX_TERM_STRIDE * i
        moved = term if i == 0 else pltpu.roll(term, FOX_TERM_STRIDE * i, 1)
        out = jnp.where((lane >= lo) & (lane < lo + A_HEADS), moved, out)
    kb_ref[0] = out.astype(BF16)


def _fox_bias(of, b_forget, batch, seq):
    bvec = jnp.zeros((1, LANE), F32).at[0, AF_LANE:AF_LANE + A_HEADS].set(b_forget)
    of4 = of.reshape(N_F_SLABS, batch, seq, LANE)
    return pl.pallas_call(
        _fox_bias_kernel,
        grid=(batch,),
        in_specs=[
            pl.BlockSpec((None, 1, seq, LANE), lambda b: (N_F_SLABS - 1, b, 0, 0)),
            pl.BlockSpec((1, LANE), lambda b: (0, 0)),
        ],
        out_specs=pl.BlockSpec((1, seq, LANE), lambda b: (b, 0, 0)),
        out_shape=jax.ShapeDtypeStruct((batch, seq, LANE), BF16),
        scratch_shapes=[pltpu.VMEM((seq, LANE), F32)],
        compiler_params=_cparams(1),
        name="fox_bias",
    )(of4, bvec)


PAIR_Q, PAIR_K, PAIR_DIAG, PAIR_BIAS, PAIR_STATE = 0, 1, 2, 3, 4
ACC_ROWS = HEAD_DIM + 16
PAIRS_PER_TRIP = 6
PIPE_DIST = 4


def _pair_table(n_tiles):
    pad = (0, 0, 1, 0, n_tiles)
    rows = []
    for qi in range(n_tiles):
        for kj in range(qi + 1):
            rows.append((qi, kj, int(kj == qi), 0 if kj == qi else (1 if kj == qi - 1 else 2), qi))
    rows += [pad] * (-len(rows) % PAIRS_PER_TRIP)
    rows = rows + [pad] * PIPE_DIST
    return np.asarray(rows, np.int32).T.copy()


def _tile_rows(ref, tile):
    return ref[pl.ds(pl.multiple_of(tile * TK, TK), TK), :]


def _load_vt(v_ref, vt_scr):
    extra = jnp.where(lax.broadcasted_iota(jnp.int32, (ACC_ROWS - HEAD_DIM, TK), 0) == 0, 1.0, 0.0).astype(BF16)
    for j in range(vt_scr.shape[0]):
        vt_scr[j, :HEAD_DIM, :] = v_ref[j * TK:(j + 1) * TK, :].T
        vt_scr[j, HEAD_DIM:, :] = extra


class _PairBufs:
    def __init__(self, s, mb):
        self.s, self.mb = s, mb


def _pair_scratch():
    return [pltpu.VMEM((TK, TQ), F32), pltpu.VMEM((1, TQ), F32)]


def _state_scratch(n_tiles):
    return [pltpu.VMEM((n_tiles + 1, 1, TQ), F32), pltpu.VMEM((n_tiles + 1, ACC_ROWS, TQ), F32)]


def _init_pipeline(m_all, acc_all):
    m_all[...] = jnp.full(m_all.shape, NEG_BIG, F32)
    acc_all[...] = jnp.zeros(acc_all.shape, F32)


def _pair_bufs(pair_scr):
    return [_PairBufs(*pair_scr[2 * i:2 * i + 2]) for i in range(len(pair_scr) // 2)]


def _qk_store(s, bufs):
    bufs.s[...] = s
    bufs.mb[...] = jnp.max(s, axis=0, keepdims=True)


def _softmax_pv_stage(tab_ref, t, bufs, m_all, vt_scr, acc_all):
    st = tab_ref[PAIR_STATE, t]
    m_prev = m_all[st]
    m_new = jnp.maximum(m_prev, bufs.mb[...])
    alpha = jnp.exp2(m_prev - m_new)
    p = jnp.exp2(bufs.s[...] - m_new).astype(BF16)
    m_all[st] = m_new
    acc_all[st] = alpha * acc_all[st] + jnp.dot(vt_scr[tab_ref[PAIR_K, t]], p, preferred_element_type=F32)


def _run_pipeline(tab_ref, bufs, qk_stage, smpv_stage):
    n_pairs = tab_ref.shape[1] - PIPE_DIST
    u, d = PAIRS_PER_TRIP, PIPE_DIST
    assert u > d

    def body(i, carry):
        for j in range(u):
            c = u * i + j
            smpv_stage(c, bufs[j % u])
            qk_stage(c + d, bufs[(d + j) % u])
        return carry

    for c in range(d):
        qk_stage(c, bufs[c % u])
    lax.fori_loop(0, n_pairs // u, body, 0)


def _normalized(acc_all, qi):
    return acc_all[qi, :HEAD_DIM, :] * (1.0 / acc_all[qi, HEAD_DIM:HEAD_DIM + 1, :])


def _silu(g):
    return g / (1.0 + jnp.exp(-g))


def _head_spec(width, slab0, n_heads, seq):
    return pl.BlockSpec((None, None, seq, width), lambda g: (slab0 + g % n_heads, g // n_heads, 0, 0))


_SMEM_SPEC = pl.BlockSpec(memory_space=pltpu.SMEM)


def _fox_attn_kernel(tab_ref, q_ref, k_ref, kb_ref, v_ref, g_ref, o_ref, vt_scr, qext_scr, m_all, acc_all, *pair_scr):
    bufs = _pair_bufs(pair_scr)
    _load_vt(v_ref, vt_scr)
    _init_pipeline(m_all, acc_all)
    row = lax.broadcasted_iota(jnp.int32, (TQ, LANE), 0)
    lane = lax.broadcasted_iota(jnp.int32, (TQ, LANE), 1)
    rel = lane - (AF_LANE + pl.program_id(0) % A_HEADS)
    ones3 = jnp.where((rel >= 0) & (rel < FOX_TERM_STRIDE * FOX_TERMS) & (rel % FOX_TERM_STRIDE == 0), 1.0, 0.0)
    qext_scr[0] = ones3.astype(BF16)
    qext_scr[1] = (ones3 + _chunk_maskq(row >> CHUNK_SHIFT, lane, 0)).astype(BF16)
    sub_key = lax.broadcasted_iota(jnp.int32, (LANE, LANE), 0)
    sub_qry = lax.broadcasted_iota(jnp.int32, (LANE, LANE), 1)

    def qk_stage(t, bufs):
        q_cat = jnp.concatenate([_tile_rows(q_ref, tab_ref[PAIR_Q, t]), qext_scr[tab_ref[PAIR_DIAG, t]]], axis=1)
        kj = tab_ref[PAIR_K, t]
        k_cat = jnp.concatenate([_tile_rows(k_ref, kj), _tile_rows(kb_ref, kj)], axis=1)
        s = _nt_dot(k_cat, q_cat)
        masked = sub_key - sub_qry > jnp.where(tab_ref[PAIR_DIAG, t] > 0, 0, TK)
        blocks = []
        for r in range(TK // LANE):
            blk = s[r * LANE:(r + 1) * LANE, :]
            mid = jnp.where(masked, NEG_BIG, blk[:, r * LANE:(r + 1) * LANE])
            parts = ([blk[:, :r * LANE]] if r > 0 else []) + [mid]
            parts += [blk[:, (r + 1) * LANE:]] if (r + 1) * LANE < TQ else []
            blocks.append(jnp.concatenate(parts, axis=1))
        _qk_store(jnp.concatenate(blocks, axis=0), bufs)

    _run_pipeline(tab_ref, bufs, qk_stage,
                  lambda t, bufs: _softmax_pv_stage(tab_ref, t, bufs, m_all, vt_scr, acc_all))
    for qi in range(m_all.shape[0] - 1):
        rows = slice(qi * TQ, (qi + 1) * TQ)
        o_ref[rows, :] = (_normalized(acc_all, qi).T * _silu(g_ref[rows, :])).astype(BF16)


def _fox_attn(tab, ob4, kb, of4, batch, seq):
    hs = functools.partial(_head_spec, n_heads=A_HEADS, seq=seq)
    n_tiles = seq // TQ
    return pl.pallas_call(
        _fox_attn_kernel,
        grid=(batch * A_HEADS,),
        in_specs=[_SMEM_SPEC, hs(LANE, SLAB_AQ), hs(LANE, SLAB_AK),
                  pl.BlockSpec((None, seq, LANE), lambda g: (g // A_HEADS, 0, 0)), hs(LANE, SLAB_AV),
                  hs(LANE, FSLAB_AGATE)],
        out_specs=hs(LANE, 0),
        out_shape=jax.ShapeDtypeStruct((A_HEADS, batch, seq, LANE), BF16),
        scratch_shapes=[pltpu.VMEM((n_tiles, ACC_ROWS, TK), BF16), pltpu.VMEM((2, TQ, LANE), BF16)]
        + _state_scratch(n_tiles) + PAIRS_PER_TRIP * _pair_scratch(),
        compiler_params=_cparams(1),
        name="fox_attn",
    )(tab, ob4, ob4, kb, ob4, of4)


def _mla_attn_kernel(tab_ref, q_ref, k_ref, v_ref, g_ref, o_ref, vt_scr, m_all, acc_all, *pair_scr):
    bufs = _pair_bufs(pair_scr)
    _load_vt(v_ref, vt_scr)
    _init_pipeline(m_all, acc_all)
    lane = lax.broadcasted_iota(jnp.int32, (TQ, LANE), 1)

    def qk_stage(t, bufs):
        q = _tile_rows(q_ref, tab_ref[PAIR_Q, t])
        first_off = jnp.where(tab_ref[PAIR_DIAG, t] > 0, LANE, B_ROPE)
        q_hi = jnp.where(lane >= first_off, jnp.zeros((), BF16), q[:, LANE:])
        q_cat = jnp.concatenate([q[:, :LANE], q_hi], axis=1)
        _qk_store(_nt_dot(_tile_rows(k_ref, tab_ref[PAIR_K, t]), q_cat), bufs)

    _run_pipeline(tab_ref, bufs, qk_stage,
                  lambda t, bufs: _softmax_pv_stage(tab_ref, t, bufs, m_all, vt_scr, acc_all))
    for qi in range(m_all.shape[0] - 1):
        rows = slice(qi * TQ, (qi + 1) * TQ)
        o_ref[rows, :] = (_normalized(acc_all, qi).T * _silu(g_ref[rows, :])).astype(BF16)


def _mla_attn(tab, q4, k4, v4, of4, batch, seq):
    hs = functools.partial(_head_spec, n_heads=B_HEADS, seq=seq)
    n_tiles = seq // TQ
    return pl.pallas_call(
        _mla_attn_kernel,
        grid=(batch * B_HEADS,),
        in_specs=[_SMEM_SPEC, hs(2 * LANE, 0), hs(2 * LANE, 0), hs(LANE, 0), hs(LANE, FSLAB_BGATE)],
        out_specs=hs(LANE, 0),
        out_shape=jax.ShapeDtypeStruct((B_HEADS, batch, seq, LANE), BF16),
        scratch_shapes=[pltpu.VMEM((n_tiles, ACC_ROWS, TK), BF16)]
        + _state_scratch(n_tiles) + PAIRS_PER_TRIP * _pair_scratch(),
        compiler_params=_cparams(1),
        name="mla_attn",
    )(tab, q4, k4, v4, of4)


def _diff_attn_kernel(tab_ref, q_ref, k_ref, v_ref, g_ref, bvec_ref, lq1_ref, lk1_ref, lq2_ref, lk2_ref, sg_ref,
                      o_ref, vt_scr, kext_scr, qext_scr, bias_scr, m1_all, acc1_all, m2_all, acc2_all, *pair_scr,
                      lam_init):
    flat = _pair_bufs(pair_scr)
    bufs = [(flat[2 * i], flat[2 * i + 1]) for i in range(PAIRS_PER_TRIP)]
    states = ((m1_all, acc1_all), (m2_all, acc2_all))
    _load_vt(v_ref, vt_scr)
    for m_all, acc_all in states:
        _init_pipeline(m_all, acc_all)
    for i in range(2):
        gen = jnp.broadcast_to(bvec_ref[i], (TK, 2 * TQ))
        bias_scr[i] = pltpu.roll(gen, 0, 1, stride=1, stride_axis=0)[:, :TQ]
    bias_scr[2] = jnp.zeros((TK, TQ), F32)
    row = lax.broadcasted_iota(jnp.int32, (TQ, LANE), 0)
    lane = lax.broadcasted_iota(jnp.int32, (TQ, LANE), 1)
    kext_scr[...] = _chunk_onehot(row >> CHUNK_SHIFT, lane, 0).astype(BF16)
    qext_scr[0] = jnp.zeros((TQ, LANE), BF16)
    qext_scr[1] = _chunk_maskq(row >> CHUNK_SHIFT, lane, 0).astype(BF16)
    lam = (jnp.exp(jnp.sum(lq1_ref[...] * lk1_ref[...], axis=-1, keepdims=True))
           - jnp.exp(jnp.sum(lq2_ref[...] * lk2_ref[...], axis=-1, keepdims=True)) + lam_init)

    def qk_stage(t, bufs2):
        q = _tile_rows(q_ref, tab_ref[PAIR_Q, t])
        qext = qext_scr[tab_ref[PAIR_DIAG, t]]
        zero = jnp.zeros_like(q)
        k_cat = jnp.concatenate([_tile_rows(k_ref, tab_ref[PAIR_K, t]), kext_scr[...]], axis=1)
        bias = bias_scr[tab_ref[PAIR_BIAS, t]]
        for bufs, q_map in zip(bufs2, (jnp.where(lane < C_QK, q, zero), jnp.where(lane >= C_QK, q, zero))):
            _qk_store(_nt_dot(k_cat, jnp.concatenate([q_map, qext], axis=1)) + bias, bufs)

    def smpv_stage(t, bufs2):
        for (m_all, acc_all), bufs in zip(states, bufs2):
            _softmax_pv_stage(tab_ref, t, bufs, m_all, vt_scr, acc_all)

    _run_pipeline(tab_ref, bufs, qk_stage, smpv_stage)
    for qi in range(m1_all.shape[0] - 1):
        rows = slice(qi * TQ, (qi + 1) * TQ)
        o = _normalized(acc1_all, qi) - lam * _normalized(acc2_all, qi)
        o = o * lax.rsqrt(jnp.mean(o * o, axis=0, keepdims=True) + EPS)
        out = o.T * sg_ref[...] * (1.0 - lam_init)
        o_ref[rows, :] = (out * _silu(g_ref[rows, :])).astype(BF16)


def _diff_attn(tab, ob4, of4, bias, lq1, lk1, lq2, lk2, sg, lam_init, batch, seq):
    hs = functools.partial(_head_spec, n_heads=C_HEADS, seq=seq)
    vec = lambda n: pl.BlockSpec((1, n), lambda g: (0, 0))
    n_tiles = seq // TQ
    return pl.pallas_call(
        functools.partial(_diff_attn_kernel, lam_init=lam_init),
        grid=(batch * C_HEADS,),
        in_specs=[_SMEM_SPEC, hs(LANE, SLAB_CQ), hs(LANE, SLAB_CK), hs(LANE, SLAB_CV), hs(LANE, FSLAB_CGATE),
                  pl.BlockSpec((None, 2, 1, 2 * TQ), lambda g: (g % C_HEADS, 0, 0, 0)),
                  vec(C_QK), vec(C_QK), vec(C_QK), vec(C_QK), vec(C_V)],
        out_specs=hs(LANE, 0),
        out_shape=jax.ShapeDtypeStruct((C_HEADS, batch, seq, LANE), BF16),
        scratch_shapes=[pltpu.VMEM((n_tiles, ACC_ROWS, TK), BF16), pltpu.VMEM((TK, LANE), BF16),
                        pltpu.VMEM((2, TQ, LANE), BF16), pltpu.VMEM((3, TK, TQ), F32)]
        + 2 * _state_scratch(n_tiles) + 2 * PAIRS_PER_TRIP * _pair_scratch(),
        compiler_params=_cparams(1),
        name="diff_attn",
    )(tab, ob4, ob4, ob4, of4, bias, lq1, lk1, lq2, lk2, sg)


def _outproj_kernel(a_ref, b_ref, c_ref, w_ref, x_ref, g_ref, o_ref, *, final_norm):
    mixed = jnp.concatenate([a_ref[h] for h in range(A_HEADS)] + [b_ref[h] for h in range(B_HEADS)]
                            + [c_ref[h] for h in range(C_HEADS)], axis=1)
    y = x_ref[...] + jnp.dot(mixed, w_ref[...], preferred_element_type=F32)
    if final_norm:
        y = _rms(y, g_ref[...])
    o_ref[...] = y


def _outproj(mix_a, mix_b, mix_c, w, x2, g, layer, final_norm):
    m = x2.shape[0]
    tm = TM_PROJ
    return pl.pallas_call(
        functools.partial(_outproj_kernel, final_norm=final_norm),
        grid=(m // tm,),
        in_specs=[
            pl.BlockSpec((A_HEADS, tm, LANE), lambda i: (0, i, 0)),
            pl.BlockSpec((B_HEADS, tm, LANE), lambda i: (0, i, 0)),
            pl.BlockSpec((C_HEADS, tm, LANE), lambda i: (0, i, 0)),
            pl.BlockSpec((None, D_MIX, D_MODEL), lambda i: (layer, 0, 0)),
            pl.BlockSpec((tm, D_MODEL), lambda i: (i, 0)),
            pl.BlockSpec((1, D_MODEL), lambda i: (0, 0)),
        ],
        out_specs=pl.BlockSpec((tm, D_MODEL), lambda i: (i, 0)),
        out_shape=jax.ShapeDtypeStruct((m, D_MODEL), F32),
        compiler_params=_cparams(1),
        name="outproj",
    )(mix_a, mix_b, mix_c, w, x2, g)


def kernel(x, norm_g, w_in, b_forget, mla_q_norm_g, w_uq, mla_kv_norm_g, w_ukv, lambda_q1, lambda_k1,
           lambda_q2, lambda_k2, diff_subln_g, rel_bias, w_out, final_norm_g):
    batch, seq, d = x.shape
    assert d == D_MODEL and seq % TQ == 0 and seq % TM_PROJ == 0 and (batch * seq) % TM_IN == 0
    m = batch * seq
    x2 = x.reshape(m, d)
    cs = _proj_col_scale()
    cos, sin = _rope_tables(seq)
    bias_vecs = _t5_bias_vectors(rel_bias)
    tab = jnp.asarray(_pair_table(seq // TQ))
    w_in_r = _prep_w_in(w_in)
    w_out_b = w_out.astype(BF16)
    for l in range(N_LAYERS):
        ob, of = _inproj(x2, norm_g[l].reshape(1, d), w_in_r, cs, l)
        ob4 = ob.reshape(N_BF_SLABS, batch, seq, LANE)
        of4 = of.reshape(N_F_SLABS, batch, seq, LANE)
        kb = _fox_bias(of, b_forget[l], batch, seq)
        mix_a = _fox_attn(tab, ob4, kb, of4, batch, seq)
        qb, kbm, vb = _mla_prep(of, mla_q_norm_g[l].reshape(1, -1), mla_kv_norm_g[l].reshape(1, -1),
                                _prep_w_uq(w_uq[l]), _prep_w_ukv(w_ukv[l]), cos, sin, seq)
        r4 = lambda a: a.reshape(a.shape[0], batch, seq, a.shape[-1])
        mix_b = _mla_attn(tab, r4(qb), r4(kbm), r4(vb), of4, batch, seq)
        lam_init = 0.8 - 0.6 * math.exp(-0.3 * l)
        mix_c = _diff_attn(tab, ob4, of4, bias_vecs, lambda_q1[l].reshape(1, -1), lambda_k1[l].reshape(1, -1),
                           lambda_q2[l].reshape(1, -1), lambda_k2[l].reshape(1, -1),
                           diff_subln_g[l].reshape(1, -1), lam_init, batch, seq)
        x2 = _outproj(mix_a.reshape(A_HEADS, m, LANE), mix_b.reshape(B_HEADS, m, LANE),
                      mix_c.reshape(C_HEADS, m, LANE), w_out_b, x2,
                      final_norm_g.reshape(1, d), l, final_norm=(l == N_LAYERS - 1))
    return x2.reshape(batch, seq, d)
```

```python
import functools
import math

import numpy as np
import jax
import jax.numpy as jnp
from jax import lax
from jax.experimental import pallas as pl
from jax.experimental.pallas import tpu as pltpu

F32 = jnp.float32
BF16 = jnp.bfloat16

D_MODEL = 2048
N_LAYERS = 2
CHUNK = 64
HEAD_DIM = 128
EPS = 1e-6
A_HEADS = 6
A_DIM = A_HEADS * HEAD_DIM
B_HEADS = 6
B_Q_LORA = 512
B_KV_LORA = 256
B_NOPE = 128
B_ROPE = 64
B_V = 128
B_DIM = B_HEADS * B_V
ROPE_THETA = 10000.0
C_HEADS = 4
C_QK = 64
C_V = 2 * C_QK
C_DIM = C_HEADS * C_V
REL_BUCKETS = 32
REL_MAX_DIST = 128
D_MIX = A_DIM + B_DIM + C_DIM

LANE = 128
V7X_VMEM_BYTES = 64 * 1024 * 1024
VMEM_LIMIT = 56 * 1024 * 1024

LOG2E = math.log2(math.e)
NEG_BIG = -1e30

TN = 768
SLABS_PER_TILE = TN // LANE
N_BF_SLABS = 30
N_F_SLABS = 24
N_BF_TILES = N_BF_SLABS // SLABS_PER_TILE
N_F_TILES = N_F_SLABS // SLABS_PER_TILE
N_PROJ = (N_BF_SLABS + N_F_SLABS) * LANE
SLAB_AQ, SLAB_AK, SLAB_AV, SLAB_CQ, SLAB_CK, SLAB_CV = 0, 6, 12, 18, 22, 26
FSLAB_AGATE, FSLAB_BGATE, FSLAB_CGATE, FSLAB_CQ, FSLAB_CKV, FSLAB_KROPE = 0, 6, 12, 16, 20, 22
AF_LANE = 64

TM_PROJ = 512
TM_IN = 1024
TQ = 512
TK = 512

CHUNK_SHIFT = CHUNK.bit_length() - 1
N_TILE_CHUNKS = TK // CHUNK
FOX_TERM_STRIDE = A_HEADS
FOX_TERMS = 3


def _chunk_onehot(row_chunk, lane, lane0):
    return jnp.where(lane - lane0 == row_chunk, 1.0, 0.0)


def _chunk_maskq(row_chunk, lane, lane0):
    c = lane - lane0
    return jnp.where((c >= 0) & (c < N_TILE_CHUNKS) & (row_chunk < c), NEG_BIG, 0.0)


def _nt_dot(a, b):
    return lax.dot_general(a, b, (((1,), (1,)), ((), ())), preferred_element_type=F32)


def _cparams(n_grid):
    return pltpu.CompilerParams(dimension_semantics=("arbitrary",) * n_grid,
                                vmem_limit_bytes=VMEM_LIMIT)


def _w_in_slab_sources():
    o = 0
    seg = {}
    for name, size in (("a_q", A_DIM), ("a_k", A_DIM), ("a_v", A_DIM), ("a_f", A_HEADS), ("a_gate", A_DIM),
                       ("b_cq", B_Q_LORA), ("b_ckv", B_KV_LORA), ("b_krope", B_ROPE), ("b_gate", B_DIM),
                       ("c_q", C_DIM), ("c_k", C_DIM), ("c_v", C_DIM), ("c_gate", C_DIM)):
        seg[name] = (o, size)
        o += size
    slabs = []
    for name in ("a_q", "a_k", "a_v", "c_q", "c_k", "c_v", "a_gate", "b_gate", "c_gate", "b_cq", "b_ckv"):
        start, size = seg[name]
        slabs += [[(start + i, start + i + LANE, 1)] for i in range(0, size, LANE)]
    kr, half = seg["b_krope"][0], B_ROPE // 2
    af = seg["a_f"][0]
    slabs.append([(kr, kr + B_ROPE, 1), (None, LANE - B_ROPE, 0)])
    slabs.append([(kr + half, kr + B_ROPE, -1), (kr, kr + half, 1), (af, af + A_HEADS, 1),
                  (None, LANE - B_ROPE - A_HEADS, 0)])
    assert len(slabs) * LANE == N_PROJ
    return slabs


N_SPECIAL_SLABS = 2


def _w_in_kernel(tab_ref, w_ref, sp_ref, o_ref, *, n_layers):
    j = pl.program_id(0)
    n_regular = pl.num_programs(0) - N_SPECIAL_SLABS
    k_chunks = o_ref.shape[2] // LANE
    rows_per_col = n_layers * k_chunks

    @pl.when(j < n_regular)
    def _():
        for kc in range(k_chunks):
            for l in range(n_layers):
                rows = w_ref[pl.ds(kc * n_layers + l, LANE, stride=rows_per_col), :]
                o_ref[l, :, kc * LANE:(kc + 1) * LANE] = rows.astype(BF16)

    @pl.when(j >= n_regular)
    def _():
        o_ref[...] = sp_ref[:, pl.ds(pl.multiple_of((j - n_regular) * LANE, LANE), LANE), :].astype(BF16)


def _prep_w_in(w):
    n_layers, d, n = w.shape
    k_chunks = d // LANE
    rows_per_col = n_layers * k_chunks
    view = jnp.transpose(w.reshape(n_layers, k_chunks, LANE, n), (3, 1, 0, 2))
    flat = view.reshape(n * rows_per_col, LANE)
    slabs = _w_in_slab_sources()
    n_regular = len(slabs) - N_SPECIAL_SLABS
    assert all(len(p) == 1 and p[0][2] == 1 for p in slabs[:n_regular])
    tab = jnp.asarray([p[0][0] for p in slabs[:n_regular]] + [0] * N_SPECIAL_SLABS, jnp.int32)

    def cols(a, b):
        part = flat[a * rows_per_col:b * rows_per_col].reshape(b - a, k_chunks, n_layers, LANE)
        return jnp.transpose(part, (2, 0, 1, 3)).reshape(n_layers, b - a, d)

    special = []
    for pieces in slabs[n_regular:]:
        for start, stop, sign in pieces:
            if start is None:
                special.append(jnp.zeros((n_layers, stop, d), w.dtype))
            else:
                special.append(cols(start, stop) if sign > 0 else -cols(start, stop))
    special = jnp.concatenate(special, axis=1)
    return pl.pallas_call(
        functools.partial(_w_in_kernel, n_layers=n_layers),
        grid_spec=pltpu.PrefetchScalarGridSpec(
            num_scalar_prefetch=1,
            grid=(len(slabs),),
            in_specs=[pl.BlockSpec((pl.Element(LANE * rows_per_col), pl.Element(LANE)),
                                   lambda j, tab: (tab[j] * rows_per_col, 0)),
                      pl.BlockSpec(special.shape, lambda j, tab: (0, 0, 0))],
            out_specs=pl.BlockSpec((n_layers, LANE, d), lambda j, tab: (0, j, 0)),
        ),
        out_shape=jax.ShapeDtypeStruct((n_layers, N_PROJ, d), BF16),
        compiler_params=_cparams(1),
        name="w_in_reorder",
    )(tab, flat, special)


def _proj_col_scale():
    s = jnp.ones((N_PROJ,), F32)
    s = s.at[SLAB_AQ * LANE:(SLAB_AQ + A_HEADS) * LANE].set(LOG2E / math.sqrt(HEAD_DIM))
    s = s.at[SLAB_CQ * LANE:(SLAB_CQ + C_HEADS) * LANE].set(LOG2E / math.sqrt(C_QK))
    return s.reshape(1, N_PROJ)


def _prep_w_uq(w):
    w = w.reshape(B_Q_LORA, B_HEADS, B_NOPE + B_ROPE)
    nope = w[:, :, :B_NOPE].reshape(B_Q_LORA, B_HEADS * B_NOPE)
    rope = w[:, :, B_NOPE:]
    half = B_ROPE // 2
    rot = jnp.concatenate([-rope[:, :, half:], rope[:, :, :half]], axis=2)
    both = jnp.concatenate([rope, rot], axis=2).reshape(B_Q_LORA, B_HEADS * LANE)
    return jnp.concatenate([nope, both], axis=1).astype(BF16)


def _prep_w_ukv(w):
    w = w.reshape(B_KV_LORA, B_HEADS, B_NOPE + B_V)
    k = w[:, :, :B_NOPE].reshape(B_KV_LORA, B_HEADS * B_NOPE)
    v = w[:, :, B_NOPE:].reshape(B_KV_LORA, B_HEADS * B_V)
    return jnp.concatenate([k, v], axis=1).astype(BF16)


def _rope_tables(seq):
    half = B_ROPE // 2
    inv = ROPE_THETA ** (-jnp.arange(half, dtype=F32) / half)
    ang = jnp.arange(seq).astype(F32)[:, None] * inv[None, :]
    pad = jnp.zeros((seq, LANE - B_ROPE), F32)
    cos = jnp.concatenate([jnp.cos(ang), jnp.cos(ang), pad], axis=1)
    sin = jnp.concatenate([jnp.sin(ang), jnp.sin(ang), pad], axis=1)
    return cos, sin


def _t5_bucket(rel):
    nb = REL_BUCKETS // 2
    max_exact = nb // 2
    ret = (rel > 0).astype(jnp.int32) * nb
    n = jnp.abs(rel)
    nf = jnp.maximum(n, 1).astype(F32)
    large = max_exact + (jnp.log(nf / max_exact) / math.log(REL_MAX_DIST / max_exact)
                         * (nb - max_exact)).astype(jnp.int32)
    large = jnp.minimum(large, nb - 1)
    return ret + jnp.where(n < max_exact, n, large)


def _t5_bias_vectors(rel_bias):
    assert TQ == TK and TQ > REL_MAX_DIST
    x = jnp.arange(2 * TQ)
    x = jnp.where(x < TQ, x, x - 2 * TQ)
    bucket = jnp.stack([_t5_bucket(-x - delta) for delta in (0, TQ)])
    rel = rel_bias - rel_bias[REL_BUCKETS // 2 - 1][None, :]
    t = jnp.transpose(rel[bucket], (2, 0, 1))
    return (t * LOG2E).reshape(C_HEADS, 2, 1, 2 * TQ)


def _inproj_kernel(x_ref, g_ref, w_ref, cs_ref, ob_ref, of_ref, h_scr):
    j = pl.program_id(1)

    def project(h, out_ref):
        acc = _nt_dot(h, w_ref[...]) * cs_ref[...]
        for s in range(SLABS_PER_TILE):
            out_ref[s] = acc[:, s * LANE:(s + 1) * LANE].astype(out_ref.dtype)

    @pl.when(j == 0)
    def _():
        x = x_ref[...]
        ms = jnp.mean(x * x, axis=-1, keepdims=True)
        h = (x * lax.rsqrt(ms + EPS) * g_ref[...]).astype(BF16)
        h_scr[...] = h
        project(h, ob_ref)

    @pl.when((j > 0) & (j < N_BF_TILES))
    def _():
        project(h_scr[...], ob_ref)

    @pl.when(j >= N_BF_TILES)
    def _():
        project(h_scr[...], of_ref)


def _inproj(x2, g, w, cs, layer):
    m = x2.shape[0]
    tm = TM_IN
    return pl.pallas_call(
        _inproj_kernel,
        grid=(m // tm, N_BF_TILES + N_F_TILES),
        in_specs=[
            pl.BlockSpec((tm, D_MODEL), lambda i, j: (i, 0)),
            pl.BlockSpec((1, D_MODEL), lambda i, j: (0, 0)),
            pl.BlockSpec((None, TN, D_MODEL), lambda i, j: (layer, j, 0)),
            pl.BlockSpec((1, TN), lambda i, j: (0, j)),
        ],
        out_specs=[
            pl.BlockSpec((SLABS_PER_TILE, tm, LANE), lambda i, j: (jnp.minimum(j, N_BF_TILES - 1), i, 0)),
            pl.BlockSpec((SLABS_PER_TILE, tm, LANE), lambda i, j: (jnp.maximum(j - N_BF_TILES, 0), i, 0)),
        ],
        out_shape=[jax.ShapeDtypeStruct((N_BF_SLABS, m, LANE), BF16),
                   jax.ShapeDtypeStruct((N_F_SLABS, m, LANE), F32)],
        scratch_shapes=[pltpu.VMEM((tm, D_MODEL), BF16)],
        compiler_params=_cparams(2),
        name="inproj",
    )(x2, g, w, cs)


def _rms(x, g):
    return x * lax.rsqrt(jnp.mean(x * x, axis=-1, keepdims=True) + EPS) * g


def _mla_prep_kernel(cq_ref, ckv_ref, kr_ref, gq_ref, gkv_ref, wq_ref, wkv_ref, cos_ref, sin_ref,
                     qo_ref, ko_ref, vo_ref, *, n_pos):
    scale = LOG2E / math.sqrt(B_NOPE + B_ROPE)
    cos = cos_ref[...]
    sin = sin_ref[...]
    cq = jnp.concatenate([cq_ref[s] for s in range(B_Q_LORA // LANE)], axis=1)
    q = jnp.dot(_rms(cq, gq_ref[...]).astype(BF16), wq_ref[...], preferred_element_type=F32)
    ckv = jnp.concatenate([ckv_ref[s] for s in range(B_KV_LORA // LANE)], axis=1)
    kv = jnp.dot(_rms(ckv, gkv_ref[...]).astype(BF16), wkv_ref[...], preferred_element_type=F32)
    tm = cos.shape[0]
    pos = (pl.program_id(0) % n_pos) * tm + lax.broadcasted_iota(jnp.int32, (tm, LANE), 0)
    chunk = (pos % TK) >> CHUNK_SHIFT
    lane = lax.broadcasted_iota(jnp.int32, (tm, LANE), 1)
    k_rope = (kr_ref[0] * cos + kr_ref[1] * sin + _chunk_onehot(chunk, lane, B_ROPE)).astype(BF16)
    mask_q = _chunk_maskq(chunk, lane, B_ROPE)
    cos_sin = cos + pltpu.roll(sin, B_ROPE, 1)
    n_h = B_HEADS * LANE
    for h in range(B_HEADS):
        sl = slice(h * LANE, (h + 1) * LANE)
        t = q[:, n_h + h * LANE:n_h + (h + 1) * LANE] * cos_sin
        q_rope = t + pltpu.roll(t, B_ROPE, 1)
        qo_ref[h, :, :LANE] = (q[:, sl] * scale).astype(BF16)
        qo_ref[h, :, LANE:] = jnp.where(lane < B_ROPE, q_rope * scale, mask_q).astype(BF16)
        ko_ref[h, :, :LANE] = kv[:, sl].astype(BF16)
        ko_ref[h, :, LANE:] = k_rope
        vo_ref[h] = kv[:, n_h + h * LANE:n_h + (h + 1) * LANE].astype(BF16)


def _mla_prep(of, gq, gkv, wq, wkv, cos, sin, seq):
    m = of.shape[1]
    tm = TM_PROJ
    n_pos = seq // tm
    nq = B_Q_LORA // LANE
    nkv = B_KV_LORA // LANE
    return pl.pallas_call(
        functools.partial(_mla_prep_kernel, n_pos=n_pos),
        grid=(m // tm,),
        in_specs=[
            pl.BlockSpec((nq, tm, LANE), lambda i: (FSLAB_CQ // nq, i, 0)),
            pl.BlockSpec((nkv, tm, LANE), lambda i: (FSLAB_CKV // nkv, i, 0)),
            pl.BlockSpec((2, tm, LANE), lambda i: (FSLAB_KROPE // 2, i, 0)),
            pl.BlockSpec((1, B_Q_LORA), lambda i: (0, 0)),
            pl.BlockSpec((1, B_KV_LORA), lambda i: (0, 0)),
            pl.BlockSpec(wq.shape, lambda i: (0, 0)),
            pl.BlockSpec(wkv.shape, lambda i: (0, 0)),
            pl.BlockSpec((tm, LANE), lambda i: (i % n_pos, 0)),
            pl.BlockSpec((tm, LANE), lambda i: (i % n_pos, 0)),
        ],
        out_specs=[
            pl.BlockSpec((B_HEADS, tm, 2 * LANE), lambda i: (0, i, 0)),
            pl.BlockSpec((B_HEADS, tm, 2 * LANE), lambda i: (0, i, 0)),
            pl.BlockSpec((B_HEADS, tm, LANE), lambda i: (0, i, 0)),
        ],
        out_shape=[jax.ShapeDtypeStruct((B_HEADS, m, 2 * LANE), BF16),
                   jax.ShapeDtypeStruct((B_HEADS, m, 2 * LANE), BF16),
                   jax.ShapeDtypeStruct((B_HEADS, m, LANE), BF16)],
        compiler_params=_cparams(1),
        name="mla_prep",
    )(of, of, of, gq, gkv, wq, wkv, cos, sin)


CUM_CHUNK = 128


def _fox_bias_kernel(af_ref, bf_ref, kb_ref, cum_scr):
    seq = af_ref.shape[1]
    r = lax.broadcasted_iota(jnp.int32, (CUM_CHUNK, CUM_CHUNK), 0)
    c = lax.broadcasted_iota(jnp.int32, (CUM_CHUNK, CUM_CHUNK), 1)
    tri = jnp.where(r >= c, 1.0, 0.0).astype(BF16)
    bvec = bf_ref[...]

    def body(t, carry):
        r0 = pl.multiple_of(t * CUM_CHUNK, CUM_CHUNK)
        x = af_ref[0, pl.ds(r0, CUM_CHUNK), :] + bvec
        logf = jnp.minimum(x, 0.0) - jnp.log1p(jnp.exp(-jnp.abs(x)))
        cs = carry
        rest = logf
        for _ in range(FOX_TERMS):
            term = rest.astype(BF16)
            rest = rest - term.astype(F32)
            cs = cs + jnp.dot(tri, term, preferred_element_type=F32)
        cum_scr[pl.ds(r0, CUM_CHUNK), :] = cs
        return cs[CUM_CHUNK - 1:CUM_CHUNK, :]

    lax.fori_loop(0, seq // CUM_CHUNK, body, jnp.zeros((1, LANE), F32))
    lane = lax.broadcasted_iota(jnp.int32, (seq, LANE), 1)
    chunk = (lax.broadcasted_iota(jnp.int32, (seq, LANE), 0) % TK) >> CHUNK_SHIFT
    out = _chunk_onehot(chunk, lane, 0)
    rest = cum_scr[...] * (-LOG2E)
    for i in range(FOX_TERMS):
        term = rest.astype(BF16).astype(F32)
        rest = rest - term
        lo = AF_LANE + FOX_TERM_STRIDE * i
        moved = term if i == 0 else pltpu.roll(term, FOX_TERM_STRIDE * i, 1)
        out = jnp.where((lane >= lo) & (lane < lo + A_HEADS), moved, out)
    kb_ref[0] = out.astype(BF16)


def _fox_bias(of, b_forget, batch, seq):
    bvec = jnp.zeros((1, LANE), F32).at[0, AF_LANE:AF_LANE + A_HEADS].set(b_forget)
    of4 = of.reshape(N_F_SLABS, batch, seq, LANE)
    return pl.pallas_call(
        _fox_bias_kernel,
        grid=(batch,),
        in_specs=[
            pl.BlockSpec((None, 1, seq, LANE), lambda b: (N_F_SLABS - 1, b, 0, 0)),
            pl.BlockSpec((1, LANE), lambda b: (0, 0)),
        ],
        out_specs=pl.BlockSpec((1, seq, LANE), lambda b: (b, 0, 0)),
        out_shape=jax.ShapeDtypeStruct((batch, seq, LANE), BF16),
        scratch_shapes=[pltpu.VMEM((seq, LANE), F32)],
        compiler_params=_cparams(1),
        name="fox_bias",
    )(of4, bvec)


PAIR_Q, PAIR_K, PAIR_DIAG, PAIR_BIAS, PAIR_STATE = 0, 1, 2, 3, 4
ACC_ROWS = HEAD_DIM + 16
PAIRS_PER_TRIP = 6
PIPE_DIST = 4


def _pair_table(n_tiles, pad=True):
    fill = (0, 0, 1, 0, n_tiles)
    rows = []
    for qi in range(n_tiles):
        for kj in range(qi + 1):
            rows.append((qi, kj, int(kj == qi), 0 if kj == qi else (1 if kj == qi - 1 else 2), qi))
    if pad:
        rows += [fill] * (-len(rows) % PAIRS_PER_TRIP + PIPE_DIST)
    return np.asarray(rows, np.int32).T.copy()


class _StaticTable:
    def __init__(self, n_tiles):
        self.a = _pair_table(n_tiles, pad=False)
        self.shape = self.a.shape

    def __getitem__(self, idx):
        return int(self.a[idx])


def _tile_rows(ref, tile):
    return ref[pl.ds(pl.multiple_of(tile * TK, TK), TK), :]


def _load_vt(v_ref, vt_scr):
    extra = jnp.where(lax.broadcasted_iota(jnp.int32, (ACC_ROWS - HEAD_DIM, TK), 0) == 0, 1.0, 0.0).astype(BF16)
    for j in range(vt_scr.shape[0]):
        vt_scr[j, :HEAD_DIM, :] = v_ref[j * TK:(j + 1) * TK, :].T
        vt_scr[j, HEAD_DIM:, :] = extra


class _PairBufs:
    def __init__(self, s, mb):
        self.s, self.mb = s, mb


def _pair_scratch():
    return [pltpu.VMEM((TK, TQ), F32), pltpu.VMEM((1, TQ), F32)]


def _state_scratch(n_tiles):
    return [pltpu.VMEM((n_tiles + 1, 1, TQ), F32), pltpu.VMEM((n_tiles + 1, ACC_ROWS, TQ), F32)]


def _init_pipeline(m_all, acc_all):
    m_all[...] = jnp.full(m_all.shape, NEG_BIG, F32)
    acc_all[...] = jnp.zeros(acc_all.shape, F32)


def _pair_bufs(pair_scr):
    return [_PairBufs(*pair_scr[2 * i:2 * i + 2]) for i in range(len(pair_scr) // 2)]


def _qk_store(s, bufs):
    bufs.s[...] = s
    bufs.mb[...] = jnp.max(s, axis=0, keepdims=True)


def _softmax_pv_stage(tab_ref, t, bufs, m_all, vt_scr, acc_all):
    st = tab_ref[PAIR_STATE, t]
    m_prev = m_all[st]
    m_new = jnp.maximum(m_prev, bufs.mb[...])
    alpha = jnp.exp2(m_prev - m_new)
    p = jnp.exp2(bufs.s[...] - m_new).astype(BF16)
    m_all[st] = m_new
    acc_all[st] = alpha * acc_all[st] + jnp.dot(vt_scr[tab_ref[PAIR_K, t]], p, preferred_element_type=F32)


def _run_pipeline(tab_ref, bufs, qk_stage, smpv_stage):
    n_pairs = tab_ref.shape[1] - PIPE_DIST
    u, d = PAIRS_PER_TRIP, PIPE_DIST
    assert u > d

    def body(i, carry):
        for j in range(u):
            c = u * i + j
            smpv_stage(c, bufs[j % u])
            qk_stage(c + d, bufs[(d + j) % u])
        return carry

    for c in range(d):
        qk_stage(c, bufs[c % u])
    lax.fori_loop(0, n_pairs // u, body, 0)


def _run_pipeline_unrolled(tab, bufs, qk_stage, smpv_stage, finalize):
    u, d = PAIRS_PER_TRIP, PIPE_DIST
    n_pairs = tab.shape[1]
    for c in range(min(d, n_pairs)):
        qk_stage(c, bufs[c % u])
    for c in range(n_pairs):
        smpv_stage(c, bufs[c % u])
        if c + d < n_pairs:
            qk_stage(c + d, bufs[(c + d) % u])
        if tab[PAIR_DIAG, c]:
            finalize(tab[PAIR_Q, c])


def _normalized(acc_all, qi):
    return acc_all[qi, :HEAD_DIM, :] * (1.0 / acc_all[qi, HEAD_DIM:HEAD_DIM + 1, :])


def _silu(g):
    return g / (1.0 + jnp.exp(-g))


def _head_spec(width, slab0, n_heads, seq):
    return pl.BlockSpec((None, None, seq, width), lambda g: (slab0 + g % n_heads, g // n_heads, 0, 0))


_SMEM_SPEC = pl.BlockSpec(memory_space=pltpu.SMEM)


def _fox_attn_kernel(q_ref, k_ref, kb_ref, v_ref, g_ref, o_ref, vt_scr, qext_scr, m_all, acc_all, *pair_scr):
    tab_ref = _StaticTable(m_all.shape[0] - 1)
    bufs = _pair_bufs(pair_scr)
    _load_vt(v_ref, vt_scr)
    _init_pipeline(m_all, acc_all)
    row = lax.broadcasted_iota(jnp.int32, (TQ, LANE), 0)
    lane = lax.broadcasted_iota(jnp.int32, (TQ, LANE), 1)
    rel = lane - (AF_LANE + pl.program_id(0) % A_HEADS)
    ones3 = jnp.where((rel >= 0) & (rel < FOX_TERM_STRIDE * FOX_TERMS) & (rel % FOX_TERM_STRIDE == 0), 1.0, 0.0)
    qext_scr[0] = ones3.astype(BF16)
    qext_scr[1] = (ones3 + _chunk_maskq(row >> CHUNK_SHIFT, lane, 0)).astype(BF16)
    sub_key = lax.broadcasted_iota(jnp.int32, (LANE, LANE), 0)
    sub_qry = lax.broadcasted_iota(jnp.int32, (LANE, LANE), 1)

    def qk_stage(t, bufs):
        q_cat = jnp.concatenate([_tile_rows(q_ref, tab_ref[PAIR_Q, t]), qext_scr[tab_ref[PAIR_DIAG, t]]], axis=1)
        kj = tab_ref[PAIR_K, t]
        k_cat = jnp.concatenate([_tile_rows(k_ref, kj), _tile_rows(kb_ref, kj)], axis=1)
        s = _nt_dot(k_cat, q_cat)
        if tab_ref[PAIR_DIAG, t]:
            blocks = []
            for r in range(TK // LANE):
                blk = s[r * LANE:(r + 1) * LANE, :]
                mid = jnp.where(sub_key > sub_qry, NEG_BIG, blk[:, r * LANE:(r + 1) * LANE])
                parts = ([blk[:, :r * LANE]] if r > 0 else []) + [mid]
                parts += [blk[:, (r + 1) * LANE:]] if (r + 1) * LANE < TQ else []
                blocks.append(jnp.concatenate(parts, axis=1))
            s = jnp.concatenate(blocks, axis=0)
        _qk_store(s, bufs)

    def finalize(qi):
        rows = slice(qi * TQ, (qi + 1) * TQ)
        o_ref[rows, :] = (_normalized(acc_all, qi).T * _silu(g_ref[rows, :])).astype(BF16)

    _run_pipeline_unrolled(tab_ref, bufs, qk_stage,
                           lambda t, bufs: _softmax_pv_stage(tab_ref, t, bufs, m_all, vt_scr, acc_all), finalize)


def _fox_attn(ob4, kb, of4, batch, seq):
    hs = functools.partial(_head_spec, n_heads=A_HEADS, seq=seq)
    n_tiles = seq // TQ
    return pl.pallas_call(
        _fox_attn_kernel,
        grid=(batch * A_HEADS,),
        in_specs=[hs(LANE, SLAB_AQ), hs(LANE, SLAB_AK),
                  pl.BlockSpec((None, seq, LANE), lambda g: (g // A_HEADS, 0, 0)), hs(LANE, SLAB_AV),
                  hs(LANE, FSLAB_AGATE)],
        out_specs=hs(LANE, 0),
        out_shape=jax.ShapeDtypeStruct((A_HEADS, batch, seq, LANE), BF16),
        scratch_shapes=[pltpu.VMEM((n_tiles, ACC_ROWS, TK), BF16), pltpu.VMEM((2, TQ, LANE), BF16)]
        + _state_scratch(n_tiles) + PAIRS_PER_TRIP * _pair_scratch(),
        compiler_params=_cparams(1),
        name="fox_attn",
    )(ob4, ob4, kb, ob4, of4)


def _mla_attn_kernel(q_ref, k_ref, v_ref, g_ref, o_ref, vt_scr, m_all, acc_all, *pair_scr):
    tab_ref = _StaticTable(m_all.shape[0] - 1)
    bufs = _pair_bufs(pair_scr)
    _load_vt(v_ref, vt_scr)
    _init_pipeline(m_all, acc_all)
    lane = lax.broadcasted_iota(jnp.int32, (TQ, LANE), 1)

    def qk_stage(t, bufs):
        q = _tile_rows(q_ref, tab_ref[PAIR_Q, t])
        first_off = LANE if tab_ref[PAIR_DIAG, t] else B_ROPE
        q_hi = jnp.where(lane >= first_off, jnp.zeros((), BF16), q[:, LANE:])
        q_cat = jnp.concatenate([q[:, :LANE], q_hi], axis=1)
        _qk_store(_nt_dot(_tile_rows(k_ref, tab_ref[PAIR_K, t]), q_cat), bufs)

    def finalize(qi):
        rows = slice(qi * TQ, (qi + 1) * TQ)
        o_ref[rows, :] = (_normalized(acc_all, qi).T * _silu(g_ref[rows, :])).astype(BF16)

    _run_pipeline_unrolled(tab_ref, bufs, qk_stage,
                           lambda t, bufs: _softmax_pv_stage(tab_ref, t, bufs, m_all, vt_scr, acc_all), finalize)


def _mla_attn(q4, k4, v4, of4, batch, seq):
    hs = functools.partial(_head_spec, n_heads=B_HEADS, seq=seq)
    n_tiles = seq // TQ
    return pl.pallas_call(
        _mla_attn_kernel,
        grid=(batch * B_HEADS,),
        in_specs=[hs(2 * LANE, 0), hs(2 * LANE, 0), hs(LANE, 0), hs(LANE, FSLAB_BGATE)],
        out_specs=hs(LANE, 0),
        out_shape=jax.ShapeDtypeStruct((B_HEADS, batch, seq, LANE), BF16),
        scratch_shapes=[pltpu.VMEM((n_tiles, ACC_ROWS, TK), BF16)]
        + _state_scratch(n_tiles) + PAIRS_PER_TRIP * _pair_scratch(),
        compiler_params=_cparams(1),
        name="mla_attn",
    )(q4, k4, v4, of4)


def _diff_attn_kernel(tab_ref, q_ref, k_ref, v_ref, g_ref, bvec_ref, lq1_ref, lk1_ref, lq2_ref, lk2_ref, sg_ref,
                      o_ref, vt_scr, kext_scr, qext_scr, bias_scr, m1_all, acc1_all, m2_all, acc2_all, *pair_scr,
                      lam_init):
    flat = _pair_bufs(pair_scr)
    bufs = [(flat[2 * i], flat[2 * i + 1]) for i in range(PAIRS_PER_TRIP)]
    states = ((m1_all, acc1_all), (m2_all, acc2_all))
    _load_vt(v_ref, vt_scr)
    for m_all, acc_all in states:
        _init_pipeline(m_all, acc_all)
    for i in range(2):
        gen = jnp.broadcast_to(bvec_ref[i], (TK, 2 * TQ))
        bias_scr[i] = pltpu.roll(gen, 0, 1, stride=1, stride_axis=0)[:, :TQ]
    bias_scr[2] = jnp.zeros((TK, TQ), F32)
    row = lax.broadcasted_iota(jnp.int32, (TQ, LANE), 0)
    lane = lax.broadcasted_iota(jnp.int32, (TQ, LANE), 1)
    kext_scr[...] = _chunk_onehot(row >> CHUNK_SHIFT, lane, 0).astype(BF16)
    qext_scr[0] = jnp.zeros((TQ, LANE), BF16)
    qext_scr[1] = _chunk_maskq(row >> CHUNK_SHIFT, lane, 0).astype(BF16)
    lam = (jnp.exp(jnp.sum(lq1_ref[...] * lk1_ref[...], axis=-1, keepdims=True))
           - jnp.exp(jnp.sum(lq2_ref[...] * lk2_ref[...], axis=-1, keepdims=True)) + lam_init)

    def qk_stage(t, bufs2):
        q = _tile_rows(q_ref, tab_ref[PAIR_Q, t])
        qext = qext_scr[tab_ref[PAIR_DIAG, t]]
        zero = jnp.zeros_like(q)
        k_cat = jnp.concatenate([_tile_rows(k_ref, tab_ref[PAIR_K, t]), kext_scr[...]], axis=1)
        bias = bias_scr[tab_ref[PAIR_BIAS, t]]
        for bufs, q_map in zip(bufs2, (jnp.where(lane < C_QK, q, zero), jnp.where(lane >= C_QK, q, zero))):
            _qk_store(_nt_dot(k_cat, jnp.concatenate([q_map, qext], axis=1)) + bias, bufs)

    def smpv_stage(t, bufs2):
        for (m_all, acc_all), bufs in zip(states, bufs2):
            _softmax_pv_stage(tab_ref, t, bufs, m_all, vt_scr, acc_all)

    _run_pipeline(tab_ref, bufs, qk_stage, smpv_stage)
    for qi in range(m1_all.shape[0] - 1):
        rows = slice(qi * TQ, (qi + 1) * TQ)
        o = _normalized(acc1_all, qi) - lam * _normalized(acc2_all, qi)
        o = o * lax.rsqrt(jnp.mean(o * o, axis=0, keepdims=True) + EPS)
        out = o.T * sg_ref[...] * (1.0 - lam_init)
        o_ref[rows, :] = (out * _silu(g_ref[rows, :])).astype(BF16)


def _diff_attn(tab, ob4, of4, bias, lq1, lk1, lq2, lk2, sg, lam_init, batch, seq):
    hs = functools.partial(_head_spec, n_heads=C_HEADS, seq=seq)
    vec = lambda n: pl.BlockSpec((1, n), lambda g: (0, 0))
    n_tiles = seq // TQ
    return pl.pallas_call(
        functools.partial(_diff_attn_kernel, lam_init=lam_init),
        grid=(batch * C_HEADS,),
        in_specs=[_SMEM_SPEC, hs(LANE, SLAB_CQ), hs(LANE, SLAB_CK), hs(LANE, SLAB_CV), hs(LANE, FSLAB_CGATE),
                  pl.BlockSpec((None, 2, 1, 2 * TQ), lambda g: (g % C_HEADS, 0, 0, 0)),
                  vec(C_QK), vec(C_QK), vec(C_QK), vec(C_QK), vec(C_V)],
        out_specs=hs(LANE, 0),
        out_shape=jax.ShapeDtypeStruct((C_HEADS, batch, seq, LANE), BF16),
        scratch_shapes=[pltpu.VMEM((n_tiles, ACC_ROWS, TK), BF16), pltpu.VMEM((TK, LANE), BF16),
                        pltpu.VMEM((2, TQ, LANE), BF16), pltpu.VMEM((3, TK, TQ), F32)]
        + 2 * _state_scratch(n_tiles) + 2 * PAIRS_PER_TRIP * _pair_scratch(),
        compiler_params=_cparams(1),
        name="diff_attn",
    )(tab, ob4, ob4, ob4, of4, bias, lq1, lk1, lq2, lk2, sg)


def _outproj_kernel(a_ref, b_ref, c_ref, w_ref, x_ref, g_ref, o_ref, *, final_norm):
    mixed = jnp.concatenate([a_ref[h] for h in range(A_HEADS)] + [b_ref[h] for h in range(B_HEADS)]
                            + [c_ref[h] for h in range(C_HEADS)], axis=1)
    y = x_ref[...] + jnp.dot(mixed, w_ref[...], preferred_element_type=F32)
    if final_norm:
        y = _rms(y, g_ref[...])
    o_ref[...] = y


def _outproj(mix_a, mix_b, mix_c, w, x2, g, layer, final_norm):
    m = x2.shape[0]
    tm = TM_PROJ
    return pl.pallas_call(
        functools.partial(_outproj_kernel, final_norm=final_norm),
        grid=(m // tm,),
        in_specs=[
            pl.BlockSpec((A_HEADS, tm, LANE), lambda i: (0, i, 0)),
            pl.BlockSpec((B_HEADS, tm, LANE), lambda i: (0, i, 0)),
            pl.BlockSpec((C_HEADS, tm, LANE), lambda i: (0, i, 0)),
            pl.BlockSpec((None, D_MIX, D_MODEL), lambda i: (layer, 0, 0)),
            pl.BlockSpec((tm, D_MODEL), lambda i: (i, 0)),
            pl.BlockSpec((1, D_MODEL), lambda i: (0, 0)),
        ],
        out_specs=pl.BlockSpec((tm, D_MODEL), lambda i: (i, 0)),
        out_shape=jax.ShapeDtypeStruct((m, D_MODEL), F32),
        compiler_params=_cparams(1),
        name="outproj",
    )(mix_a, mix_b, mix_c, w, x2, g)


def kernel(x, norm_g, w_in, b_forget, mla_q_norm_g, w_uq, mla_kv_norm_g, w_ukv, lambda_q1, lambda_k1,
           lambda_q2, lambda_k2, diff_subln_g, rel_bias, w_out, final_norm_g):
    batch, seq, d = x.shape
    assert d == D_MODEL and seq % TQ == 0 and seq % TM_PROJ == 0 and (batch * seq) % TM_IN == 0
    m = batch * seq
    x2 = x.reshape(m, d)
    cs = _proj_col_scale()
    cos, sin = _rope_tables(seq)
    bias_vecs = _t5_bias_vectors(rel_bias)
    tab = jnp.asarray(_pair_table(seq // TQ))
    w_in_r = _prep_w_in(w_in)
    w_out_b = w_out.astype(BF16)
    for l in range(N_LAYERS):
        ob, of = _inproj(x2, norm_g[l].reshape(1, d), w_in_r, cs, l)
        ob4 = ob.reshape(N_BF_SLABS, batch, seq, LANE)
        of4 = of.reshape(N_F_SLABS, batch, seq, LANE)
        kb = _fox_bias(of, b_forget[l], batch, seq)
        mix_a = _fox_attn(ob4, kb, of4, batch, seq)
        qb, kbm, vb = _mla_prep(of, mla_q_norm_g[l].reshape(1, -1), mla_kv_norm_g[l].reshape(1, -1),
                                _prep_w_uq(w_uq[l]), _prep_w_ukv(w_ukv[l]), cos, sin, seq)
        r4 = lambda a: a.reshape(a.shape[0], batch, seq, a.shape[-1])
        mix_b = _mla_attn(r4(qb), r4(kbm), r4(vb), of4, batch, seq)
        lam_init = 0.8 - 0.6 * math.exp(-0.3 * l)
        mix_c = _diff_attn(tab, ob4, of4, bias_vecs, lambda_q1[l].reshape(1, -1), lambda_k1[l].reshape(1, -1),
                           lambda_q2[l].reshape(1, -1), lambda_k2[l].reshape(1, -1),
                           diff_subln_g[l].reshape(1, -1), lam_init, batch, seq)
        x2 = _outproj(mix_a.reshape(A_HEADS, m, LANE), mix_b.reshape(B_HEADS, m, LANE),
                      mix_c.reshape(C_HEADS, m, LANE), w_out_b, x2,
                      final_norm_g.reshape(1, d), l, final_norm=(l == N_LAYERS - 1))
    return x2.reshape(batch, seq, d)
```

```python
import functools
import math

import numpy as np
import jax
import jax.numpy as jnp
from jax import lax
from jax.experimental import pallas as pl
from jax.experimental.pallas import tpu as pltpu

F32 = jnp.float32
BF16 = jnp.bfloat16

D_MODEL = 2048
N_LAYERS = 2
CHUNK = 64
HEAD_DIM = 128
EPS = 1e-6
A_HEADS = 6
A_DIM = A_HEADS * HEAD_DIM
B_HEADS = 6
B_Q_LORA = 512
B_KV_LORA = 256
B_NOPE = 128
B_ROPE = 64
B_V = 128
B_DIM = B_HEADS * B_V
ROPE_THETA = 10000.0
C_HEADS = 4
C_QK = 64
C_V = 2 * C_QK
C_DIM = C_HEADS * C_V
REL_BUCKETS = 32
REL_MAX_DIST = 128
D_MIX = A_DIM + B_DIM + C_DIM

LANE = 128
V7X_VMEM_BYTES = 64 * 1024 * 1024
VMEM_LIMIT = 56 * 1024 * 1024

LOG2E = math.log2(math.e)
NEG_BIG = -1e30

TN = 768
SLABS_PER_TILE = TN // LANE
N_BF_SLABS = 30
N_F_SLABS = 24
N_BF_TILES = N_BF_SLABS // SLABS_PER_TILE
N_F_TILES = N_F_SLABS // SLABS_PER_TILE
N_PROJ = (N_BF_SLABS + N_F_SLABS) * LANE
SLAB_AQ, SLAB_AK, SLAB_AV, SLAB_CQ, SLAB_CK, SLAB_CV = 0, 6, 12, 18, 22, 26
FSLAB_AGATE, FSLAB_BGATE, FSLAB_CGATE, FSLAB_CQ, FSLAB_CKV, FSLAB_KROPE = 0, 6, 12, 16, 20, 22
AF_LANE = 64

TM_PROJ = 512
TM_IN = 1024
TQ = 512
TK = 512

CHUNK_SHIFT = CHUNK.bit_length() - 1
N_TILE_CHUNKS = TK // CHUNK
FOX_TERM_STRIDE = A_HEADS
FOX_TERMS = 3


def _chunk_onehot(row_chunk, lane, lane0):
    return jnp.where(lane - lane0 == row_chunk, 1.0, 0.0)


def _chunk_maskq(row_chunk, lane, lane0):
    c = lane - lane0
    return jnp.where((c >= 0) & (c < N_TILE_CHUNKS) & (row_chunk < c), NEG_BIG, 0.0)


def _nt_dot(a, b):
    return lax.dot_general(a, b, (((1,), (1,)), ((), ())), preferred_element_type=F32)


def _cparams(n_grid):
    return pltpu.CompilerParams(dimension_semantics=("arbitrary",) * n_grid,
                                vmem_limit_bytes=VMEM_LIMIT)


def _w_in_slab_sources():
    o = 0
    seg = {}
    for name, size in (("a_q", A_DIM), ("a_k", A_DIM), ("a_v", A_DIM), ("a_f", A_HEADS), ("a_gate", A_DIM),
                       ("b_cq", B_Q_LORA), ("b_ckv", B_KV_LORA), ("b_krope", B_ROPE), ("b_gate", B_DIM),
                       ("c_q", C_DIM), ("c_k", C_DIM), ("c_v", C_DIM), ("c_gate", C_DIM)):
        seg[name] = (o, size)
        o += size
    slabs = []
    for name in ("a_q", "a_k", "a_v", "c_q", "c_k", "c_v", "a_gate", "b_gate", "c_gate", "b_cq", "b_ckv"):
        start, size = seg[name]
        slabs += [[(start + i, start + i + LANE, 1)] for i in range(0, size, LANE)]
    kr, half = seg["b_krope"][0], B_ROPE // 2
    af = seg["a_f"][0]
    slabs.append([(kr, kr + B_ROPE, 1), (None, LANE - B_ROPE, 0)])
    slabs.append([(kr + half, kr + B_ROPE, -1), (kr, kr + half, 1), (af, af + A_HEADS, 1),
                  (None, LANE - B_ROPE - A_HEADS, 0)])
    assert len(slabs) * LANE == N_PROJ
    return slabs


N_SPECIAL_SLABS = 2


def _w_in_kernel(tab_ref, w_ref, sp_ref, o_ref, *, n_layers):
    j = pl.program_id(0)
    n_regular = pl.num_programs(0) - N_SPECIAL_SLABS
    k_chunks = o_ref.shape[2] // LANE
    rows_per_col = n_layers * k_chunks

    @pl.when(j < n_regular)
    def _():
        for kc in range(k_chunks):
            for l in range(n_layers):
                rows = w_ref[pl.ds(kc * n_layers + l, LANE, stride=rows_per_col), :]
                o_ref[l, :, kc * LANE:(kc + 1) * LANE] = rows.astype(BF16)

    @pl.when(j >= n_regular)
    def _():
        o_ref[...] = sp_ref[:, pl.ds(pl.multiple_of((j - n_regular) * LANE, LANE), LANE), :].astype(BF16)


def _prep_w_in(w):
    n_layers, d, n = w.shape
    k_chunks = d // LANE
    rows_per_col = n_layers * k_chunks
    view = jnp.transpose(w.reshape(n_layers, k_chunks, LANE, n), (3, 1, 0, 2))
    flat = view.reshape(n * rows_per_col, LANE)
    slabs = _w_in_slab_sources()
    n_regular = len(slabs) - N_SPECIAL_SLABS
    assert all(len(p) == 1 and p[0][2] == 1 for p in slabs[:n_regular])
    tab = jnp.asarray([p[0][0] for p in slabs[:n_regular]] + [0] * N_SPECIAL_SLABS, jnp.int32)

    def cols(a, b):
        part = flat[a * rows_per_col:b * rows_per_col].reshape(b - a, k_chunks, n_layers, LANE)
        return jnp.transpose(part, (2, 0, 1, 3)).reshape(n_layers, b - a, d)

    special = []
    for pieces in slabs[n_regular:]:
        for start, stop, sign in pieces:
            if start is None:
                special.append(jnp.zeros((n_layers, stop, d), w.dtype))
            else:
                special.append(cols(start, stop) if sign > 0 else -cols(start, stop))
    special = jnp.concatenate(special, axis=1)
    return pl.pallas_call(
        functools.partial(_w_in_kernel, n_layers=n_layers),
        grid_spec=pltpu.PrefetchScalarGridSpec(
            num_scalar_prefetch=1,
            grid=(len(slabs),),
            in_specs=[pl.BlockSpec((pl.Element(LANE * rows_per_col), pl.Element(LANE)),
                                   lambda j, tab: (tab[j] * rows_per_col, 0)),
                      pl.BlockSpec(special.shape, lambda j, tab: (0, 0, 0))],
            out_specs=pl.BlockSpec((n_layers, LANE, d), lambda j, tab: (0, j, 0)),
        ),
        out_shape=jax.ShapeDtypeStruct((n_layers, N_PROJ, d), BF16),
        compiler_params=_cparams(1),
        name="w_in_reorder",
    )(tab, flat, special)


def _proj_col_scale():
    s = jnp.ones((N_PROJ,), F32)
    s = s.at[SLAB_AQ * LANE:(SLAB_AQ + A_HEADS) * LANE].set(LOG2E / math.sqrt(HEAD_DIM))
    s = s.at[SLAB_CQ * LANE:(SLAB_CQ + C_HEADS) * LANE].set(LOG2E / math.sqrt(C_QK))
    return s.reshape(1, N_PROJ)


def _prep_w_uq(w):
    w = w.reshape(B_Q_LORA, B_HEADS, B_NOPE + B_ROPE)
    nope = w[:, :, :B_NOPE].reshape(B_Q_LORA, B_HEADS * B_NOPE)
    rope = w[:, :, B_NOPE:]
    half = B_ROPE // 2
    rot = jnp.concatenate([-rope[:, :, half:], rope[:, :, :half]], axis=2)
    both = jnp.concatenate([rope, rot], axis=2).reshape(B_Q_LORA, B_HEADS * LANE)
    return jnp.concatenate([nope, both], axis=1).astype(BF16)


def _prep_w_ukv(w):
    w = w.reshape(B_KV_LORA, B_HEADS, B_NOPE + B_V)
    k = w[:, :, :B_NOPE].reshape(B_KV_LORA, B_HEADS * B_NOPE)
    v = w[:, :, B_NOPE:].reshape(B_KV_LORA, B_HEADS * B_V)
    return jnp.concatenate([k, v], axis=1).astype(BF16)


def _rope_tables(seq):
    half = B_ROPE // 2
    inv = ROPE_THETA ** (-jnp.arange(half, dtype=F32) / half)
    ang = jnp.arange(seq).astype(F32)[:, None] * inv[None, :]
    pad = jnp.zeros((seq, LANE - B_ROPE), F32)
    cos = jnp.concatenate([jnp.cos(ang), jnp.cos(ang), pad], axis=1)
    sin = jnp.concatenate([jnp.sin(ang), jnp.sin(ang), pad], axis=1)
    return cos, sin


def _t5_bucket(rel):
    nb = REL_BUCKETS // 2
    max_exact = nb // 2
    ret = (rel > 0).astype(jnp.int32) * nb
    n = jnp.abs(rel)
    nf = jnp.maximum(n, 1).astype(F32)
    large = max_exact + (jnp.log(nf / max_exact) / math.log(REL_MAX_DIST / max_exact)
                         * (nb - max_exact)).astype(jnp.int32)
    large = jnp.minimum(large, nb - 1)
    return ret + jnp.where(n < max_exact, n, large)


def _t5_bias_vectors(rel_bias):
    assert TQ == TK and TQ > REL_MAX_DIST
    x = jnp.arange(2 * TQ)
    x = jnp.where(x < TQ, x, x - 2 * TQ)
    bucket = jnp.stack([_t5_bucket(-x - delta) for delta in (0, TQ)])
    rel = rel_bias - rel_bias[REL_BUCKETS // 2 - 1][None, :]
    t = jnp.transpose(rel[bucket], (2, 0, 1))
    return (t * LOG2E).reshape(C_HEADS, 2, 1, 2 * TQ)


def _inproj_kernel(x_ref, g_ref, w_ref, cs_ref, ob_ref, of_ref, h_scr):
    j = pl.program_id(1)

    def project(h, out_ref):
        acc = _nt_dot(h, w_ref[...]) * cs_ref[...]
        for s in range(SLABS_PER_TILE):
            out_ref[s] = acc[:, s * LANE:(s + 1) * LANE].astype(out_ref.dtype)

    @pl.when(j == 0)
    def _():
        x = x_ref[...]
        ms = jnp.mean(x * x, axis=-1, keepdims=True)
        h = (x * lax.rsqrt(ms + EPS) * g_ref[...]).astype(BF16)
        h_scr[...] = h
        project(h, ob_ref)

    @pl.when((j > 0) & (j < N_BF_TILES))
    def _():
        project(h_scr[...], ob_ref)

    @pl.when(j >= N_BF_TILES)
    def _():
        project(h_scr[...], of_ref)


def _inproj(x2, g, w, cs, layer):
    m = x2.shape[0]
    tm = TM_IN
    return pl.pallas_call(
        _inproj_kernel,
        grid=(m // tm, N_BF_TILES + N_F_TILES),
        in_specs=[
            pl.BlockSpec((tm, D_MODEL), lambda i, j: (i, 0)),
            pl.BlockSpec((1, D_MODEL), lambda i, j: (0, 0)),
            pl.BlockSpec((None, TN, D_MODEL), lambda i, j: (layer, j, 0)),
            pl.BlockSpec((1, TN), lambda i, j: (0, j)),
        ],
        out_specs=[
            pl.BlockSpec((SLABS_PER_TILE, tm, LANE), lambda i, j: (jnp.minimum(j, N_BF_TILES - 1), i, 0)),
            pl.BlockSpec((SLABS_PER_TILE, tm, LANE), lambda i, j: (jnp.maximum(j - N_BF_TILES, 0), i, 0)),
        ],
        out_shape=[jax.ShapeDtypeStruct((N_BF_SLABS, m, LANE), BF16),
                   jax.ShapeDtypeStruct((N_F_SLABS, m, LANE), F32)],
        scratch_shapes=[pltpu.VMEM((tm, D_MODEL), BF16)],
        compiler_params=_cparams(2),
        name="inproj",
    )(x2, g, w, cs)


def _rms(x, g):
    return x * lax.rsqrt(jnp.mean(x * x, axis=-1, keepdims=True) + EPS) * g


def _mla_prep_kernel(cq_ref, ckv_ref, kr_ref, gq_ref, gkv_ref, wq_ref, wkv_ref, cos_ref, sin_ref,
                     qo_ref, ko_ref, vo_ref, *, n_pos):
    scale = LOG2E / math.sqrt(B_NOPE + B_ROPE)
    cos = cos_ref[...]
    sin = sin_ref[...]
    cq = jnp.concatenate([cq_ref[s] for s in range(B_Q_LORA // LANE)], axis=1)
    q = jnp.dot(_rms(cq, gq_ref[...]).astype(BF16), wq_ref[...], preferred_element_type=F32)
    ckv = jnp.concatenate([ckv_ref[s] for s in range(B_KV_LORA // LANE)], axis=1)
    kv = jnp.dot(_rms(ckv, gkv_ref[...]).astype(BF16), wkv_ref[...], preferred_element_type=F32)
    tm = cos.shape[0]
    pos = (pl.program_id(0) % n_pos) * tm + lax.broadcasted_iota(jnp.int32, (tm, LANE), 0)
    chunk = (pos % TK) >> CHUNK_SHIFT
    lane = lax.broadcasted_iota(jnp.int32, (tm, LANE), 1)
    k_rope = (kr_ref[0] * cos + kr_ref[1] * sin + _chunk_onehot(chunk, lane, B_ROPE)).astype(BF16)
    mask_q = _chunk_maskq(chunk, lane, B_ROPE)
    cos_sin = cos + pltpu.roll(sin, B_ROPE, 1)
    n_h = B_HEADS * LANE
    for h in range(B_HEADS):
        sl = slice(h * LANE, (h + 1) * LANE)
        t = q[:, n_h + h * LANE:n_h + (h + 1) * LANE] * cos_sin
        q_rope = t + pltpu.roll(t, B_ROPE, 1)
        qo_ref[h, :, :LANE] = (q[:, sl] * scale).astype(BF16)
        qo_ref[h, :, LANE:] = jnp.where(lane < B_ROPE, q_rope * scale, mask_q).astype(BF16)
        ko_ref[h, :, :LANE] = kv[:, sl].astype(BF16)
        ko_ref[h, :, LANE:] = k_rope
        vo_ref[h] = kv[:, n_h + h * LANE:n_h + (h + 1) * LANE].astype(BF16)


def _mla_prep(of, gq, gkv, wq, wkv, cos, sin, seq):
    m = of.shape[1]
    tm = TM_PROJ
    n_pos = seq // tm
    nq = B_Q_LORA // LANE
    nkv = B_KV_LORA // LANE
    return pl.pallas_call(
        functools.partial(_mla_prep_kernel, n_pos=n_pos),
        grid=(m // tm,),
        in_specs=[
            pl.BlockSpec((nq, tm, LANE), lambda i: (FSLAB_CQ // nq, i, 0)),
            pl.BlockSpec((nkv, tm, LANE), lambda i: (FSLAB_CKV // nkv, i, 0)),
            pl.BlockSpec((2, tm, LANE), lambda i: (FSLAB_KROPE // 2, i, 0)),
            pl.BlockSpec((1, B_Q_LORA), lambda i: (0, 0)),
            pl.BlockSpec((1, B_KV_LORA), lambda i: (0, 0)),
            pl.BlockSpec(wq.shape, lambda i: (0, 0)),
            pl.BlockSpec(wkv.shape, lambda i: (0, 0)),
            pl.BlockSpec((tm, LANE), lambda i: (i % n_pos, 0)),
            pl.BlockSpec((tm, LANE), lambda i: (i % n_pos, 0)),
        ],
        out_specs=[
            pl.BlockSpec((B_HEADS, tm, 2 * LANE), lambda i: (0, i, 0)),
            pl.BlockSpec((B_HEADS, tm, 2 * LANE), lambda i: (0, i, 0)),
            pl.BlockSpec((B_HEADS, tm, LANE), lambda i: (0, i, 0)),
        ],
        out_shape=[jax.ShapeDtypeStruct((B_HEADS, m, 2 * LANE), BF16),
                   jax.ShapeDtypeStruct((B_HEADS, m, 2 * LANE), BF16),
                   jax.ShapeDtypeStruct((B_HEADS, m, LANE), BF16)],
        compiler_params=_cparams(1),
        name="mla_prep",
    )(of, of, of, gq, gkv, wq, wkv, cos, sin)


CUM_CHUNK = 128


def _fox_bias_kernel(af_ref, bf_ref, kb_ref, cum_scr):
    seq = af_ref.shape[1]
    r = lax.broadcasted_iota(jnp.int32, (CUM_CHUNK, CUM_CHUNK), 0)
    c = lax.broadcasted_iota(jnp.int32, (CUM_CHUNK, CUM_CHUNK), 1)
    tri = jnp.where(r >= c, 1.0, 0.0).astype(BF16)
    bvec = bf_ref[...]

    def body(t, carry):
        r0 = pl.multiple_of(t * CUM_CHUNK, CUM_CHUNK)
        x = af_ref[0, pl.ds(r0, CUM_CHUNK), :] + bvec
        logf = jnp.minimum(x, 0.0) - jnp.log1p(jnp.exp(-jnp.abs(x)))
        cs = carry
        rest = logf
        for _ in range(FOX_TERMS):
            term = rest.astype(BF16)
            rest = rest - term.astype(F32)
            cs = cs + jnp.dot(tri, term, preferred_element_type=F32)
        cum_scr[pl.ds(r0, CUM_CHUNK), :] = cs
        return cs[CUM_CHUNK - 1:CUM_CHUNK, :]

    lax.fori_loop(0, seq // CUM_CHUNK, body, jnp.zeros((1, LANE), F32))
    lane = lax.broadcasted_iota(jnp.int32, (seq, LANE), 1)
    chunk = (lax.broadcasted_iota(jnp.int32, (seq, LANE), 0) % TK) >> CHUNK_SHIFT
    out = _chunk_onehot(chunk, lane, 0)
    rest = cum_scr[...] * (-LOG2E)
    for i in range(FOX_TERMS):
        term = rest.astype(BF16).astype(F32)
        rest = rest - term
        lo = AF_LANE + FOX_TERM_STRIDE * i
        moved = term if i == 0 else pltpu.roll(term, FOX_TERM_STRIDE * i, 1)
        out = jnp.where((lane >= lo) & (lane < lo + A_HEADS), moved, out)
    kb_ref[0] = out.astype(BF16)


def _fox_bias(of, b_forget, batch, seq):
    bvec = jnp.zeros((1, LANE), F32).at[0, AF_LANE:AF_LANE + A_HEADS].set(b_forget)
    of4 = of.reshape(N_F_SLABS, batch, seq, LANE)
    return pl.pallas_call(
        _fox_bias_kernel,
        grid=(batch,),
        in_specs=[
            pl.BlockSpec((None, 1, seq, LANE), lambda b: (N_F_SLABS - 1, b, 0, 0)),
            pl.BlockSpec((1, LANE), lambda b: (0, 0)),
        ],
        out_specs=pl.BlockSpec((1, seq, LANE), lambda b: (b, 0, 0)),
        out_shape=jax.ShapeDtypeStruct((batch, seq, LANE), BF16),
        scratch_shapes=[pltpu.VMEM((seq, LANE), F32)],
        compiler_params=_cparams(1),
        name="fox_bias",
    )(of4, bvec)


PAIR_Q, PAIR_K, PAIR_DIAG, PAIR_BIAS, PAIR_STATE = 0, 1, 2, 3, 4
ACC_ROWS = HEAD_DIM + 16
N_PAIR_BUFS = 6
PIPE_DIST = 4
N_BIAS_TILES = 2


class _PairTable:
    def __init__(self, n_tiles):
        rows = [(qi, kj, int(kj == qi), qi - kj, qi) for qi in range(n_tiles) for kj in range(qi + 1)]
        self.a = np.asarray(rows, np.int64).T
        self.shape = self.a.shape

    def __getitem__(self, idx):
        return int(self.a[idx])


def _tile_rows(ref, tile):
    return ref[pl.ds(pl.multiple_of(tile * TK, TK), TK), :]


def _load_vt(v_ref, vt_scr):
    extra = jnp.where(lax.broadcasted_iota(jnp.int32, (ACC_ROWS - HEAD_DIM, TK), 0) == 0, 1.0, 0.0).astype(BF16)
    for j in range(vt_scr.shape[0]):
        vt_scr[j, :HEAD_DIM, :] = v_ref[j * TK:(j + 1) * TK, :].T
        vt_scr[j, HEAD_DIM:, :] = extra


class _PairBufs:
    def __init__(self, s, mb):
        self.s, self.mb = s, mb


def _pair_scratch():
    return [pltpu.VMEM((TK, TQ), F32), pltpu.VMEM((1, TQ), F32)]


def _state_scratch(n_tiles):
    return [pltpu.VMEM((n_tiles, 1, TQ), F32), pltpu.VMEM((n_tiles, ACC_ROWS, TQ), F32)]


def _init_pipeline(m_all, acc_all):
    m_all[...] = jnp.full(m_all.shape, NEG_BIG, F32)
    acc_all[...] = jnp.zeros(acc_all.shape, F32)


def _pair_bufs(pair_scr):
    return [_PairBufs(*pair_scr[2 * i:2 * i + 2]) for i in range(len(pair_scr) // 2)]


def _qk_store(s, bufs):
    bufs.s[...] = s
    bufs.mb[...] = jnp.max(s, axis=0, keepdims=True)


def _softmax_pv_stage(tab_ref, t, bufs, m_all, vt_scr, acc_all):
    st = tab_ref[PAIR_STATE, t]
    m_prev = m_all[st]
    m_new = jnp.maximum(m_prev, bufs.mb[...])
    alpha = jnp.exp2(m_prev - m_new)
    p = jnp.exp2(bufs.s[...] - m_new).astype(BF16)
    m_all[st] = m_new
    acc_all[st] = alpha * acc_all[st] + jnp.dot(vt_scr[tab_ref[PAIR_K, t]], p, preferred_element_type=F32)


def _run_pipeline(tab, bufs, qk_stage, smpv_stage, finalize):
    u, d = N_PAIR_BUFS, PIPE_DIST
    assert u > d
    n_pairs = tab.shape[1]
    for c in range(min(d, n_pairs)):
        qk_stage(c, bufs[c % u])
    for c in range(n_pairs):
        smpv_stage(c, bufs[c % u])
        if c + d < n_pairs:
            qk_stage(c + d, bufs[(c + d) % u])
        if tab[PAIR_DIAG, c]:
            finalize(tab[PAIR_Q, c])


def _normalized(acc_all, qi):
    return acc_all[qi, :HEAD_DIM, :] * (1.0 / acc_all[qi, HEAD_DIM:HEAD_DIM + 1, :])


def _silu(g):
    return g / (1.0 + jnp.exp(-g))


def _head_spec(width, slab0, n_heads, seq):
    return pl.BlockSpec((None, None, seq, width), lambda g: (slab0 + g % n_heads, g // n_heads, 0, 0))


def _fox_attn_kernel(q_ref, k_ref, kb_ref, v_ref, g_ref, o_ref, vt_scr, qext_scr, m_all, acc_all, *pair_scr):
    tab_ref = _PairTable(m_all.shape[0])
    bufs = _pair_bufs(pair_scr)
    _load_vt(v_ref, vt_scr)
    _init_pipeline(m_all, acc_all)
    row = lax.broadcasted_iota(jnp.int32, (TQ, LANE), 0)
    lane = lax.broadcasted_iota(jnp.int32, (TQ, LANE), 1)
    rel = lane - (AF_LANE + pl.program_id(0) % A_HEADS)
    ones3 = jnp.where((rel >= 0) & (rel < FOX_TERM_STRIDE * FOX_TERMS) & (rel % FOX_TERM_STRIDE == 0), 1.0, 0.0)
    qext_scr[0] = ones3.astype(BF16)
    qext_scr[1] = (ones3 + _chunk_maskq(row >> CHUNK_SHIFT, lane, 0)).astype(BF16)
    sub_key = lax.broadcasted_iota(jnp.int32, (LANE, LANE), 0)
    sub_qry = lax.broadcasted_iota(jnp.int32, (LANE, LANE), 1)

    def qk_stage(t, bufs):
        q_cat = jnp.concatenate([_tile_rows(q_ref, tab_ref[PAIR_Q, t]), qext_scr[tab_ref[PAIR_DIAG, t]]], axis=1)
        kj = tab_ref[PAIR_K, t]
        k_cat = jnp.concatenate([_tile_rows(k_ref, kj), _tile_rows(kb_ref, kj)], axis=1)
        s = _nt_dot(k_cat, q_cat)
        if tab_ref[PAIR_DIAG, t]:
            blocks = []
            for r in range(TK // LANE):
                blk = s[r * LANE:(r + 1) * LANE, :]
                mid = jnp.where(sub_key > sub_qry, NEG_BIG, blk[:, r * LANE:(r + 1) * LANE])
                parts = ([blk[:, :r * LANE]] if r > 0 else []) + [mid]
                parts += [blk[:, (r + 1) * LANE:]] if (r + 1) * LANE < TQ else []
                blocks.append(jnp.concatenate(parts, axis=1))
            s = jnp.concatenate(blocks, axis=0)
        _qk_store(s, bufs)

    def finalize(qi):
        rows = slice(qi * TQ, (qi + 1) * TQ)
        o_ref[rows, :] = (_normalized(acc_all, qi).T * _silu(g_ref[rows, :])).astype(BF16)

    _run_pipeline(tab_ref, bufs, qk_stage,
                           lambda t, bufs: _softmax_pv_stage(tab_ref, t, bufs, m_all, vt_scr, acc_all), finalize)


def _fox_attn(ob4, kb, of4, batch, seq):
    hs = functools.partial(_head_spec, n_heads=A_HEADS, seq=seq)
    n_tiles = seq // TQ
    return pl.pallas_call(
        _fox_attn_kernel,
        grid=(batch * A_HEADS,),
        in_specs=[hs(LANE, SLAB_AQ), hs(LANE, SLAB_AK),
                  pl.BlockSpec((None, seq, LANE), lambda g: (g // A_HEADS, 0, 0)), hs(LANE, SLAB_AV),
                  hs(LANE, FSLAB_AGATE)],
        out_specs=hs(LANE, 0),
        out_shape=jax.ShapeDtypeStruct((A_HEADS, batch, seq, LANE), BF16),
        scratch_shapes=[pltpu.VMEM((n_tiles, ACC_ROWS, TK), BF16), pltpu.VMEM((2, TQ, LANE), BF16)]
        + _state_scratch(n_tiles) + N_PAIR_BUFS * _pair_scratch(),
        compiler_params=_cparams(1),
        name="fox_attn",
    )(ob4, ob4, kb, ob4, of4)


def _mla_attn_kernel(q_ref, k_ref, v_ref, g_ref, o_ref, vt_scr, m_all, acc_all, *pair_scr):
    tab_ref = _PairTable(m_all.shape[0])
    bufs = _pair_bufs(pair_scr)
    _load_vt(v_ref, vt_scr)
    _init_pipeline(m_all, acc_all)
    lane = lax.broadcasted_iota(jnp.int32, (TQ, LANE), 1)

    def qk_stage(t, bufs):
        q = _tile_rows(q_ref, tab_ref[PAIR_Q, t])
        first_off = LANE if tab_ref[PAIR_DIAG, t] else B_ROPE
        q_hi = jnp.where(lane >= first_off, jnp.zeros((), BF16), q[:, LANE:])
        q_cat = jnp.concatenate([q[:, :LANE], q_hi], axis=1)
        _qk_store(_nt_dot(_tile_rows(k_ref, tab_ref[PAIR_K, t]), q_cat), bufs)

    def finalize(qi):
        rows = slice(qi * TQ, (qi + 1) * TQ)
        o_ref[rows, :] = (_normalized(acc_all, qi).T * _silu(g_ref[rows, :])).astype(BF16)

    _run_pipeline(tab_ref, bufs, qk_stage,
                           lambda t, bufs: _softmax_pv_stage(tab_ref, t, bufs, m_all, vt_scr, acc_all), finalize)


def _mla_attn(q4, k4, v4, of4, batch, seq):
    hs = functools.partial(_head_spec, n_heads=B_HEADS, seq=seq)
    n_tiles = seq // TQ
    return pl.pallas_call(
        _mla_attn_kernel,
        grid=(batch * B_HEADS,),
        in_specs=[hs(2 * LANE, 0), hs(2 * LANE, 0), hs(LANE, 0), hs(LANE, FSLAB_BGATE)],
        out_specs=hs(LANE, 0),
        out_shape=jax.ShapeDtypeStruct((B_HEADS, batch, seq, LANE), BF16),
        scratch_shapes=[pltpu.VMEM((n_tiles, ACC_ROWS, TK), BF16)]
        + _state_scratch(n_tiles) + N_PAIR_BUFS * _pair_scratch(),
        compiler_params=_cparams(1),
        name="mla_attn",
    )(q4, k4, v4, of4)


def _diff_attn_kernel(q_ref, k_ref, v_ref, g_ref, bvec_ref, lq1_ref, lk1_ref, lq2_ref, lk2_ref, sg_ref,
                      o_ref, vt_scr, kext_scr, qext_scr, bias_scr, m1_all, acc1_all, m2_all, acc2_all, *pair_scr,
                      lam_init):
    tab_ref = _PairTable(m1_all.shape[0])
    flat = _pair_bufs(pair_scr)
    bufs = [(flat[2 * i], flat[2 * i + 1]) for i in range(N_PAIR_BUFS)]
    states = ((m1_all, acc1_all), (m2_all, acc2_all))
    _load_vt(v_ref, vt_scr)
    for m_all, acc_all in states:
        _init_pipeline(m_all, acc_all)
    for i in range(N_BIAS_TILES):
        gen = jnp.broadcast_to(bvec_ref[i], (TK, 2 * TQ))
        bias_scr[i] = pltpu.roll(gen, 0, 1, stride=1, stride_axis=0)[:, :TQ]
    row = lax.broadcasted_iota(jnp.int32, (TQ, LANE), 0)
    lane = lax.broadcasted_iota(jnp.int32, (TQ, LANE), 1)
    kext_scr[...] = _chunk_onehot(row >> CHUNK_SHIFT, lane, 0).astype(BF16)
    qext_scr[0] = jnp.zeros((TQ, LANE), BF16)
    qext_scr[1] = _chunk_maskq(row >> CHUNK_SHIFT, lane, 0).astype(BF16)
    lam = (jnp.exp(jnp.sum(lq1_ref[...] * lk1_ref[...], axis=-1, keepdims=True))
           - jnp.exp(jnp.sum(lq2_ref[...] * lk2_ref[...], axis=-1, keepdims=True)) + lam_init)

    def qk_stage(t, bufs2):
        q = _tile_rows(q_ref, tab_ref[PAIR_Q, t])
        qext = qext_scr[tab_ref[PAIR_DIAG, t]]
        zero = jnp.zeros_like(q)
        k_cat = jnp.concatenate([_tile_rows(k_ref, tab_ref[PAIR_K, t]), kext_scr[...]], axis=1)
        near = tab_ref[PAIR_BIAS, t] < N_BIAS_TILES
        for bufs, q_map in zip(bufs2, (jnp.where(lane < C_QK, q, zero), jnp.where(lane >= C_QK, q, zero))):
            s = _nt_dot(k_cat, jnp.concatenate([q_map, qext], axis=1))
            _qk_store(s + bias_scr[tab_ref[PAIR_BIAS, t]] if near else s, bufs)

    def smpv_stage(t, bufs2):
        for (m_all, acc_all), bufs in zip(states, bufs2):
            _softmax_pv_stage(tab_ref, t, bufs, m_all, vt_scr, acc_all)

    def finalize(qi):
        rows = slice(qi * TQ, (qi + 1) * TQ)
        o = _normalized(acc1_all, qi) - lam * _normalized(acc2_all, qi)
        o = o * lax.rsqrt(jnp.mean(o * o, axis=0, keepdims=True) + EPS)
        out = o.T * sg_ref[...] * (1.0 - lam_init)
        o_ref[rows, :] = (out * _silu(g_ref[rows, :])).astype(BF16)

    _run_pipeline(tab_ref, bufs, qk_stage, smpv_stage, finalize)


def _diff_attn(ob4, of4, bias, lq1, lk1, lq2, lk2, sg, lam_init, batch, seq):
    hs = functools.partial(_head_spec, n_heads=C_HEADS, seq=seq)
    vec = lambda n: pl.BlockSpec((1, n), lambda g: (0, 0))
    n_tiles = seq // TQ
    return pl.pallas_call(
        functools.partial(_diff_attn_kernel, lam_init=lam_init),
        grid=(batch * C_HEADS,),
        in_specs=[hs(LANE, SLAB_CQ), hs(LANE, SLAB_CK), hs(LANE, SLAB_CV), hs(LANE, FSLAB_CGATE),
                  pl.BlockSpec((None, 2, 1, 2 * TQ), lambda g: (g % C_HEADS, 0, 0, 0)),
                  vec(C_QK), vec(C_QK), vec(C_QK), vec(C_QK), vec(C_V)],
        out_specs=hs(LANE, 0),
        out_shape=jax.ShapeDtypeStruct((C_HEADS, batch, seq, LANE), BF16),
        scratch_shapes=[pltpu.VMEM((n_tiles, ACC_ROWS, TK), BF16), pltpu.VMEM((TK, LANE), BF16),
                        pltpu.VMEM((2, TQ, LANE), BF16), pltpu.VMEM((N_BIAS_TILES, TK, TQ), F32)]
        + 2 * _state_scratch(n_tiles) + 2 * N_PAIR_BUFS * _pair_scratch(),
        compiler_params=_cparams(1),
        name="diff_attn",
    )(ob4, ob4, ob4, of4, bias, lq1, lk1, lq2, lk2, sg)


def _outproj_kernel(a_ref, b_ref, c_ref, w_ref, x_ref, g_ref, o_ref, *, final_norm):
    mixed = jnp.concatenate([a_ref[h] for h in range(A_HEADS)] + [b_ref[h] for h in range(B_HEADS)]
                            + [c_ref[h] for h in range(C_HEADS)], axis=1)
    y = x_ref[...] + jnp.dot(mixed, w_ref[...], preferred_element_type=F32)
    if final_norm:
        y = _rms(y, g_ref[...])
    o_ref[...] = y


def _outproj(mix_a, mix_b, mix_c, w, x2, g, layer, final_norm):
    m = x2.shape[0]
    tm = TM_PROJ
    return pl.pallas_call(
        functools.partial(_outproj_kernel, final_norm=final_norm),
        grid=(m // tm,),
        in_specs=[
            pl.BlockSpec((A_HEADS, tm, LANE), lambda i: (0, i, 0)),
            pl.BlockSpec((B_HEADS, tm, LANE), lambda i: (0, i, 0)),
            pl.BlockSpec((C_HEADS, tm, LANE), lambda i: (0, i, 0)),
            pl.BlockSpec((None, D_MIX, D_MODEL), lambda i: (layer, 0, 0)),
            pl.BlockSpec((tm, D_MODEL), lambda i: (i, 0)),
            pl.BlockSpec((1, D_MODEL), lambda i: (0, 0)),
        ],
        out_specs=pl.BlockSpec((tm, D_MODEL), lambda i: (i, 0)),
        out_shape=jax.ShapeDtypeStruct((m, D_MODEL), F32),
        compiler_params=_cparams(1),
        name="outproj",
    )(mix_a, mix_b, mix_c, w, x2, g)


def kernel(x, norm_g, w_in, b_forget, mla_q_norm_g, w_uq, mla_kv_norm_g, w_ukv, lambda_q1, lambda_k1,
           lambda_q2, lambda_k2, diff_subln_g, rel_bias, w_out, final_norm_g):
    batch, seq, d = x.shape
    assert d == D_MODEL and seq % TQ == 0 and seq % TM_PROJ == 0 and (batch * seq) % TM_IN == 0
    m = batch * seq
    x2 = x.reshape(m, d)
    cs = _proj_col_scale()
    cos, sin = _rope_tables(seq)
    bias_vecs = _t5_bias_vectors(rel_bias)
    w_in_r = _prep_w_in(w_in)
    w_out_b = w_out.astype(BF16)
    for l in range(N_LAYERS):
        ob, of = _inproj(x2, norm_g[l].reshape(1, d), w_in_r, cs, l)
        ob4 = ob.reshape(N_BF_SLABS, batch, seq, LANE)
        of4 = of.reshape(N_F_SLABS, batch, seq, LANE)
        kb = _fox_bias(of, b_forget[l], batch, seq)
        mix_a = _fox_attn(ob4, kb, of4, batch, seq)
        qb, kbm, vb = _mla_prep(of, mla_q_norm_g[l].reshape(1, -1), mla_kv_norm_g[l].reshape(1, -1),
                                _prep_w_uq(w_uq[l]), _prep_w_ukv(w_ukv[l]), cos, sin, seq)
        r4 = lambda a: a.reshape(a.shape[0], batch, seq, a.shape[-1])
        mix_b = _mla_attn(r4(qb), r4(kbm), r4(vb), of4, batch, seq)
        lam_init = 0.8 - 0.6 * math.exp(-0.3 * l)
        mix_c = _diff_attn(ob4, of4, bias_vecs, lambda_q1[l].reshape(1, -1), lambda_k1[l].reshape(1, -1),
                           lambda_q2[l].reshape(1, -1), lambda_k2[l].reshape(1, -1),
                           diff_subln_g[l].reshape(1, -1), lam_init, batch, seq)
        x2 = _outproj(mix_a.reshape(A_HEADS, m, LANE), mix_b.reshape(B_HEADS, m, LANE),
                      mix_c.reshape(C_HEADS, m, LANE), w_out_b, x2,
                      final_norm_g.reshape(1, d), l, final_norm=(l == N_LAYERS - 1))
    return x2.reshape(batch, seq, d)
```

```python
import functools
import math

import numpy as np
import jax
import jax.numpy as jnp
from jax import lax
from jax.experimental import pallas as pl
from jax.experimental.pallas import tpu as pltpu

F32 = jnp.float32
BF16 = jnp.bfloat16

D_MODEL = 2048
N_LAYERS = 2
CHUNK = 64
HEAD_DIM = 128
EPS = 1e-6
A_HEADS = 6
A_DIM = A_HEADS * HEAD_DIM
B_HEADS = 6
B_Q_LORA = 512
B_KV_LORA = 256
B_NOPE = 128
B_ROPE = 64
B_V = 128
B_DIM = B_HEADS * B_V
ROPE_THETA = 10000.0
C_HEADS = 4
C_QK = 64
C_V = 2 * C_QK
C_DIM = C_HEADS * C_V
REL_BUCKETS = 32
REL_MAX_DIST = 128
D_MIX = A_DIM + B_DIM + C_DIM

LANE = 128
V7X_VMEM_BYTES = 64 * 1024 * 1024
VMEM_LIMIT = 56 * 1024 * 1024

LOG2E = math.log2(math.e)
NEG_BIG = -1e30

TN = 768
SLABS_PER_TILE = TN // LANE
N_BF_SLABS = 30
N_F_SLABS = 24
N_BF_TILES = N_BF_SLABS // SLABS_PER_TILE
N_F_TILES = N_F_SLABS // SLABS_PER_TILE
N_PROJ = (N_BF_SLABS + N_F_SLABS) * LANE
SLAB_AQ, SLAB_AK, SLAB_AV, SLAB_CQ, SLAB_CK, SLAB_CV = 0, 6, 12, 18, 22, 26
FSLAB_AGATE, FSLAB_BGATE, FSLAB_CGATE, FSLAB_CQ, FSLAB_CKV, FSLAB_KROPE = 0, 6, 12, 16, 20, 22
AF_LANE = 64

TM_PROJ = 512
TM_IN = 1024
TQ = 512
TK = 512

CHUNK_SHIFT = CHUNK.bit_length() - 1
N_TILE_CHUNKS = TK // CHUNK
FOX_TERM_STRIDE = A_HEADS
FOX_TERMS = 3


def _chunk_onehot(row_chunk, lane, lane0):
    return jnp.where(lane - lane0 == row_chunk, 1.0, 0.0)


def _chunk_maskq(row_chunk, lane, lane0):
    c = lane - lane0
    return jnp.where((c >= 0) & (c < N_TILE_CHUNKS) & (row_chunk < c), NEG_BIG, 0.0)


def _nt_dot(a, b):
    return lax.dot_general(a, b, (((1,), (1,)), ((), ())), preferred_element_type=F32)


def _cparams(n_grid):
    return pltpu.CompilerParams(dimension_semantics=("arbitrary",) * n_grid,
                                vmem_limit_bytes=VMEM_LIMIT)


def _w_in_slab_sources():
    o = 0
    seg = {}
    for name, size in (("a_q", A_DIM), ("a_k", A_DIM), ("a_v", A_DIM), ("a_f", A_HEADS), ("a_gate", A_DIM),
                       ("b_cq", B_Q_LORA), ("b_ckv", B_KV_LORA), ("b_krope", B_ROPE), ("b_gate", B_DIM),
                       ("c_q", C_DIM), ("c_k", C_DIM), ("c_v", C_DIM), ("c_gate", C_DIM)):
        seg[name] = (o, size)
        o += size
    slabs = []
    for name in ("a_q", "a_k", "a_v", "c_q", "c_k", "c_v", "a_gate", "b_gate", "c_gate", "b_cq", "b_ckv"):
        start, size = seg[name]
        slabs += [[(start + i, start + i + LANE, 1)] for i in range(0, size, LANE)]
    kr, half = seg["b_krope"][0], B_ROPE // 2
    af = seg["a_f"][0]
    slabs.append([(kr, kr + B_ROPE, 1), (None, LANE - B_ROPE, 0)])
    slabs.append([(kr + half, kr + B_ROPE, -1), (kr, kr + half, 1), (af, af + A_HEADS, 1),
                  (None, LANE - B_ROPE - A_HEADS, 0)])
    assert len(slabs) * LANE == N_PROJ
    return slabs


N_SPECIAL_SLABS = 2


def _w_in_kernel(tab_ref, w_ref, sp_ref, o_ref, *, n_layers):
    j = pl.program_id(0)
    n_regular = pl.num_programs(0) - N_SPECIAL_SLABS
    k_chunks = o_ref.shape[2] // LANE
    rows_per_col = n_layers * k_chunks

    @pl.when(j < n_regular)
    def _():
        for kc in range(k_chunks):
            for l in range(n_layers):
                rows = w_ref[pl.ds(kc * n_layers + l, LANE, stride=rows_per_col), :]
                o_ref[l, :, kc * LANE:(kc + 1) * LANE] = rows.astype(BF16)

    @pl.when(j >= n_regular)
    def _():
        o_ref[...] = sp_ref[:, pl.ds(pl.multiple_of((j - n_regular) * LANE, LANE), LANE), :].astype(BF16)


def _prep_w_in(w):
    n_layers, d, n = w.shape
    k_chunks = d // LANE
    rows_per_col = n_layers * k_chunks
    view = jnp.transpose(w.reshape(n_layers, k_chunks, LANE, n), (3, 1, 0, 2))
    flat = view.reshape(n * rows_per_col, LANE)
    slabs = _w_in_slab_sources()
    n_regular = len(slabs) - N_SPECIAL_SLABS
    assert all(len(p) == 1 and p[0][2] == 1 for p in slabs[:n_regular])
    tab = jnp.asarray([p[0][0] for p in slabs[:n_regular]] + [0] * N_SPECIAL_SLABS, jnp.int32)

    def cols(a, b):
        part = flat[a * rows_per_col:b * rows_per_col].reshape(b - a, k_chunks, n_layers, LANE)
        return jnp.transpose(part, (2, 0, 1, 3)).reshape(n_layers, b - a, d)

    special = []
    for pieces in slabs[n_regular:]:
        for start, stop, sign in pieces:
            if start is None:
                special.append(jnp.zeros((n_layers, stop, d), w.dtype))
            else:
                special.append(cols(start, stop) if sign > 0 else -cols(start, stop))
    special = jnp.concatenate(special, axis=1)
    return pl.pallas_call(
        functools.partial(_w_in_kernel, n_layers=n_layers),
        grid_spec=pltpu.PrefetchScalarGridSpec(
            num_scalar_prefetch=1,
            grid=(len(slabs),),
            in_specs=[pl.BlockSpec((pl.Element(LANE * rows_per_col), pl.Element(LANE)),
                                   lambda j, tab: (tab[j] * rows_per_col, 0)),
                      pl.BlockSpec(special.shape, lambda j, tab: (0, 0, 0))],
            out_specs=pl.BlockSpec((n_layers, LANE, d), lambda j, tab: (0, j, 0)),
        ),
        out_shape=jax.ShapeDtypeStruct((n_layers, N_PROJ, d), BF16),
        compiler_params=_cparams(1),
        name="w_in_reorder",
    )(tab, flat, special)


def _proj_col_scale():
    s = jnp.ones((N_PROJ,), F32)
    s = s.at[SLAB_AQ * LANE:(SLAB_AQ + A_HEADS) * LANE].set(LOG2E / math.sqrt(HEAD_DIM))
    s = s.at[SLAB_CQ * LANE:(SLAB_CQ + C_HEADS) * LANE].set(LOG2E / math.sqrt(C_QK))
    return s.reshape(1, N_PROJ)


def _prep_w_uq(w):
    w = w.reshape(B_Q_LORA, B_HEADS, B_NOPE + B_ROPE)
    nope = w[:, :, :B_NOPE].reshape(B_Q_LORA, B_HEADS * B_NOPE)
    rope = w[:, :, B_NOPE:]
    half = B_ROPE // 2
    rot = jnp.concatenate([-rope[:, :, half:], rope[:, :, :half]], axis=2)
    both = jnp.concatenate([rope, rot], axis=2).reshape(B_Q_LORA, B_HEADS * LANE)
    return jnp.concatenate([nope, both], axis=1).astype(BF16)


def _prep_w_ukv(w):
    w = w.reshape(B_KV_LORA, B_HEADS, B_NOPE + B_V)
    k = w[:, :, :B_NOPE].reshape(B_KV_LORA, B_HEADS * B_NOPE)
    v = w[:, :, B_NOPE:].reshape(B_KV_LORA, B_HEADS * B_V)
    return jnp.concatenate([k, v], axis=1).astype(BF16)


def _rope_tables(seq):
    half = B_ROPE // 2
    inv = ROPE_THETA ** (-jnp.arange(half, dtype=F32) / half)
    ang = jnp.arange(seq).astype(F32)[:, None] * inv[None, :]
    pad = jnp.zeros((seq, LANE - B_ROPE), F32)
    cos = jnp.concatenate([jnp.cos(ang), jnp.cos(ang), pad], axis=1)
    sin = jnp.concatenate([jnp.sin(ang), jnp.sin(ang), pad], axis=1)
    return cos, sin


def _t5_bucket(rel):
    nb = REL_BUCKETS // 2
    max_exact = nb // 2
    ret = (rel > 0).astype(jnp.int32) * nb
    n = jnp.abs(rel)
    nf = jnp.maximum(n, 1).astype(F32)
    large = max_exact + (jnp.log(nf / max_exact) / math.log(REL_MAX_DIST / max_exact)
                         * (nb - max_exact)).astype(jnp.int32)
    large = jnp.minimum(large, nb - 1)
    return ret + jnp.where(n < max_exact, n, large)


def _t5_bias_vectors(rel_bias):
    assert TQ == TK and TQ > REL_MAX_DIST
    x = jnp.arange(2 * TQ)
    x = jnp.where(x < TQ, x, x - 2 * TQ)
    bucket = jnp.stack([_t5_bucket(-x - delta) for delta in (0, TQ)])
    rel = rel_bias - rel_bias[REL_BUCKETS // 2 - 1][None, :]
    t = jnp.transpose(rel[bucket], (2, 0, 1))
    return (t * LOG2E).reshape(C_HEADS, 2, 1, 2 * TQ)


def _inproj_kernel(x_ref, g_ref, w_ref, cs_ref, ob_ref, of_ref, h_scr):
    j = pl.program_id(1)

    def project(h, out_ref):
        acc = _nt_dot(h, w_ref[...]) * cs_ref[...]
        for s in range(SLABS_PER_TILE):
            out_ref[s] = acc[:, s * LANE:(s + 1) * LANE].astype(out_ref.dtype)

    @pl.when(j == 0)
    def _():
        x = x_ref[...]
        ms = jnp.mean(x * x, axis=-1, keepdims=True)
        h = (x * lax.rsqrt(ms + EPS) * g_ref[...]).astype(BF16)
        h_scr[...] = h
        project(h, ob_ref)

    @pl.when((j > 0) & (j < N_BF_TILES))
    def _():
        project(h_scr[...], ob_ref)

    @pl.when(j >= N_BF_TILES)
    def _():
        project(h_scr[...], of_ref)


def _inproj(x2, g, w, cs, layer):
    m = x2.shape[0]
    tm = TM_IN
    return pl.pallas_call(
        _inproj_kernel,
        grid=(m // tm, N_BF_TILES + N_F_TILES),
        in_specs=[
            pl.BlockSpec((tm, D_MODEL), lambda i, j: (i, 0)),
            pl.BlockSpec((1, D_MODEL), lambda i, j: (0, 0)),
            pl.BlockSpec((None, TN, D_MODEL), lambda i, j: (layer, j, 0)),
            pl.BlockSpec((1, TN), lambda i, j: (0, j)),
        ],
        out_specs=[
            pl.BlockSpec((SLABS_PER_TILE, tm, LANE), lambda i, j: (jnp.minimum(j, N_BF_TILES - 1), i, 0)),
            pl.BlockSpec((SLABS_PER_TILE, tm, LANE), lambda i, j: (jnp.maximum(j - N_BF_TILES, 0), i, 0)),
        ],
        out_shape=[jax.ShapeDtypeStruct((N_BF_SLABS, m, LANE), BF16),
                   jax.ShapeDtypeStruct((N_F_SLABS, m, LANE), F32)],
        scratch_shapes=[pltpu.VMEM((tm, D_MODEL), BF16)],
        compiler_params=_cparams(2),
        name="inproj",
    )(x2, g, w, cs)


def _rms(x, g):
    return x * lax.rsqrt(jnp.mean(x * x, axis=-1, keepdims=True) + EPS) * g


def _mla_prep_kernel(cq_ref, ckv_ref, kr_ref, gq_ref, gkv_ref, wq_ref, wkv_ref, cos_ref, sin_ref,
                     qo_ref, ko_ref, vo_ref, *, n_pos):
    scale = LOG2E / math.sqrt(B_NOPE + B_ROPE)
    cos = cos_ref[...]
    sin = sin_ref[...]
    cq = jnp.concatenate([cq_ref[s] for s in range(B_Q_LORA // LANE)], axis=1)
    q = jnp.dot(_rms(cq, gq_ref[...]).astype(BF16), wq_ref[...], preferred_element_type=F32)
    ckv = jnp.concatenate([ckv_ref[s] for s in range(B_KV_LORA // LANE)], axis=1)
    kv = jnp.dot(_rms(ckv, gkv_ref[...]).astype(BF16), wkv_ref[...], preferred_element_type=F32)
    tm = cos.shape[0]
    pos = (pl.program_id(0) % n_pos) * tm + lax.broadcasted_iota(jnp.int32, (tm, LANE), 0)
    chunk = (pos % TK) >> CHUNK_SHIFT
    lane = lax.broadcasted_iota(jnp.int32, (tm, LANE), 1)
    k_rope = (kr_ref[0] * cos + kr_ref[1] * sin + _chunk_onehot(chunk, lane, B_ROPE)).astype(BF16)
    mask_q = _chunk_maskq(chunk, lane, B_ROPE)
    cos_sin = cos + pltpu.roll(sin, B_ROPE, 1)
    n_h = B_HEADS * LANE
    for h in range(B_HEADS):
        sl = slice(h * LANE, (h + 1) * LANE)
        t = q[:, n_h + h * LANE:n_h + (h + 1) * LANE] * cos_sin
        q_rope = t + pltpu.roll(t, B_ROPE, 1)
        qo_ref[h, :, :LANE] = (q[:, sl] * scale).astype(BF16)
        qo_ref[h, :, LANE:] = jnp.where(lane < B_ROPE, q_rope * scale, mask_q).astype(BF16)
        ko_ref[h, :, :LANE] = kv[:, sl].astype(BF16)
        ko_ref[h, :, LANE:] = k_rope
        vo_ref[h] = kv[:, n_h + h * LANE:n_h + (h + 1) * LANE].astype(BF16)


def _mla_prep(of, gq, gkv, wq, wkv, cos, sin, seq):
    m = of.shape[1]
    tm = TM_PROJ
    n_pos = seq // tm
    nq = B_Q_LORA // LANE
    nkv = B_KV_LORA // LANE
    return pl.pallas_call(
        functools.partial(_mla_prep_kernel, n_pos=n_pos),
        grid=(m // tm,),
        in_specs=[
            pl.BlockSpec((nq, tm, LANE), lambda i: (FSLAB_CQ // nq, i, 0)),
            pl.BlockSpec((nkv, tm, LANE), lambda i: (FSLAB_CKV // nkv, i, 0)),
            pl.BlockSpec((2, tm, LANE), lambda i: (FSLAB_KROPE // 2, i, 0)),
            pl.BlockSpec((1, B_Q_LORA), lambda i: (0, 0)),
            pl.BlockSpec((1, B_KV_LORA), lambda i: (0, 0)),
            pl.BlockSpec(wq.shape, lambda i: (0, 0)),
            pl.BlockSpec(wkv.shape, lambda i: (0, 0)),
            pl.BlockSpec((tm, LANE), lambda i: (i % n_pos, 0)),
            pl.BlockSpec((tm, LANE), lambda i: (i % n_pos, 0)),
        ],
        out_specs=[
            pl.BlockSpec((B_HEADS, tm, 2 * LANE), lambda i: (0, i, 0)),
            pl.BlockSpec((B_HEADS, tm, 2 * LANE), lambda i: (0, i, 0)),
            pl.BlockSpec((B_HEADS, tm, LANE), lambda i: (0, i, 0)),
        ],
        out_shape=[jax.ShapeDtypeStruct((B_HEADS, m, 2 * LANE), BF16),
                   jax.ShapeDtypeStruct((B_HEADS, m, 2 * LANE), BF16),
                   jax.ShapeDtypeStruct((B_HEADS, m, LANE), BF16)],
        compiler_params=_cparams(1),
        name="mla_prep",
    )(of, of, of, gq, gkv, wq, wkv, cos, sin)


CUM_CHUNK = 128


def _fox_bias_kernel(af_ref, bf_ref, kb_ref, cum_scr):
    seq = af_ref.shape[1]
    r = lax.broadcasted_iota(jnp.int32, (CUM_CHUNK, CUM_CHUNK), 0)
    c = lax.broadcasted_iota(jnp.int32, (CUM_CHUNK, CUM_CHUNK), 1)
    tri = jnp.where(r >= c, 1.0, 0.0).astype(BF16)
    bvec = bf_ref[...]

    def body(t, carry):
        r0 = pl.multiple_of(t * CUM_CHUNK, CUM_CHUNK)
        x = af_ref[0, pl.ds(r0, CUM_CHUNK), :] + bvec
        logf = jnp.minimum(x, 0.0) - jnp.log1p(jnp.exp(-jnp.abs(x)))
        cs = carry
        rest = logf
        for _ in range(FOX_TERMS):
            term = rest.astype(BF16)
            rest = rest - term.astype(F32)
            cs = cs + jnp.dot(tri, term, preferred_element_type=F32)
        cum_scr[pl.ds(r0, CUM_CHUNK), :] = cs
        return cs[CUM_CHUNK - 1:CUM_CHUNK, :]

    lax.fori_loop(0, seq // CUM_CHUNK, body, jnp.zeros((1, LANE), F32))
    lane = lax.broadcasted_iota(jnp.int32, (seq, LANE), 1)
    chunk = (lax.broadcasted_iota(jnp.int32, (seq, LANE), 0) % TK) >> CHUNK_SHIFT
    out = _chunk_onehot(chunk, lane, 0)
    rest = cum_scr[...] * (-LOG2E)
    for i in range(FOX_TERMS):
        term = rest.astype(BF16).astype(F32)
        rest = rest - term
        lo = AF_LANE + FOX_TERM_STRIDE * i
        moved = term if i == 0 else pltpu.roll(term, FOX_TERM_STRIDE * i, 1)
        out = jnp.where((lane >= lo) & (lane < lo + A_HEADS), moved, out)
    kb_ref[0] = out.astype(BF16)


def _fox_bias(of, b_forget, batch, seq):
    bvec = jnp.zeros((1, LANE), F32).at[0, AF_LANE:AF_LANE + A_HEADS].set(b_forget)
    of4 = of.reshape(N_F_SLABS, batch, seq, LANE)
    return pl.pallas_call(
        _fox_bias_kernel,
        grid=(batch,),
        in_specs=[
            pl.BlockSpec((None, 1, seq, LANE), lambda b: (N_F_SLABS - 1, b, 0, 0)),
            pl.BlockSpec((1, LANE), lambda b: (0, 0)),
        ],
        out_specs=pl.BlockSpec((1, seq, LANE), lambda b: (b, 0, 0)),
        out_shape=jax.ShapeDtypeStruct((batch, seq, LANE), BF16),
        scratch_shapes=[pltpu.VMEM((seq, LANE), F32)],
        compiler_params=_cparams(1),
        name="fox_bias",
    )(of4, bvec)


PAIR_Q, PAIR_K, PAIR_DIAG, PAIR_BIAS, PAIR_STATE = 0, 1, 2, 3, 4
ACC_ROWS = HEAD_DIM + 16
N_PAIR_BUFS = 5
PIPE_DIST = 3
N_BIAS_TILES = 2


class _PairTable:
    def __init__(self, n_tiles):
        rows = [(qi, kj, int(kj == qi), qi - kj, qi) for qi in range(n_tiles) for kj in range(qi + 1)]
        self.a = np.asarray(rows, np.int64).T
        self.shape = self.a.shape

    def __getitem__(self, idx):
        return int(self.a[idx])


def _tile_rows(ref, tile):
    return ref[pl.ds(pl.multiple_of(tile * TK, TK), TK), :]


def _load_vt(v_ref, vt_scr):
    extra = jnp.where(lax.broadcasted_iota(jnp.int32, (ACC_ROWS - HEAD_DIM, TK), 0) == 0, 1.0, 0.0).astype(BF16)
    for j in range(vt_scr.shape[0]):
        vt_scr[j, :HEAD_DIM, :] = v_ref[j * TK:(j + 1) * TK, :].T
        vt_scr[j, HEAD_DIM:, :] = extra


class _PairBufs:
    def __init__(self, s, mb):
        self.s, self.mb = s, mb


def _pair_scratch():
    return [pltpu.VMEM((TK, TQ), F32), pltpu.VMEM((1, TQ), F32)]


def _state_scratch(n_tiles):
    return [pltpu.VMEM((n_tiles, 1, TQ), F32), pltpu.VMEM((n_tiles, ACC_ROWS, TQ), F32)]


def _init_pipeline(m_all, acc_all):
    m_all[...] = jnp.full(m_all.shape, NEG_BIG, F32)
    acc_all[...] = jnp.zeros(acc_all.shape, F32)


def _pair_bufs(pair_scr):
    return [_PairBufs(*pair_scr[2 * i:2 * i + 2]) for i in range(len(pair_scr) // 2)]


def _qk_store(s, bufs):
    bufs.s[...] = s
    bufs.mb[...] = jnp.max(s, axis=0, keepdims=True)


def _softmax_pv_stage(tab_ref, t, bufs, m_all, vt_scr, acc_all):
    st = tab_ref[PAIR_STATE, t]
    m_prev = m_all[st]
    m_new = jnp.maximum(m_prev, bufs.mb[...])
    alpha = jnp.exp2(m_prev - m_new)
    p = jnp.exp2(bufs.s[...] - m_new).astype(BF16)
    m_all[st] = m_new
    acc_all[st] = alpha * acc_all[st] + jnp.dot(vt_scr[tab_ref[PAIR_K, t]], p, preferred_element_type=F32)


def _run_pipeline(tab, bufs, qk_stage, smpv_stage, finalize):
    u, d = N_PAIR_BUFS, PIPE_DIST
    assert u > d
    n_pairs = tab.shape[1]
    for c in range(min(d, n_pairs)):
        qk_stage(c, bufs[c % u])
    for c in range(n_pairs):
        smpv_stage(c, bufs[c % u])
        if c + d < n_pairs:
            qk_stage(c + d, bufs[(c + d) % u])
        if tab[PAIR_DIAG, c]:
            finalize(tab[PAIR_Q, c])


def _normalized(acc_all, qi):
    return acc_all[qi, :HEAD_DIM, :] * (1.0 / acc_all[qi, HEAD_DIM:HEAD_DIM + 1, :])


def _silu(g):
    return g / (1.0 + jnp.exp(-g))


def _head_spec(width, slab0, n_heads, seq):
    return pl.BlockSpec((None, None, seq, width), lambda g: (slab0 + g % n_heads, g // n_heads, 0, 0))


def _fox_attn_kernel(q_ref, k_ref, kb_ref, v_ref, g_ref, o_ref, vt_scr, qext_scr, m_all, acc_all, *pair_scr):
    tab_ref = _PairTable(m_all.shape[0])
    bufs = _pair_bufs(pair_scr)
    _load_vt(v_ref, vt_scr)
    _init_pipeline(m_all, acc_all)
    row = lax.broadcasted_iota(jnp.int32, (TQ, LANE), 0)
    lane = lax.broadcasted_iota(jnp.int32, (TQ, LANE), 1)
    rel = lane - (AF_LANE + pl.program_id(0) % A_HEADS)
    ones3 = jnp.where((rel >= 0) & (rel < FOX_TERM_STRIDE * FOX_TERMS) & (rel % FOX_TERM_STRIDE == 0), 1.0, 0.0)
    qext_scr[0] = ones3.astype(BF16)
    qext_scr[1] = (ones3 + _chunk_maskq(row >> CHUNK_SHIFT, lane, 0)).astype(BF16)
    sub_key = lax.broadcasted_iota(jnp.int32, (LANE, LANE), 0)
    sub_qry = lax.broadcasted_iota(jnp.int32, (LANE, LANE), 1)

    def qk_stage(t, bufs):
        q_cat = jnp.concatenate([_tile_rows(q_ref, tab_ref[PAIR_Q, t]), qext_scr[tab_ref[PAIR_DIAG, t]]], axis=1)
        kj = tab_ref[PAIR_K, t]
        k_cat = jnp.concatenate([_tile_rows(k_ref, kj), _tile_rows(kb_ref, kj)], axis=1)
        s = _nt_dot(k_cat, q_cat)
        if tab_ref[PAIR_DIAG, t]:
            blocks = []
            for r in range(TK // LANE):
                blk = s[r * LANE:(r + 1) * LANE, :]
                mid = jnp.where(sub_key > sub_qry, NEG_BIG, blk[:, r * LANE:(r + 1) * LANE])
                parts = ([blk[:, :r * LANE]] if r > 0 else []) + [mid]
                parts += [blk[:, (r + 1) * LANE:]] if (r + 1) * LANE < TQ else []
                blocks.append(jnp.concatenate(parts, axis=1))
            s = jnp.concatenate(blocks, axis=0)
        _qk_store(s, bufs)

    def finalize(qi):
        rows = slice(qi * TQ, (qi + 1) * TQ)
        o_ref[rows, :] = (_normalized(acc_all, qi).T * _silu(g_ref[rows, :])).astype(BF16)

    _run_pipeline(tab_ref, bufs, qk_stage,
                           lambda t, bufs: _softmax_pv_stage(tab_ref, t, bufs, m_all, vt_scr, acc_all), finalize)


def _fox_attn(ob4, kb, of4, batch, seq):
    hs = functools.partial(_head_spec, n_heads=A_HEADS, seq=seq)
    n_tiles = seq // TQ
    return pl.pallas_call(
        _fox_attn_kernel,
        grid=(batch * A_HEADS,),
        in_specs=[hs(LANE, SLAB_AQ), hs(LANE, SLAB_AK),
                  pl.BlockSpec((None, seq, LANE), lambda g: (g // A_HEADS, 0, 0)), hs(LANE, SLAB_AV),
                  hs(LANE, FSLAB_AGATE)],
        out_specs=hs(LANE, 0),
        out_shape=jax.ShapeDtypeStruct((A_HEADS, batch, seq, LANE), BF16),
        scratch_shapes=[pltpu.VMEM((n_tiles, ACC_ROWS, TK), BF16), pltpu.VMEM((2, TQ, LANE), BF16)]
        + _state_scratch(n_tiles) + N_PAIR_BUFS * _pair_scratch(),
        compiler_params=_cparams(1),
        name="fox_attn",
    )(ob4, ob4, kb, ob4, of4)


def _mla_attn_kernel(q_ref, k_ref, v_ref, g_ref, o_ref, vt_scr, m_all, acc_all, *pair_scr):
    tab_ref = _PairTable(m_all.shape[0])
    bufs = _pair_bufs(pair_scr)
    _load_vt(v_ref, vt_scr)
    _init_pipeline(m_all, acc_all)
    lane = lax.broadcasted_iota(jnp.int32, (TQ, LANE), 1)

    def qk_stage(t, bufs):
        q = _tile_rows(q_ref, tab_ref[PAIR_Q, t])
        first_off = LANE if tab_ref[PAIR_DIAG, t] else B_ROPE
        q_hi = jnp.where(lane >= first_off, jnp.zeros((), BF16), q[:, LANE:])
        q_cat = jnp.concatenate([q[:, :LANE], q_hi], axis=1)
        _qk_store(_nt_dot(_tile_rows(k_ref, tab_ref[PAIR_K, t]), q_cat), bufs)

    def finalize(qi):
        rows = slice(qi * TQ, (qi + 1) * TQ)
        o_ref[rows, :] = (_normalized(acc_all, qi).T * _silu(g_ref[rows, :])).astype(BF16)

    _run_pipeline(tab_ref, bufs, qk_stage,
                           lambda t, bufs: _softmax_pv_stage(tab_ref, t, bufs, m_all, vt_scr, acc_all), finalize)


def _mla_attn(q4, k4, v4, of4, batch, seq):
    hs = functools.partial(_head_spec, n_heads=B_HEADS, seq=seq)
    n_tiles = seq // TQ
    return pl.pallas_call(
        _mla_attn_kernel,
        grid=(batch * B_HEADS,),
        in_specs=[hs(2 * LANE, 0), hs(2 * LANE, 0), hs(LANE, 0), hs(LANE, FSLAB_BGATE)],
        out_specs=hs(LANE, 0),
        out_shape=jax.ShapeDtypeStruct((B_HEADS, batch, seq, LANE), BF16),
        scratch_shapes=[pltpu.VMEM((n_tiles, ACC_ROWS, TK), BF16)]
        + _state_scratch(n_tiles) + N_PAIR_BUFS * _pair_scratch(),
        compiler_params=_cparams(1),
        name="mla_attn",
    )(q4, k4, v4, of4)


def _diff_attn_kernel(q_ref, k_ref, v_ref, g_ref, bvec_ref, lq1_ref, lk1_ref, lq2_ref, lk2_ref, sg_ref,
                      o_ref, vt_scr, kext_scr, qext_scr, bias_scr, m1_all, acc1_all, m2_all, acc2_all, *pair_scr,
                      lam_init):
    tab_ref = _PairTable(m1_all.shape[0])
    flat = _pair_bufs(pair_scr)
    bufs = [(flat[2 * i], flat[2 * i + 1]) for i in range(N_PAIR_BUFS)]
    states = ((m1_all, acc1_all), (m2_all, acc2_all))
    _load_vt(v_ref, vt_scr)
    for m_all, acc_all in states:
        _init_pipeline(m_all, acc_all)
    for i in range(N_BIAS_TILES):
        gen = jnp.broadcast_to(bvec_ref[i], (TK, 2 * TQ))
        bias_scr[i] = pltpu.roll(gen, 0, 1, stride=1, stride_axis=0)[:, :TQ]
    row = lax.broadcasted_iota(jnp.int32, (TQ, LANE), 0)
    lane = lax.broadcasted_iota(jnp.int32, (TQ, LANE), 1)
    kext_scr[...] = _chunk_onehot(row >> CHUNK_SHIFT, lane, 0).astype(BF16)
    qext_scr[0] = jnp.zeros((TQ, LANE), BF16)
    qext_scr[1] = _chunk_maskq(row >> CHUNK_SHIFT, lane, 0).astype(BF16)
    lam = (jnp.exp(jnp.sum(lq1_ref[...] * lk1_ref[...], axis=-1, keepdims=True))
           - jnp.exp(jnp.sum(lq2_ref[...] * lk2_ref[...], axis=-1, keepdims=True)) + lam_init)

    def qk_stage(t, bufs2):
        q = _tile_rows(q_ref, tab_ref[PAIR_Q, t])
        qext = qext_scr[tab_ref[PAIR_DIAG, t]]
        zero = jnp.zeros_like(q)
        k_cat = jnp.concatenate([_tile_rows(k_ref, tab_ref[PAIR_K, t]), kext_scr[...]], axis=1)
        near = tab_ref[PAIR_BIAS, t] < N_BIAS_TILES
        for bufs, q_map in zip(bufs2, (jnp.where(lane < C_QK, q, zero), jnp.where(lane >= C_QK, q, zero))):
            s = _nt_dot(k_cat, jnp.concatenate([q_map, qext], axis=1))
            _qk_store(s + bias_scr[tab_ref[PAIR_BIAS, t]] if near else s, bufs)

    def smpv_stage(t, bufs2):
        for (m_all, acc_all), bufs in zip(states, bufs2):
            _softmax_pv_stage(tab_ref, t, bufs, m_all, vt_scr, acc_all)

    def finalize(qi):
        rows = slice(qi * TQ, (qi + 1) * TQ)
        o = _normalized(acc1_all, qi) - lam * _normalized(acc2_all, qi)
        o = o * lax.rsqrt(jnp.mean(o * o, axis=0, keepdims=True) + EPS)
        out = o.T * sg_ref[...] * (1.0 - lam_init)
        o_ref[rows, :] = (out * _silu(g_ref[rows, :])).astype(BF16)

    _run_pipeline(tab_ref, bufs, qk_stage, smpv_stage, finalize)


def _diff_attn(ob4, of4, bias, lq1, lk1, lq2, lk2, sg, lam_init, batch, seq):
    hs = functools.partial(_head_spec, n_heads=C_HEADS, seq=seq)
    vec = lambda n: pl.BlockSpec((1, n), lambda g: (0, 0))
    n_tiles = seq // TQ
    return pl.pallas_call(
        functools.partial(_diff_attn_kernel, lam_init=lam_init),
        grid=(batch * C_HEADS,),
        in_specs=[hs(LANE, SLAB_CQ), hs(LANE, SLAB_CK), hs(LANE, SLAB_CV), hs(LANE, FSLAB_CGATE),
                  pl.BlockSpec((None, 2, 1, 2 * TQ), lambda g: (g % C_HEADS, 0, 0, 0)),
                  vec(C_QK), vec(C_QK), vec(C_QK), vec(C_QK), vec(C_V)],
        out_specs=hs(LANE, 0),
        out_shape=jax.ShapeDtypeStruct((C_HEADS, batch, seq, LANE), BF16),
        scratch_shapes=[pltpu.VMEM((n_tiles, ACC_ROWS, TK), BF16), pltpu.VMEM((TK, LANE), BF16),
                        pltpu.VMEM((2, TQ, LANE), BF16), pltpu.VMEM((N_BIAS_TILES, TK, TQ), F32)]
        + 2 * _state_scratch(n_tiles) + 2 * N_PAIR_BUFS * _pair_scratch(),
        compiler_params=_cparams(1),
        name="diff_attn",
    )(ob4, ob4, ob4, of4, bias, lq1, lk1, lq2, lk2, sg)


def _outproj_kernel(a_ref, b_ref, c_ref, w_ref, x_ref, g_ref, o_ref, *, final_norm):
    mixed = jnp.concatenate([a_ref[h] for h in range(A_HEADS)] + [b_ref[h] for h in range(B_HEADS)]
                            + [c_ref[h] for h in range(C_HEADS)], axis=1)
    y = x_ref[...] + jnp.dot(mixed, w_ref[...], preferred_element_type=F32)
    if final_norm:
        y = _rms(y, g_ref[...])
    o_ref[...] = y


def _outproj(mix_a, mix_b, mix_c, w, x2, g, layer, final_norm):
    m = x2.shape[0]
    tm = TM_PROJ
    return pl.pallas_call(
        functools.partial(_outproj_kernel, final_norm=final_norm),
        grid=(m // tm,),
        in_specs=[
            pl.BlockSpec((A_HEADS, tm, LANE), lambda i: (0, i, 0)),
            pl.BlockSpec((B_HEADS, tm, LANE), lambda i: (0, i, 0)),
            pl.BlockSpec((C_HEADS, tm, LANE), lambda i: (0, i, 0)),
            pl.BlockSpec((None, D_MIX, D_MODEL), lambda i: (layer, 0, 0)),
            pl.BlockSpec((tm, D_MODEL), lambda i: (i, 0)),
            pl.BlockSpec((1, D_MODEL), lambda i: (0, 0)),
        ],
        out_specs=pl.BlockSpec((tm, D_MODEL), lambda i: (i, 0)),
        out_shape=jax.ShapeDtypeStruct((m, D_MODEL), F32),
        compiler_params=_cparams(1),
        name="outproj",
    )(mix_a, mix_b, mix_c, w, x2, g)


def kernel(x, norm_g, w_in, b_forget, mla_q_norm_g, w_uq, mla_kv_norm_g, w_ukv, lambda_q1, lambda_k1,
           lambda_q2, lambda_k2, diff_subln_g, rel_bias, w_out, final_norm_g):
    batch, seq, d = x.shape
    assert d == D_MODEL and seq % TQ == 0 and seq % TM_PROJ == 0 and (batch * seq) % TM_IN == 0
    m = batch * seq
    x2 = x.reshape(m, d)
    cs = _proj_col_scale()
    cos, sin = _rope_tables(seq)
    bias_vecs = _t5_bias_vectors(rel_bias)
    w_in_r = _prep_w_in(w_in)
    w_out_b = w_out.astype(BF16)
    for l in range(N_LAYERS):
        ob, of = _inproj(x2, norm_g[l].reshape(1, d), w_in_r, cs, l)
        ob4 = ob.reshape(N_BF_SLABS, batch, seq, LANE)
        of4 = of.reshape(N_F_SLABS, batch, seq, LANE)
        kb = _fox_bias(of, b_forget[l], batch, seq)
        mix_a = _fox_attn(ob4, kb, of4, batch, seq)
        qb, kbm, vb = _mla_prep(of, mla_q_norm_g[l].reshape(1, -1), mla_kv_norm_g[l].reshape(1, -1),
                                _prep_w_uq(w_uq[l]), _prep_w_ukv(w_ukv[l]), cos, sin, seq)
        r4 = lambda a: a.reshape(a.shape[0], batch, seq, a.shape[-1])
        mix_b = _mla_attn(r4(qb), r4(kbm), r4(vb), of4, batch, seq)
        lam_init = 0.8 - 0.6 * math.exp(-0.3 * l)
        mix_c = _diff_attn(ob4, of4, bias_vecs, lambda_q1[l].reshape(1, -1), lambda_k1[l].reshape(1, -1),
                           lambda_q2[l].reshape(1, -1), lambda_k2[l].reshape(1, -1),
                           diff_subln_g[l].reshape(1, -1), lam_init, batch, seq)
        x2 = _outproj(mix_a.reshape(A_HEADS, m, LANE), mix_b.reshape(B_HEADS, m, LANE),
                      mix_c.reshape(C_HEADS, m, LANE), w_out_b, x2,
                      final_norm_g.reshape(1, d), l, final_norm=(l == N_LAYERS - 1))
    return x2.reshape(batch, seq, d)
```

```python
import functools
import math

import numpy as np
import jax
import jax.numpy as jnp
from jax import lax
from jax.experimental import pallas as pl
from jax.experimental.pallas import tpu as pltpu

F32 = jnp.float32
BF16 = jnp.bfloat16

D_MODEL = 2048
N_LAYERS = 2
CHUNK = 64
HEAD_DIM = 128
EPS = 1e-6
A_HEADS = 6
A_DIM = A_HEADS * HEAD_DIM
B_HEADS = 6
B_Q_LORA = 512
B_KV_LORA = 256
B_NOPE = 128
B_ROPE = 64
B_V = 128
B_DIM = B_HEADS * B_V
ROPE_THETA = 10000.0
C_HEADS = 4
C_QK = 64
C_V = 2 * C_QK
C_DIM = C_HEADS * C_V
REL_BUCKETS = 32
REL_MAX_DIST = 128
D_MIX = A_DIM + B_DIM + C_DIM

LANE = 128
V7X_VMEM_BYTES = 64 * 1024 * 1024
VMEM_LIMIT = 56 * 1024 * 1024

LOG2E = math.log2(math.e)
NEG_BIG = -1e30

TN = 768
SLABS_PER_TILE = TN // LANE
N_BF_SLABS = 30
N_F_SLABS = 24
N_BF_TILES = N_BF_SLABS // SLABS_PER_TILE
N_F_TILES = N_F_SLABS // SLABS_PER_TILE
N_PROJ = (N_BF_SLABS + N_F_SLABS) * LANE
SLAB_AQ, SLAB_AK, SLAB_AV, SLAB_CQ, SLAB_CK, SLAB_CV = 0, 6, 12, 18, 22, 26
FSLAB_AGATE, FSLAB_BGATE, FSLAB_CGATE, FSLAB_CQ, FSLAB_CKV, FSLAB_KROPE = 0, 6, 12, 16, 20, 22
AF_LANE = 64

TM_PROJ = 512
TM_IN = 1024
TQ = 512
TK = 512

CHUNK_SHIFT = CHUNK.bit_length() - 1
N_TILE_CHUNKS = TK // CHUNK
FOX_TERM_STRIDE = A_HEADS
FOX_TERMS = 3


def _chunk_onehot(row_chunk, lane, lane0):
    return jnp.where(lane - lane0 == row_chunk, 1.0, 0.0)


def _chunk_maskq(row_chunk, lane, lane0):
    c = lane - lane0
    return jnp.where((c >= 0) & (c < N_TILE_CHUNKS) & (row_chunk < c), NEG_BIG, 0.0)


def _nt_dot(a, b):
    return lax.dot_general(a, b, (((1,), (1,)), ((), ())), preferred_element_type=F32)


def _cparams(n_grid):
    return pltpu.CompilerParams(dimension_semantics=("arbitrary",) * n_grid,
                                vmem_limit_bytes=VMEM_LIMIT)


def _w_in_slab_sources():
    o = 0
    seg = {}
    for name, size in (("a_q", A_DIM), ("a_k", A_DIM), ("a_v", A_DIM), ("a_f", A_HEADS), ("a_gate", A_DIM),
                       ("b_cq", B_Q_LORA), ("b_ckv", B_KV_LORA), ("b_krope", B_ROPE), ("b_gate", B_DIM),
                       ("c_q", C_DIM), ("c_k", C_DIM), ("c_v", C_DIM), ("c_gate", C_DIM)):
        seg[name] = (o, size)
        o += size
    slabs = []
    for name in ("a_q", "a_k", "a_v", "c_q", "c_k", "c_v", "a_gate", "b_gate", "c_gate", "b_cq", "b_ckv"):
        start, size = seg[name]
        slabs += [[(start + i, start + i + LANE, 1)] for i in range(0, size, LANE)]
    kr, half = seg["b_krope"][0], B_ROPE // 2
    af = seg["a_f"][0]
    slabs.append([(kr, kr + B_ROPE, 1), (None, LANE - B_ROPE, 0)])
    slabs.append([(kr + half, kr + B_ROPE, -1), (kr, kr + half, 1), (af, af + A_HEADS, 1),
                  (None, LANE - B_ROPE - A_HEADS, 0)])
    assert len(slabs) * LANE == N_PROJ
    return slabs


N_SPECIAL_SLABS = 2


def _w_in_kernel(tab_ref, w_ref, sp_ref, o_ref, *, n_layers):
    j = pl.program_id(0)
    n_regular = pl.num_programs(0) - N_SPECIAL_SLABS
    k_chunks = o_ref.shape[2] // LANE
    rows_per_col = n_layers * k_chunks

    @pl.when(j < n_regular)
    def _():
        for kc in range(k_chunks):
            for l in range(n_layers):
                rows = w_ref[pl.ds(kc * n_layers + l, LANE, stride=rows_per_col), :]
                o_ref[l, :, kc * LANE:(kc + 1) * LANE] = rows.astype(BF16)

    @pl.when(j >= n_regular)
    def _():
        o_ref[...] = sp_ref[:, pl.ds(pl.multiple_of((j - n_regular) * LANE, LANE), LANE), :].astype(BF16)


def _prep_w_in(w):
    n_layers, d, n = w.shape
    k_chunks = d // LANE
    rows_per_col = n_layers * k_chunks
    view = jnp.transpose(w.reshape(n_layers, k_chunks, LANE, n), (3, 1, 0, 2))
    flat = view.reshape(n * rows_per_col, LANE)
    slabs = _w_in_slab_sources()
    n_regular = len(slabs) - N_SPECIAL_SLABS
    assert all(len(p) == 1 and p[0][2] == 1 for p in slabs[:n_regular])
    tab = jnp.asarray([p[0][0] for p in slabs[:n_regular]] + [0] * N_SPECIAL_SLABS, jnp.int32)

    def cols(a, b):
        part = flat[a * rows_per_col:b * rows_per_col].reshape(b - a, k_chunks, n_layers, LANE)
        return jnp.transpose(part, (2, 0, 1, 3)).reshape(n_layers, b - a, d)

    special = []
    for pieces in slabs[n_regular:]:
        for start, stop, sign in pieces:
            if start is None:
                special.append(jnp.zeros((n_layers, stop, d), w.dtype))
            else:
                special.append(cols(start, stop) if sign > 0 else -cols(start, stop))
    special = jnp.concatenate(special, axis=1)
    return pl.pallas_call(
        functools.partial(_w_in_kernel, n_layers=n_layers),
        grid_spec=pltpu.PrefetchScalarGridSpec(
            num_scalar_prefetch=1,
            grid=(len(slabs),),
            in_specs=[pl.BlockSpec((pl.Element(LANE * rows_per_col), pl.Element(LANE)),
                                   lambda j, tab: (tab[j] * rows_per_col, 0)),
                      pl.BlockSpec(special.shape, lambda j, tab: (0, 0, 0))],
            out_specs=pl.BlockSpec((n_layers, LANE, d), lambda j, tab: (0, j, 0)),
        ),
        out_shape=jax.ShapeDtypeStruct((n_layers, N_PROJ, d), BF16),
        compiler_params=_cparams(1),
        name="w_in_reorder",
    )(tab, flat, special)


def _proj_col_scale():
    s = jnp.ones((N_PROJ,), F32)
    s = s.at[SLAB_AQ * LANE:(SLAB_AQ + A_HEADS) * LANE].set(LOG2E / math.sqrt(HEAD_DIM))
    s = s.at[SLAB_CQ * LANE:(SLAB_CQ + C_HEADS) * LANE].set(LOG2E / math.sqrt(C_QK))
    return s.reshape(1, N_PROJ)


def _prep_w_uq(w):
    w = w.reshape(B_Q_LORA, B_HEADS, B_NOPE + B_ROPE)
    nope = w[:, :, :B_NOPE].reshape(B_Q_LORA, B_HEADS * B_NOPE)
    rope = w[:, :, B_NOPE:]
    half = B_ROPE // 2
    rot = jnp.concatenate([-rope[:, :, half:], rope[:, :, :half]], axis=2)
    both = jnp.concatenate([rope, rot], axis=2).reshape(B_Q_LORA, B_HEADS * LANE)
    return jnp.concatenate([nope, both], axis=1).astype(BF16)


def _prep_w_ukv(w):
    w = w.reshape(B_KV_LORA, B_HEADS, B_NOPE + B_V)
    k = w[:, :, :B_NOPE].reshape(B_KV_LORA, B_HEADS * B_NOPE)
    v = w[:, :, B_NOPE:].reshape(B_KV_LORA, B_HEADS * B_V)
    return jnp.concatenate([k, v], axis=1).astype(BF16)


def _rope_tables(seq):
    half = B_ROPE // 2
    inv = ROPE_THETA ** (-jnp.arange(half, dtype=F32) / half)
    ang = jnp.arange(seq).astype(F32)[:, None] * inv[None, :]
    pad = jnp.zeros((seq, LANE - B_ROPE), F32)
    cos = jnp.concatenate([jnp.cos(ang), jnp.cos(ang), pad], axis=1)
    sin = jnp.concatenate([jnp.sin(ang), jnp.sin(ang), pad], axis=1)
    return cos, sin


def _t5_bucket(rel):
    nb = REL_BUCKETS // 2
    max_exact = nb // 2
    ret = (rel > 0).astype(jnp.int32) * nb
    n = jnp.abs(rel)
    nf = jnp.maximum(n, 1).astype(F32)
    large = max_exact + (jnp.log(nf / max_exact) / math.log(REL_MAX_DIST / max_exact)
                         * (nb - max_exact)).astype(jnp.int32)
    large = jnp.minimum(large, nb - 1)
    return ret + jnp.where(n < max_exact, n, large)


def _t5_bias_vectors(rel_bias):
    assert TQ == TK and TQ > REL_MAX_DIST
    x = jnp.arange(2 * TQ)
    x = jnp.where(x < TQ, x, x - 2 * TQ)
    bucket = jnp.stack([_t5_bucket(-x - delta) for delta in (0, TQ)])
    rel = rel_bias - rel_bias[REL_BUCKETS // 2 - 1][None, :]
    t = jnp.transpose(rel[bucket], (2, 0, 1))
    return (t * LOG2E).reshape(C_HEADS, 2, 1, 2 * TQ)


def _inproj_kernel(x_ref, g_ref, w_ref, cs_ref, ob_ref, of_ref, h_scr):
    j = pl.program_id(1)

    def project(h, out_ref):
        acc = _nt_dot(h, w_ref[...]) * cs_ref[...]
        for s in range(SLABS_PER_TILE):
            out_ref[s] = acc[:, s * LANE:(s + 1) * LANE].astype(out_ref.dtype)

    @pl.when(j == 0)
    def _():
        x = x_ref[...]
        ms = jnp.mean(x * x, axis=-1, keepdims=True)
        h = (x * lax.rsqrt(ms + EPS) * g_ref[...]).astype(BF16)
        h_scr[...] = h
        project(h, ob_ref)

    @pl.when((j > 0) & (j < N_BF_TILES))
    def _():
        project(h_scr[...], ob_ref)

    @pl.when(j >= N_BF_TILES)
    def _():
        project(h_scr[...], of_ref)


def _inproj(x2, g, w, cs, layer):
    m = x2.shape[0]
    tm = TM_IN
    return pl.pallas_call(
        _inproj_kernel,
        grid=(m // tm, N_BF_TILES + N_F_TILES),
        in_specs=[
            pl.BlockSpec((tm, D_MODEL), lambda i, j: (i, 0)),
            pl.BlockSpec((1, D_MODEL), lambda i, j: (0, 0)),
            pl.BlockSpec((None, TN, D_MODEL), lambda i, j: (layer, j, 0)),
            pl.BlockSpec((1, TN), lambda i, j: (0, j)),
        ],
        out_specs=[
            pl.BlockSpec((SLABS_PER_TILE, tm, LANE), lambda i, j: (jnp.minimum(j, N_BF_TILES - 1), i, 0)),
            pl.BlockSpec((SLABS_PER_TILE, tm, LANE), lambda i, j: (jnp.maximum(j - N_BF_TILES, 0), i, 0)),
        ],
        out_shape=[jax.ShapeDtypeStruct((N_BF_SLABS, m, LANE), BF16),
                   jax.ShapeDtypeStruct((N_F_SLABS, m, LANE), F32)],
        scratch_shapes=[pltpu.VMEM((tm, D_MODEL), BF16)],
        compiler_params=_cparams(2),
        name="inproj",
    )(x2, g, w, cs)


def _rms(x, g):
    return x * lax.rsqrt(jnp.mean(x * x, axis=-1, keepdims=True) + EPS) * g


def _mla_prep_kernel(cq_ref, ckv_ref, kr_ref, gq_ref, gkv_ref, wq_ref, wkv_ref, cos_ref, sin_ref,
                     qo_ref, ko_ref, vo_ref, *, n_pos):
    scale = LOG2E / math.sqrt(B_NOPE + B_ROPE)
    cos = cos_ref[...]
    sin = sin_ref[...]
    cq = jnp.concatenate([cq_ref[s] for s in range(B_Q_LORA // LANE)], axis=1)
    q = jnp.dot(_rms(cq, gq_ref[...]).astype(BF16), wq_ref[...], preferred_element_type=F32)
    ckv = jnp.concatenate([ckv_ref[s] for s in range(B_KV_LORA // LANE)], axis=1)
    kv = jnp.dot(_rms(ckv, gkv_ref[...]).astype(BF16), wkv_ref[...], preferred_element_type=F32)
    tm = cos.shape[0]
    pos = (pl.program_id(0) % n_pos) * tm + lax.broadcasted_iota(jnp.int32, (tm, LANE), 0)
    chunk = (pos % TK) >> CHUNK_SHIFT
    lane = lax.broadcasted_iota(jnp.int32, (tm, LANE), 1)
    k_rope = (kr_ref[0] * cos + kr_ref[1] * sin + _chunk_onehot(chunk, lane, B_ROPE)).astype(BF16)
    mask_q = _chunk_maskq(chunk, lane, B_ROPE)
    cos_sin = cos + pltpu.roll(sin, B_ROPE, 1)
    n_h = B_HEADS * LANE
    for h in range(B_HEADS):
        sl = slice(h * LANE, (h + 1) * LANE)
        t = q[:, n_h + h * LANE:n_h + (h + 1) * LANE] * cos_sin
        q_rope = t + pltpu.roll(t, B_ROPE, 1)
        qo_ref[h, :, :LANE] = (q[:, sl] * scale).astype(BF16)
        qo_ref[h, :, LANE:] = jnp.where(lane < B_ROPE, q_rope * scale, mask_q).astype(BF16)
        ko_ref[h, :, :LANE] = kv[:, sl].astype(BF16)
        ko_ref[h, :, LANE:] = k_rope
        vo_ref[h] = kv[:, n_h + h * LANE:n_h + (h + 1) * LANE].astype(BF16)


def _mla_prep(of, gq, gkv, wq, wkv, cos, sin, seq):
    m = of.shape[1]
    tm = TM_PROJ
    n_pos = seq // tm
    nq = B_Q_LORA // LANE
    nkv = B_KV_LORA // LANE
    return pl.pallas_call(
        functools.partial(_mla_prep_kernel, n_pos=n_pos),
        grid=(m // tm,),
        in_specs=[
            pl.BlockSpec((nq, tm, LANE), lambda i: (FSLAB_CQ // nq, i, 0)),
            pl.BlockSpec((nkv, tm, LANE), lambda i: (FSLAB_CKV // nkv, i, 0)),
            pl.BlockSpec((2, tm, LANE), lambda i: (FSLAB_KROPE // 2, i, 0)),
            pl.BlockSpec((1, B_Q_LORA), lambda i: (0, 0)),
            pl.BlockSpec((1, B_KV_LORA), lambda i: (0, 0)),
            pl.BlockSpec(wq.shape, lambda i: (0, 0)),
            pl.BlockSpec(wkv.shape, lambda i: (0, 0)),
            pl.BlockSpec((tm, LANE), lambda i: (i % n_pos, 0)),
            pl.BlockSpec((tm, LANE), lambda i: (i % n_pos, 0)),
        ],
        out_specs=[
            pl.BlockSpec((B_HEADS, tm, 2 * LANE), lambda i: (0, i, 0)),
            pl.BlockSpec((B_HEADS, tm, 2 * LANE), lambda i: (0, i, 0)),
            pl.BlockSpec((B_HEADS, tm, LANE), lambda i: (0, i, 0)),
        ],
        out_shape=[jax.ShapeDtypeStruct((B_HEADS, m, 2 * LANE), BF16),
                   jax.ShapeDtypeStruct((B_HEADS, m, 2 * LANE), BF16),
                   jax.ShapeDtypeStruct((B_HEADS, m, LANE), BF16)],
        compiler_params=_cparams(1),
        name="mla_prep",
    )(of, of, of, gq, gkv, wq, wkv, cos, sin)


CUM_CHUNK = 128


def _fox_bias_kernel(af_ref, bf_ref, kb_ref, cum_scr):
    seq = af_ref.shape[1]
    r = lax.broadcasted_iota(jnp.int32, (CUM_CHUNK, CUM_CHUNK), 0)
    c = lax.broadcasted_iota(jnp.int32, (CUM_CHUNK, CUM_CHUNK), 1)
    tri = jnp.where(r >= c, 1.0, 0.0).astype(BF16)
    bvec = bf_ref[...]

    def body(t, carry):
        r0 = pl.multiple_of(t * CUM_CHUNK, CUM_CHUNK)
        x = af_ref[0, pl.ds(r0, CUM_CHUNK), :] + bvec
        logf = jnp.minimum(x, 0.0) - jnp.log1p(jnp.exp(-jnp.abs(x)))
        cs = carry
        rest = logf
        for _ in range(FOX_TERMS):
            term = rest.astype(BF16)
            rest = rest - term.astype(F32)
            cs = cs + jnp.dot(tri, term, preferred_element_type=F32)
        cum_scr[pl.ds(r0, CUM_CHUNK), :] = cs
        return cs[CUM_CHUNK - 1:CUM_CHUNK, :]

    lax.fori_loop(0, seq // CUM_CHUNK, body, jnp.zeros((1, LANE), F32))
    lane = lax.broadcasted_iota(jnp.int32, (seq, LANE), 1)
    chunk = (lax.broadcasted_iota(jnp.int32, (seq, LANE), 0) % TK) >> CHUNK_SHIFT
    out = _chunk_onehot(chunk, lane, 0)
    rest = cum_scr[...] * (-LOG2E)
    for i in range(FOX_TERMS):
        term = rest.astype(BF16).astype(F32)
        rest = rest - term
        lo = AF_LANE + FOX_TERM_STRIDE * i
        moved = term if i == 0 else pltpu.roll(term, FOX_TERM_STRIDE * i, 1)
        out = jnp.where((lane >= lo) & (lane < lo + A_HEADS), moved, out)
    kb_ref[0] = out.astype(BF16)


def _fox_bias(of, b_forget, batch, seq):
    bvec = jnp.zeros((1, LANE), F32).at[0, AF_LANE:AF_LANE + A_HEADS].set(b_forget)
    of4 = of.reshape(N_F_SLABS, batch, seq, LANE)
    return pl.pallas_call(
        _fox_bias_kernel,
        grid=(batch,),
        in_specs=[
            pl.BlockSpec((None, 1, seq, LANE), lambda b: (N_F_SLABS - 1, b, 0, 0)),
            pl.BlockSpec((1, LANE), lambda b: (0, 0)),
        ],
        out_specs=pl.BlockSpec((1, seq, LANE), lambda b: (b, 0, 0)),
        out_shape=jax.ShapeDtypeStruct((batch, seq, LANE), BF16),
        scratch_shapes=[pltpu.VMEM((seq, LANE), F32)],
        compiler_params=_cparams(1),
        name="fox_bias",
    )(of4, bvec)


PAIR_Q, PAIR_K, PAIR_DIAG, PAIR_BIAS, PAIR_STATE = 0, 1, 2, 3, 4
ACC_ROWS = HEAD_DIM + 16
N_PAIR_BUFS = 6
PIPE_DIST = 4
N_BIAS_TILES = 2


class _PairTable:
    def __init__(self, n_tiles):
        rows = [(qi, kj, int(kj == qi), qi - kj, qi) for qi in range(n_tiles) for kj in range(qi + 1)]
        self.a = np.asarray(rows, np.int64).T
        self.shape = self.a.shape

    def __getitem__(self, idx):
        return int(self.a[idx])


def _tile_rows(ref, tile):
    return ref[pl.ds(pl.multiple_of(tile * TK, TK), TK), :]


def _load_vt(v_ref, vt_scr):
    extra = jnp.where(lax.broadcasted_iota(jnp.int32, (ACC_ROWS - HEAD_DIM, TK), 0) == 0, 1.0, 0.0).astype(BF16)
    for j in range(vt_scr.shape[0]):
        vt_scr[j, :HEAD_DIM, :] = v_ref[j * TK:(j + 1) * TK, :].T
        vt_scr[j, HEAD_DIM:, :] = extra


class _PairBufs:
    def __init__(self, s, mb):
        self.s, self.mb = s, mb


def _pair_scratch():
    return [pltpu.VMEM((TK, TQ), F32), pltpu.VMEM((1, TQ), F32)]


def _state_scratch(n_tiles):
    return [pltpu.VMEM((n_tiles, 1, TQ), F32), pltpu.VMEM((n_tiles, ACC_ROWS, TQ), F32)]


def _init_pipeline(m_all, acc_all):
    m_all[...] = jnp.full(m_all.shape, NEG_BIG, F32)
    acc_all[...] = jnp.zeros(acc_all.shape, F32)


def _pair_bufs(pair_scr):
    return [_PairBufs(*pair_scr[2 * i:2 * i + 2]) for i in range(len(pair_scr) // 2)]


def _qk_store(s, bufs):
    bufs.s[...] = s
    bufs.mb[...] = jnp.max(s, axis=0, keepdims=True)


def _softmax_pv_stage(tab_ref, t, bufs, m_all, vt_scr, acc_all):
    st = tab_ref[PAIR_STATE, t]
    m_prev = m_all[st]
    m_new = jnp.maximum(m_prev, bufs.mb[...])
    alpha = jnp.exp2(m_prev - m_new)
    p = jnp.exp2(bufs.s[...] - m_new).astype(BF16)
    m_all[st] = m_new
    acc_all[st] = alpha * acc_all[st] + jnp.dot(vt_scr[tab_ref[PAIR_K, t]], p, preferred_element_type=F32)


def _run_pipeline(tab, bufs, qk_stage, smpv_stage, finalize):
    u, d = N_PAIR_BUFS, PIPE_DIST
    assert u > d
    n_pairs = tab.shape[1]
    for c in range(min(d, n_pairs)):
        qk_stage(c, bufs[c % u])
    for c in range(n_pairs):
        smpv_stage(c, bufs[c % u])
        if c + d < n_pairs:
            qk_stage(c + d, bufs[(c + d) % u])
        if tab[PAIR_DIAG, c]:
            finalize(tab[PAIR_Q, c])


def _normalized(acc_all, qi):
    return acc_all[qi, :HEAD_DIM, :] * (1.0 / acc_all[qi, HEAD_DIM:HEAD_DIM + 1, :])


def _silu(g):
    return g / (1.0 + jnp.exp(-g))


def _head_spec(width, slab0, n_heads, seq):
    return pl.BlockSpec((None, None, seq, width), lambda g: (slab0 + g % n_heads, g // n_heads, 0, 0))


def _fox_attn_kernel(q_ref, k_ref, kb_ref, v_ref, g_ref, o_ref, vt_scr, qext_scr, m_all, acc_all, *pair_scr):
    tab_ref = _PairTable(m_all.shape[0])
    bufs = _pair_bufs(pair_scr)
    _load_vt(v_ref, vt_scr)
    _init_pipeline(m_all, acc_all)
    row = lax.broadcasted_iota(jnp.int32, (TQ, LANE), 0)
    lane = lax.broadcasted_iota(jnp.int32, (TQ, LANE), 1)
    rel = lane - (AF_LANE + pl.program_id(0) % A_HEADS)
    ones3 = jnp.where((rel >= 0) & (rel < FOX_TERM_STRIDE * FOX_TERMS) & (rel % FOX_TERM_STRIDE == 0), 1.0, 0.0)
    qext_scr[0] = ones3.astype(BF16)
    qext_scr[1] = (ones3 + _chunk_maskq(row >> CHUNK_SHIFT, lane, 0)).astype(BF16)
    sub_key = lax.broadcasted_iota(jnp.int32, (LANE, LANE), 0)
    sub_qry = lax.broadcasted_iota(jnp.int32, (LANE, LANE), 1)

    def qk_stage(t, bufs):
        q_cat = jnp.concatenate([_tile_rows(q_ref, tab_ref[PAIR_Q, t]), qext_scr[tab_ref[PAIR_DIAG, t]]], axis=1)
        kj = tab_ref[PAIR_K, t]
        k_cat = jnp.concatenate([_tile_rows(k_ref, kj), _tile_rows(kb_ref, kj)], axis=1)
        s = _nt_dot(k_cat, q_cat)
        if tab_ref[PAIR_DIAG, t]:
            blocks = []
            for r in range(TK // LANE):
                blk = s[r * LANE:(r + 1) * LANE, :]
                mid = jnp.where(sub_key > sub_qry, NEG_BIG, blk[:, r * LANE:(r + 1) * LANE])
                parts = ([blk[:, :r * LANE]] if r > 0 else []) + [mid]
                parts += [blk[:, (r + 1) * LANE:]] if (r + 1) * LANE < TQ else []
                blocks.append(jnp.concatenate(parts, axis=1))
            s = jnp.concatenate(blocks, axis=0)
        _qk_store(s, bufs)

    def finalize(qi):
        rows = slice(qi * TQ, (qi + 1) * TQ)
        o_ref[rows, :] = (_normalized(acc_all, qi).T * _silu(g_ref[rows, :])).astype(BF16)

    _run_pipeline(tab_ref, bufs, qk_stage,
                           lambda t, bufs: _softmax_pv_stage(tab_ref, t, bufs, m_all, vt_scr, acc_all), finalize)


def _fox_attn(ob4, kb, of4, batch, seq):
    hs = functools.partial(_head_spec, n_heads=A_HEADS, seq=seq)
    n_tiles = seq // TQ
    return pl.pallas_call(
        _fox_attn_kernel,
        grid=(batch * A_HEADS,),
        in_specs=[hs(LANE, SLAB_AQ), hs(LANE, SLAB_AK),
                  pl.BlockSpec((None, seq, LANE), lambda g: (g // A_HEADS, 0, 0)), hs(LANE, SLAB_AV),
                  hs(LANE, FSLAB_AGATE)],
        out_specs=hs(LANE, 0),
        out_shape=jax.ShapeDtypeStruct((A_HEADS, batch, seq, LANE), BF16),
        scratch_shapes=[pltpu.VMEM((n_tiles, ACC_ROWS, TK), BF16), pltpu.VMEM((2, TQ, LANE), BF16)]
        + _state_scratch(n_tiles) + N_PAIR_BUFS * _pair_scratch(),
        compiler_params=_cparams(1),
        name="fox_attn",
    )(ob4, ob4, kb, ob4, of4)


def _mla_attn_kernel(q_ref, k_ref, v_ref, g_ref, o_ref, vt_scr, m_all, acc_all, *pair_scr):
    tab_ref = _PairTable(m_all.shape[0])
    bufs = _pair_bufs(pair_scr)
    _load_vt(v_ref, vt_scr)
    _init_pipeline(m_all, acc_all)
    lane = lax.broadcasted_iota(jnp.int32, (TQ, LANE), 1)

    def qk_stage(t, bufs):
        q = _tile_rows(q_ref, tab_ref[PAIR_Q, t])
        first_off = LANE if tab_ref[PAIR_DIAG, t] else B_ROPE
        q_hi = jnp.where(lane >= first_off, jnp.zeros((), BF16), q[:, LANE:])
        q_cat = jnp.concatenate([q[:, :LANE], q_hi], axis=1)
        _qk_store(_nt_dot(_tile_rows(k_ref, tab_ref[PAIR_K, t]), q_cat), bufs)

    def finalize(qi):
        rows = slice(qi * TQ, (qi + 1) * TQ)
        o_ref[rows, :] = (_normalized(acc_all, qi).T * _silu(g_ref[rows, :])).astype(BF16)

    _run_pipeline(tab_ref, bufs, qk_stage,
                           lambda t, bufs: _softmax_pv_stage(tab_ref, t, bufs, m_all, vt_scr, acc_all), finalize)


def _mla_attn(q4, k4, v4, of4, batch, seq):
    hs = functools.partial(_head_spec, n_heads=B_HEADS, seq=seq)
    n_tiles = seq // TQ
    return pl.pallas_call(
        _mla_attn_kernel,
        grid=(batch * B_HEADS,),
        in_specs=[hs(2 * LANE, 0), hs(2 * LANE, 0), hs(LANE, 0), hs(LANE, FSLAB_BGATE)],
        out_specs=hs(LANE, 0),
        out_shape=jax.ShapeDtypeStruct((B_HEADS, batch, seq, LANE), BF16),
        scratch_shapes=[pltpu.VMEM((n_tiles, ACC_ROWS, TK), BF16)]
        + _state_scratch(n_tiles) + N_PAIR_BUFS * _pair_scratch(),
        compiler_params=_cparams(1),
        name="mla_attn",
    )(q4, k4, v4, of4)


def _diff_attn_kernel(q_ref, k_ref, v_ref, g_ref, bvec_ref, lq1_ref, lk1_ref, lq2_ref, lk2_ref, sg_ref,
                      o_ref, vt_scr, kext_scr, qext_scr, bias_scr, m1_all, acc1_all, m2_all, acc2_all, *pair_scr,
                      lam_init):
    tab_ref = _PairTable(m1_all.shape[0])
    flat = _pair_bufs(pair_scr)
    bufs = [(flat[2 * i], flat[2 * i + 1]) for i in range(N_PAIR_BUFS)]
    states = ((m1_all, acc1_all), (m2_all, acc2_all))
    _load_vt(v_ref, vt_scr)
    for m_all, acc_all in states:
        _init_pipeline(m_all, acc_all)
    for i in range(N_BIAS_TILES):
        gen = jnp.broadcast_to(bvec_ref[i], (TK, 2 * TQ))
        bias_scr[i] = pltpu.roll(gen, 0, 1, stride=1, stride_axis=0)[:, :TQ]
    row = lax.broadcasted_iota(jnp.int32, (TQ, LANE), 0)
    lane = lax.broadcasted_iota(jnp.int32, (TQ, LANE), 1)
    kext_scr[...] = _chunk_onehot(row >> CHUNK_SHIFT, lane, 0).astype(BF16)
    qext_scr[0] = jnp.zeros((TQ, LANE), BF16)
    qext_scr[1] = _chunk_maskq(row >> CHUNK_SHIFT, lane, 0).astype(BF16)
    lam = (jnp.exp(jnp.sum(lq1_ref[...] * lk1_ref[...], axis=-1, keepdims=True))
           - jnp.exp(jnp.sum(lq2_ref[...] * lk2_ref[...], axis=-1, keepdims=True)) + lam_init)

    def qk_stage(t, bufs2):
        q = _tile_rows(q_ref, tab_ref[PAIR_Q, t])
        qext = qext_scr[tab_ref[PAIR_DIAG, t]]
        zero = jnp.zeros_like(q)
        k_cat = jnp.concatenate([_tile_rows(k_ref, tab_ref[PAIR_K, t]), kext_scr[...]], axis=1)
        near = tab_ref[PAIR_BIAS, t] < N_BIAS_TILES
        for bufs, q_map in zip(bufs2, (jnp.where(lane < C_QK, q, zero), jnp.where(lane >= C_QK, q, zero))):
            s = _nt_dot(k_cat, jnp.concatenate([q_map, qext], axis=1))
            _qk_store(s + bias_scr[tab_ref[PAIR_BIAS, t]] if near else s, bufs)

    def smpv_stage(t, bufs2):
        for (m_all, acc_all), bufs in zip(states, bufs2):
            _softmax_pv_stage(tab_ref, t, bufs, m_all, vt_scr, acc_all)

    def finalize(qi):
        rows = slice(qi * TQ, (qi + 1) * TQ)
        o = _normalized(acc1_all, qi) - lam * _normalized(acc2_all, qi)
        o = o * lax.rsqrt(jnp.mean(o * o, axis=0, keepdims=True) + EPS)
        out = o.T * sg_ref[...] * (1.0 - lam_init)
        o_ref[rows, :] = (out * _silu(g_ref[rows, :])).astype(BF16)

    _run_pipeline(tab_ref, bufs, qk_stage, smpv_stage, finalize)


def _diff_attn(ob4, of4, bias, lq1, lk1, lq2, lk2, sg, lam_init, batch, seq):
    hs = functools.partial(_head_spec, n_heads=C_HEADS, seq=seq)
    vec = lambda n: pl.BlockSpec((1, n), lambda g: (0, 0))
    n_tiles = seq // TQ
    return pl.pallas_call(
        functools.partial(_diff_attn_kernel, lam_init=lam_init),
        grid=(batch * C_HEADS,),
        in_specs=[hs(LANE, SLAB_CQ), hs(LANE, SLAB_CK), hs(LANE, SLAB_CV), hs(LANE, FSLAB_CGATE),
                  pl.BlockSpec((None, 2, 1, 2 * TQ), lambda g: (g % C_HEADS, 0, 0, 0)),
                  vec(C_QK), vec(C_QK), vec(C_QK), vec(C_QK), vec(C_V)],
        out_specs=hs(LANE, 0),
        out_shape=jax.ShapeDtypeStruct((C_HEADS, batch, seq, LANE), BF16),
        scratch_shapes=[pltpu.VMEM((n_tiles, ACC_ROWS, TK), BF16), pltpu.VMEM((TK, LANE), BF16),
                        pltpu.VMEM((2, TQ, LANE), BF16), pltpu.VMEM((N_BIAS_TILES, TK, TQ), F32)]
        + 2 * _state_scratch(n_tiles) + 2 * N_PAIR_BUFS * _pair_scratch(),
        compiler_params=_cparams(1),
        name="diff_attn",
    )(ob4, ob4, ob4, of4, bias, lq1, lk1, lq2, lk2, sg)


def _outproj_kernel(a_ref, b_ref, c_ref, w_ref, x_ref, g_ref, o_ref, w_scr, *, final_norm):
    @pl.when(pl.program_id(0) == 0)
    def _():
        for r in range(0, D_MIX, LANE):
            w_scr[r:r + LANE, :] = w_ref[r:r + LANE, :].astype(BF16)

    mixed = jnp.concatenate([a_ref[h] for h in range(A_HEADS)] + [b_ref[h] for h in range(B_HEADS)]
                            + [c_ref[h] for h in range(C_HEADS)], axis=1)
    y = x_ref[...] + jnp.dot(mixed, w_scr[...], preferred_element_type=F32)
    if final_norm:
        y = _rms(y, g_ref[...])
    o_ref[...] = y


def _outproj(mix_a, mix_b, mix_c, w, x2, g, layer, final_norm):
    m = x2.shape[0]
    tm = TM_PROJ
    return pl.pallas_call(
        functools.partial(_outproj_kernel, final_norm=final_norm),
        grid=(m // tm,),
        in_specs=[
            pl.BlockSpec((A_HEADS, tm, LANE), lambda i: (0, i, 0)),
            pl.BlockSpec((B_HEADS, tm, LANE), lambda i: (0, i, 0)),
            pl.BlockSpec((C_HEADS, tm, LANE), lambda i: (0, i, 0)),
            pl.BlockSpec((None, D_MIX, D_MODEL), lambda i: (layer, 0, 0), pipeline_mode=pl.Buffered(1)),
            pl.BlockSpec((tm, D_MODEL), lambda i: (i, 0)),
            pl.BlockSpec((1, D_MODEL), lambda i: (0, 0)),
        ],
        out_specs=pl.BlockSpec((tm, D_MODEL), lambda i: (i, 0)),
        out_shape=jax.ShapeDtypeStruct((m, D_MODEL), F32),
        scratch_shapes=[pltpu.VMEM((D_MIX, D_MODEL), BF16)],
        compiler_params=_cparams(1),
        name="outproj",
    )(mix_a, mix_b, mix_c, w, x2, g)


def kernel(x, norm_g, w_in, b_forget, mla_q_norm_g, w_uq, mla_kv_norm_g, w_ukv, lambda_q1, lambda_k1,
           lambda_q2, lambda_k2, diff_subln_g, rel_bias, w_out, final_norm_g):
    batch, seq, d = x.shape
    assert d == D_MODEL and seq % TQ == 0 and seq % TM_PROJ == 0 and (batch * seq) % TM_IN == 0
    m = batch * seq
    x2 = x.reshape(m, d)
    cs = _proj_col_scale()
    cos, sin = _rope_tables(seq)
    bias_vecs = _t5_bias_vectors(rel_bias)
    w_in_r = _prep_w_in(w_in)
    for l in range(N_LAYERS):
        ob, of = _inproj(x2, norm_g[l].reshape(1, d), w_in_r, cs, l)
        ob4 = ob.reshape(N_BF_SLABS, batch, seq, LANE)
        of4 = of.reshape(N_F_SLABS, batch, seq, LANE)
        kb = _fox_bias(of, b_forget[l], batch, seq)
        mix_a = _fox_attn(ob4, kb, of4, batch, seq)
        qb, kbm, vb = _mla_prep(of, mla_q_norm_g[l].reshape(1, -1), mla_kv_norm_g[l].reshape(1, -1),
                                _prep_w_uq(w_uq[l]), _prep_w_ukv(w_ukv[l]), cos, sin, seq)
        r4 = lambda a: a.reshape(a.shape[0], batch, seq, a.shape[-1])
        mix_b = _mla_attn(r4(qb), r4(kbm), r4(vb), of4, batch, seq)
        lam_init = 0.8 - 0.6 * math.exp(-0.3 * l)
        mix_c = _diff_attn(ob4, of4, bias_vecs, lambda_q1[l].reshape(1, -1), lambda_k1[l].reshape(1, -1),
                           lambda_q2[l].reshape(1, -1), lambda_k2[l].reshape(1, -1),
                           diff_subln_g[l].reshape(1, -1), lam_init, batch, seq)
        x2 = _outproj(mix_a.reshape(A_HEADS, m, LANE), mix_b.reshape(B_HEADS, m, LANE),
                      mix_c.reshape(C_HEADS, m, LANE), w_out, x2,
                      final_norm_g.reshape(1, d), l, final_norm=(l == N_LAYERS - 1))
    return x2.reshape(batch, seq, d)
```
